```python
import jax, jax.numpy as jnp
from jax import lax
import numpy as np

D_MODEL = 1024
BATCH = 2
SEQ = 16384
DEPTH = 1
DEC_BATCH = 128
DEC_SEQ = 4
PAST_LEN = 8192
PAGE_SIZE = 128

N_HEADS = 8
HEAD_DIM = 64
N_KV_HEADS = 2
ATTN_WIDTH = N_HEADS * HEAD_DIM
KV_WIDTH = N_KV_HEADS * HEAD_DIM
ROT_DIV = 4
ROPE_THETA = 500000.0
IDX_HEADS = 4
IDX_DIM = 64
IDX_SCALE = (IDX_HEADS * IDX_DIM) ** -0.5
TOPK_MAX = 256
CONV_CH = D_MODEL // 2
CONV_K = 3
D_MIX = ATTN_WIDTH + CONV_CH
D_FF = 2 * D_MODEL
N_SUB = 3
Q_BLOCK = 128
RMS_EPS = 1e-6
MIX_SPLITS = [ATTN_WIDTH, KV_WIDTH, KV_WIDTH, IDX_HEADS * IDX_DIM, IDX_DIM, IDX_HEADS, CONV_CH, CONV_CH, CONV_CH]
D_IN = sum(MIX_SPLITS)

kernel_name = "hymba_dsa_shortconv_macaron_decoder_step"


def rms_norm(x, g):
    xf = x.astype(jnp.float32)
    y = xf * lax.rsqrt(jnp.mean(xf * xf, axis=-1, keepdims=True) + RMS_EPS)
    return y.astype(x.dtype) * g


def rope(x, pos):
    dh = x.shape[-1]
    rot = dh // ROT_DIV
    half = rot // 2
    inv = jnp.power(ROPE_THETA, -jnp.arange(half, dtype=jnp.float32) * 2.0 / rot)
    ang = pos.astype(jnp.float32)[:, None] * inv[None, :]
    cos = jnp.cos(ang)[:, None, :]
    sin = jnp.sin(ang)[:, None, :]
    xf = x.astype(jnp.float32)
    x1 = xf[..., :half]
    x2 = xf[..., half:rot]
    out = jnp.concatenate([x1 * cos - x2 * sin, x2 * cos + x1 * sin, xf[..., rot:]], axis=-1)
    return out.astype(x.dtype)


def split_mix(z):
    offs = [int(o) for o in np.cumsum(MIX_SPLITS)[:-1]]
    return jnp.split(z, offs, axis=-1)


def project_mix(h, w_in, pos):
    N, T, _ = h.shape
    q, k, v, qi, ki, wi, gb, gc, xin = split_mix(h @ w_in)
    q = rope(q.reshape(N, T, N_HEADS, HEAD_DIM), pos)
    k = rope(k.reshape(N, T, N_KV_HEADS, HEAD_DIM), pos)
    v = v.reshape(N, T, N_KV_HEADS, HEAD_DIM)
    qi = rope(qi.reshape(N, T, IDX_HEADS, IDX_DIM), pos)
    ki = rope(ki[:, :, None, :], pos)[:, :, 0, :]
    wi = wi * IDX_SCALE
    return q, k, v, qi, ki, wi, gb, gc * xin


def indexer_select(qi, wi, ki, qpos, topk):
    s = jnp.einsum('nthd,nld->nthl', qi.astype(jnp.float32), ki.astype(jnp.float32))
    score = jnp.einsum('nthl,nth->ntl', jax.nn.relu(s), wi.astype(jnp.float32))
    kpos = jnp.arange(ki.shape[1])
    admissible = kpos[None, :] <= qpos[:, None]
    score = jnp.where(admissible[None], score, -jnp.inf)
    _, idx = lax.top_k(score, topk)
    valid = idx <= qpos[None, :, None]
    return idx, valid


def sparse_attend(q, kg, vg, valid):
    N, T, H, Dh = q.shape
    qg = q.reshape(N, T, N_KV_HEADS, H // N_KV_HEADS, Dh).astype(jnp.float32)
    s = jnp.einsum('ntgrd,ntkgd->ntgrk', qg, kg.astype(jnp.float32)) * (Dh ** -0.5)
    s = jnp.where(valid[:, :, None, None, :], s, -jnp.inf)
    p = jax.nn.softmax(s, axis=-1)
    o = jnp.einsum('ntgrk,ntkgd->ntgrd', p, vg.astype(jnp.float32))
    return o.reshape(N, T, H * Dh).astype(q.dtype)


def short_conv(u, buf, w_conv):
    T = u.shape[1]
    ext = jnp.concatenate([buf, u], axis=1)
    y = w_conv[0] * ext[:, 0:T]
    for j in range(1, CONV_K):
        y = y + w_conv[j] * ext[:, j:j + T]
    return y, ext[:, T:]


def prompt_mixer(h, w_in, w_conv):
    N, S, _ = h.shape
    pos = jnp.arange(S)
    q, k, v, qi, ki, wi, gb, u = project_mix(h, w_in, pos)
    topk = min(TOPK_MAX, S // 4)
    bidx = jnp.arange(N)[:, None, None]

    def block(i):
        t0 = i * Q_BLOCK
        qb = lax.dynamic_slice_in_dim(q, t0, Q_BLOCK, axis=1)
        qib = lax.dynamic_slice_in_dim(qi, t0, Q_BLOCK, axis=1)
        wib = lax.dynamic_slice_in_dim(wi, t0, Q_BLOCK, axis=1)
        qpos = t0 + jnp.arange(Q_BLOCK)
        idx, valid = indexer_select(qib, wib, ki, qpos, topk)
        return sparse_attend(qb, k[bidx, idx], v[bidx, idx], valid)

    o = lax.map(block, jnp.arange(S // Q_BLOCK))
    o = jnp.swapaxes(o, 0, 1).reshape(N, S, ATTN_WIDTH)
    y_conv, buf = short_conv(u, jnp.zeros((N, CONV_K - 1, CONV_CH), u.dtype), w_conv)
    out = jnp.concatenate([o, gb * y_conv], axis=-1)
    return out, (k, v, ki, buf)


def sample_mixer(h, cache_k, cache_v, cache_idx_k, state_conv, page_table, w_in, w_conv):
    N, T, _ = h.shape
    past = page_table.shape[1] * PAGE_SIZE
    pos = past + jnp.arange(T)
    q, k, v, qi, ki, wi, gb, u = project_mix(h, w_in, pos)
    ki_past = cache_idx_k[page_table].reshape(N, past, IDX_DIM)
    ki_all = jnp.concatenate([ki_past, ki], axis=1)
    topk = min(TOPK_MAX, (past + T) // 4)
    idx, valid = indexer_select(qi, wi, ki_all, pos, topk)
    bidx = jnp.arange(N)[:, None, None]
    is_past = (idx < past)[..., None, None]
    pi = jnp.minimum(idx, past - 1)
    phys = page_table[bidx, pi // PAGE_SIZE]
    off = pi % PAGE_SIZE
    ni = jnp.clip(idx - past, 0, T - 1)
    kg = jnp.where(is_past, cache_k[phys, off], k[bidx, ni])
    vg = jnp.where(is_past, cache_v[phys, off], v[bidx, ni])
    o = sparse_attend(q, kg, vg, valid)
    y_conv, buf = short_conv(u, state_conv, w_conv)
    out = jnp.concatenate([o, gb * y_conv], axis=-1)
    return out, (k, v, ki, buf)


def swiglu(h, w13, w2):
    a, b = jnp.split(h @ w13, 2, axis=-1)
    return (jax.nn.silu(a) * b) @ w2


def decoder_layer(x, c, mixer, w_ada, b_ada, g_pre, g_post, ffn1_w13, ffn1_w2, w_o, ffn2_w13, ffn2_w2):
    mod = (jax.nn.silu(c) @ w_ada + b_ada).reshape(c.shape[0], N_SUB, 3, D_MODEL)

    def pre(x, j):
        return rms_norm(x, g_pre[j]) * (1 + mod[:, j, 1][:, None, :]) + mod[:, j, 0][:, None, :]

    def post(x, y, j, alpha):
        return x + alpha * mod[:, j, 2][:, None, :] * rms_norm(y, g_post[j])

    x = post(x, swiglu(pre(x, 0), ffn1_w13, ffn1_w2), 0, 0.5)
    mixed, new_state = mixer(pre(x, 1))
    x = post(x, mixed @ w_o, 1, 1.0)
    x = post(x, swiglu(pre(x, 2), ffn2_w13, ffn2_w2), 2, 0.5)
    return x, new_state


def setup_inputs(seed: int = 0) -> dict:
    key = jax.random.key(seed)
    ks = jax.random.split(key, 24)
    n_pages = PAST_LEN // PAGE_SIZE
    n_pool = (DEC_BATCH * n_pages * 5) // 4
    perm = jax.random.permutation(ks[0], n_pool)
    page_table = perm[:DEC_BATCH * n_pages].reshape(DEC_BATCH, n_pages).astype(jnp.int32)
    nrm = lambda k, shape, s=1.0: jax.random.normal(k, shape, jnp.float32) * s
    return {
        "x_prompt": nrm(ks[1], (BATCH, SEQ, D_MODEL)),
        "x_sample": nrm(ks[2], (DEC_BATCH, DEC_SEQ, D_MODEL)),
        "cache_k": nrm(ks[3], (DEPTH, n_pool, PAGE_SIZE, N_KV_HEADS, HEAD_DIM)),
        "cache_v": nrm(ks[4], (DEPTH, n_pool, PAGE_SIZE, N_KV_HEADS, HEAD_DIM)),
        "cache_idx_k": nrm(ks[5], (DEPTH, n_pool, PAGE_SIZE, IDX_DIM)),
        "state_conv": nrm(ks[6], (DEPTH, DEC_BATCH, CONV_K - 1, CONV_CH)),
        "page_table": page_table,
        "c_prompt": nrm(ks[7], (BATCH, D_MODEL)),
        "c_sample": nrm(ks[8], (DEC_BATCH, D_MODEL)),
        "w_ada": nrm(ks[9], (DEPTH, D_MODEL, N_SUB * 3 * D_MODEL), 0.5 * D_MODEL ** -0.5),
        "b_ada": nrm(ks[10], (DEPTH, N_SUB * 3 * D_MODEL), 0.02),
        "g_pre": 1.0 + nrm(ks[11], (DEPTH, N_SUB, D_MODEL), 0.02),
        "g_post": 1.0 + nrm(ks[12], (DEPTH, N_SUB, D_MODEL), 0.02),
        "ffn1_w13": nrm(ks[13], (DEPTH, D_MODEL, 2 * D_FF), D_MODEL ** -0.5),
        "ffn1_w2": nrm(ks[14], (DEPTH, D_FF, D_MODEL), D_FF ** -0.5),
        "w_in": nrm(ks[15], (DEPTH, D_MODEL, D_IN), D_MODEL ** -0.5),
        "w_conv": nrm(ks[16], (DEPTH, CONV_K, CONV_CH), CONV_K ** -0.5),
        "w_o": nrm(ks[17], (DEPTH, D_MIX, D_MODEL), D_MIX ** -0.5),
        "ffn2_w13": nrm(ks[18], (DEPTH, D_MODEL, 2 * D_FF), D_MODEL ** -0.5),
        "ffn2_w2": nrm(ks[19], (DEPTH, D_FF, D_MODEL), D_FF ** -0.5),
    }


def reference(x_prompt, x_sample, cache_k, cache_v, cache_idx_k, state_conv, page_table, c_prompt, c_sample,
              w_ada, b_ada, g_pre, g_post, ffn1_w13, ffn1_w2, w_in, w_conv, w_o, ffn2_w13, ffn2_w2):
    yp, ys = x_prompt, x_sample
    kp_l, vp_l, ip_l, cp_l, ks_l, vs_l, is_l, cs_l = [], [], [], [], [], [], [], []
    for l in range(DEPTH):
        lw = (w_ada[l], b_ada[l], g_pre[l], g_post[l], ffn1_w13[l], ffn1_w2[l], w_o[l], ffn2_w13[l], ffn2_w2[l])
        yp, (kp, vp, ip, cp) = decoder_layer(
            yp, c_prompt, lambda h: prompt_mixer(h, w_in[l], w_conv[l]), *lw)
        ys, (kn, vn, inn, cn) = decoder_layer(
            ys, c_sample,
            lambda h: sample_mixer(h, cache_k[l], cache_v[l], cache_idx_k[l], state_conv[l], page_table,
                                   w_in[l], w_conv[l]), *lw)
        kp_l.append(kp); vp_l.append(vp); ip_l.append(ip); cp_l.append(cp)
        ks_l.append(kn); vs_l.append(vn); is_l.append(inn); cs_l.append(cn)
    return (yp, ys, jnp.stack(kp_l), jnp.stack(vp_l), jnp.stack(ip_l), jnp.stack(cp_l),
            jnp.stack(ks_l), jnp.stack(vs_l), jnp.stack(is_l), jnp.stack(cs_l))
```

```python
import functools

import numpy as np
import jax
import jax.numpy as jnp
from jax import lax
from jax.experimental import pallas as pl
from jax.experimental.pallas import tpu as pltpu

F32 = jnp.float32
BF16 = jnp.bfloat16
I32 = jnp.int32

N_HEADS = 8
HEAD_DIM = 64
N_KV_HEADS = 2
HEADS_PER_KV = N_HEADS // N_KV_HEADS
ATTN_WIDTH = N_HEADS * HEAD_DIM
KV_WIDTH = N_KV_HEADS * HEAD_DIM
ROT_DIV = 4
ROT = HEAD_DIM // ROT_DIV
ROT_HALF = ROT // 2
ROPE_THETA = 500000.0
IDX_HEADS = 4
IDX_DIM = 64
IDX_SCALE = (IDX_HEADS * IDX_DIM) ** -0.5
TOPK_MAX = 256
CONV_K = 3
N_SUB = 3
RMS_EPS = 1e-6

LANES = 128
SUBLANES = 8
VMEM_LIMIT = 56 * 1024 * 1024

NEG_BIG = -1e30
F32_MAX = float(np.finfo(np.float32).max)
INT_MIN = -(2 ** 31)
INT_MAX = 2 ** 31 - 1

NT_DIMS = (((1,), (1,)), ((), ()))


def _cparams(sem):
    return pltpu.CompilerParams(dimension_semantics=sem, vmem_limit_bytes=VMEM_LIMIT)


def _silu(x):
    return x * (1.0 / (1.0 + jnp.exp(-x)))


def _rms(x):
    return x * lax.rsqrt(jnp.mean(x * x, axis=-1, keepdims=True) + RMS_EPS)


def _tile_lanes(x, reps):
    return x if reps == 1 else jnp.concatenate([x] * reps, axis=1)


def _ada_kernel(c_ref, w_ref, b_ref, o_ref):
    a = _silu(c_ref[...]).astype(BF16)
    o_ref[...] = jnp.dot(a, w_ref[...].astype(BF16), preferred_element_type=F32) + b_ref[...]


def _ada_mod(c, w_ada, b_ada):
    n, d = c.shape
    nout = w_ada.shape[1]
    tn = 9 * LANES if nout % (9 * LANES) == 0 else LANES
    return pl.pallas_call(
        _ada_kernel,
        grid=(nout // tn,),
        in_specs=[pl.BlockSpec((n, d), lambda j: (0, 0)),
                  pl.BlockSpec((d, tn), lambda j: (0, j)),
                  pl.BlockSpec((1, tn), lambda j: (0, j))],
        out_specs=pl.BlockSpec((n, tn), lambda j: (0, j)),
        out_shape=jax.ShapeDtypeStruct((n, nout), F32),
        compiler_params=_cparams(("arbitrary",)),
        name="ada_mod",
    )(c, w_ada, b_ada.reshape(1, nout))


def _ffn_kernel(x_ref, sh_ref, sc_ref, gt_ref, gpre_ref, gpost_ref, w13_ref, w2_ref, o_ref, *, d_ff, chunk):
    x = x_ref[...]
    h = _rms(x) * gpre_ref[...] * (1.0 + sc_ref[0]) + sh_ref[0]
    hb = h.astype(BF16)
    y = jnp.zeros(x.shape, F32)
    for c in range(d_ff // chunk):
        a = jnp.dot(hb, w13_ref[:, c * chunk:(c + 1) * chunk], preferred_element_type=F32)
        b = jnp.dot(hb, w13_ref[:, d_ff + c * chunk:d_ff + (c + 1) * chunk], preferred_element_type=F32)
        g = (_silu(a) * b).astype(BF16)
        y = y + jnp.dot(g, w2_ref[c * chunk:(c + 1) * chunk, :], preferred_element_type=F32)
    o_ref[...] = x + 0.5 * gt_ref[0] * (_rms(y) * gpost_ref[...])


def _mod_spec(mod, tm, tiles_per_group):
    _, r, d = mod.shape
    return pl.BlockSpec((1, r, d), lambda i: (i // tiles_per_group, 0, 0))


def _ffn(x, shift, scale, gate, g_pre, g_post, w13, w2, tm, tiles_per_group):
    rows, d = x.shape
    d_ff = w2.shape[0]
    const = lambda i: (0, 0)
    return pl.pallas_call(
        functools.partial(_ffn_kernel, d_ff=d_ff, chunk=min(512, d_ff)),
        grid=(rows // tm,),
        in_specs=[pl.BlockSpec((tm, d), lambda i: (i, 0)),
                  _mod_spec(shift, tm, tiles_per_group),
                  _mod_spec(scale, tm, tiles_per_group),
                  _mod_spec(gate, tm, tiles_per_group),
                  pl.BlockSpec((1, d), const),
                  pl.BlockSpec((1, d), const),
                  pl.BlockSpec(w13.shape, const),
                  pl.BlockSpec(w2.shape, const)],
        out_specs=pl.BlockSpec((tm, d), lambda i: (i, 0)),
        out_shape=jax.ShapeDtypeStruct((rows, d), F32),
        compiler_params=_cparams(("arbitrary",)),
        name="ffn",
    )(x, shift, scale, gate, g_pre, g_post, w13, w2)


_OFF_Q = 0
_OFF_K = _OFF_Q + ATTN_WIDTH
_OFF_V = _OFF_K + KV_WIDTH
_OFF_QI = _OFF_V + KV_WIDTH
_OFF_KIWI = _OFF_QI + IDX_HEADS * IDX_DIM
_OFF_GB = _OFF_KIWI + LANES


def _pack_w_in(w_in, conv_ch):
    splits = [ATTN_WIDTH, KV_WIDTH, KV_WIDTH, IDX_HEADS * IDX_DIM, IDX_DIM, IDX_HEADS, conv_ch, conv_ch, conv_ch]
    offs = np.cumsum([0] + splits)
    d = w_in.shape[0]
    pad = jnp.zeros((d, LANES - IDX_DIM - IDX_HEADS), w_in.dtype)
    return jnp.concatenate([w_in[:, :offs[6]], pad, w_in[:, offs[6]:]], axis=1).astype(BF16)


def _rope_tables(pos):
    inv = jnp.power(ROPE_THETA, -jnp.arange(ROT_HALF, dtype=F32) * 2.0 / ROT)
    ang = pos.astype(F32)[:, None] * inv[None, :]
    cos, sin = jnp.cos(ang), jnp.sin(ang)
    n = pos.shape[0]
    one = jnp.ones((n, HEAD_DIM - ROT), F32)
    zero_r = jnp.zeros((n, HEAD_DIM - ROT), F32)
    zero_h = jnp.zeros((n, ROT_HALF), F32)
    c = jnp.concatenate([cos, cos, one], axis=1)
    s1 = jnp.concatenate([-sin, zero_h, zero_r], axis=1)
    s2 = jnp.concatenate([zero_h, sin, zero_r], axis=1)
    rep = LANES // HEAD_DIM
    return jnp.tile(c, (1, rep)), jnp.tile(s1, (1, rep)), jnp.tile(s2, (1, rep))


def _rope128(x, c, s1, s2):
    return x * c + pltpu.roll(x, LANES - ROT_HALF, 1) * s1 + pltpu.roll(x, ROT_HALF, 1) * s2


def _mix_in_kernel(x_ref, sh_ref, sc_ref, gpre_ref, w_ref, rc_ref, rs1_ref, rs2_ref,
                   q_ref, k_ref, v_ref, qi_ref, kiwi_ref, gb_ref, u_ref, *, conv_ch):
    x = x_ref[...]
    hb = (_rms(x) * gpre_ref[...] * (1.0 + sc_ref[0]) + sh_ref[0]).astype(BF16)
    c, s1, s2 = rc_ref[...], rs1_ref[...], rs2_ref[...]

    def proj(off, width):
        return jnp.dot(hb, w_ref[:, off:off + width], preferred_element_type=F32)

    def rope_cols(z):
        return [_rope128(z[:, g * LANES:(g + 1) * LANES], c, s1, s2) for g in range(z.shape[1] // LANES)]

    lane = lax.broadcasted_iota(I32, (x.shape[0], LANES), 1)
    low = lane < HEAD_DIM

    zq = proj(_OFF_Q, ATTN_WIDTH)
    for g, r in enumerate(rope_cols(zq)):
        r = r * (HEAD_DIM ** -0.5)
        swapped = pltpu.roll(r, HEAD_DIM, 1)
        for e in range(2):
            h = 2 * g + e
            kv_group = h // HEADS_PER_KV
            src = r if e == kv_group else swapped
            keep = low if kv_group == 0 else jnp.logical_not(low)
            q_ref[:, h * LANES:(h + 1) * LANES] = jnp.where(keep, src, 0.0).astype(BF16)
    zk = proj(_OFF_K, KV_WIDTH)
    for g, r in enumerate(rope_cols(zk)):
        k_ref[:, g * LANES:(g + 1) * LANES] = r
    v_ref[...] = proj(_OFF_V, KV_WIDTH)
    zqi = proj(_OFF_QI, IDX_HEADS * IDX_DIM)
    for g, r in enumerate(rope_cols(zqi)):
        qi_ref[:, (2 * g) * LANES:(2 * g + 1) * LANES] = jnp.where(low, r, 0.0).astype(BF16)
        qi_ref[:, (2 * g + 1) * LANES:(2 * g + 2) * LANES] = jnp.where(low, 0.0, r).astype(BF16)
    zkw = proj(_OFF_KIWI, LANES)
    kiwi_ref[...] = jnp.where(lane < IDX_DIM, _rope128(zkw, c, s1, s2), zkw * IDX_SCALE)
    gb_ref[...] = proj(_OFF_GB, conv_ch)
    u_ref[...] = proj(_OFF_GB + conv_ch, conv_ch) * proj(_OFF_GB + 2 * conv_ch, conv_ch)


def _mix_in(x, shift, scale, g_pre, w_in_p, rope_tabs, tm, tiles_per_group, rope_tiles, conv_ch):
    rows, d = x.shape
    const = lambda i: (0, 0)
    row = lambda i: (i, 0)
    rope_spec = pl.BlockSpec((tm, LANES), lambda i: (i % rope_tiles, 0))
    assert 2 * HEAD_DIM == LANES and KV_WIDTH == LANES and 2 * IDX_DIM == LANES
    widths = [(N_HEADS * LANES, BF16), (KV_WIDTH, F32), (KV_WIDTH, F32), (IDX_HEADS * LANES, BF16),
              (LANES, F32), (conv_ch, F32), (conv_ch, F32)]
    return pl.pallas_call(
        functools.partial(_mix_in_kernel, conv_ch=conv_ch),
        grid=(rows // tm,),
        in_specs=[pl.BlockSpec((tm, d), row),
                  _mod_spec(shift, tm, tiles_per_group),
                  _mod_spec(scale, tm, tiles_per_group),
                  pl.BlockSpec((1, d), const),
                  pl.BlockSpec(w_in_p.shape, const),
                  rope_spec, rope_spec, rope_spec],
        out_specs=[pl.BlockSpec((tm, w), row) for w, _ in widths],
        out_shape=[jax.ShapeDtypeStruct((rows, w), dt) for w, dt in widths],
        compiler_params=_cparams(("arbitrary",)),
        name="mix_in",
    )(x, shift, scale, g_pre, w_in_p, *rope_tabs)


def _key_to_float(u):
    o = u ^ INT_MIN
    bits = jnp.where(o >= 0, o, o ^ INT_MAX)
    return lax.bitcast_convert_type(bits, F32)


def _count(sc_ref, nkb, r0, rg, tk, pred):
    def body(kb, acc):
        for j in range(tk // LANES):
            x = sc_ref[kb, r0:r0 + rg, j * LANES:(j + 1) * LANES]
            kpos = kb * tk + j * LANES + lax.broadcasted_iota(I32, (rg, LANES), 1)
            acc = acc + jnp.where(pred(x, kpos), 1.0, 0.0)
        return acc
    acc = lax.fori_loop(0, nkb, body, jnp.zeros((rg, LANES), F32))
    return jnp.broadcast_to(jnp.sum(acc, axis=1, keepdims=True), (rg, LANES))


def _select(sc_ref, t_ref, j_ref, nkb, rows, rg, tk, topk, idx_bits, rows_valid):
    for r0 in range(0, rows, rg):
        cnt = functools.partial(_count, sc_ref, nkb, r0, rg, tk)

        def bit_body(i, key):
            cand = key | jnp.left_shift(jnp.int32(1), 31 - i)
            t = _key_to_float(cand)
            c = cnt(lambda x, kp: x >= t)
            return jnp.where(c >= topk, cand, key)

        key = lax.fori_loop(0, 32, bit_body, jnp.zeros((rg, LANES), I32))
        c_fin = cnt(lambda x, kp: x >= -F32_MAX)
        thr = jnp.where(c_fin >= topk, _key_to_float(key), -F32_MAX)
        c_gt = cnt(lambda x, kp: x > thr)
        c_ge = cnt(lambda x, kp: x >= thr)
        need = topk - c_gt
        row = r0 + lax.broadcasted_iota(I32, (rg, LANES), 0)
        excess = jnp.max(jnp.where((c_ge > topk) & (row < rows_valid), 1.0, 0.0)) > 0.5

        def tie_search():
            def jbody(i, p):
                cand = p | jnp.left_shift(jnp.int32(1), idx_bits - 1 - i)
                c = cnt(lambda x, kp: (x == thr) & (kp < cand))
                return jnp.where(c < need, cand, p)
            return lax.fori_loop(0, idx_bits, jbody, jnp.zeros((rg, LANES), I32))

        jmax = lax.cond(excess, tie_search, lambda: jnp.full((rg, LANES), INT_MAX, I32))
        t_ref[r0:r0 + rg, :] = thr
        j_ref[r0:r0 + rg, :] = jmax


def _sel_mask(scb, kpos, thr, jmax):
    return (scb > thr) | ((scb == thr) & (kpos <= jmax))


def _prompt_attn_kernel(q_ref, qi_ref, kiwi_ref, kt_ref, v_ref, kit_ref, o_ref,
                        sc_ref, wib_ref, m_ref, l_ref, acc_ref, t_ref, j_ref,
                        *, tq, topk, idx_bits):
    tk = tq
    reps = tk // LANES
    i = pl.program_id(1)
    nkb = i + 1
    q0 = i * tq

    kiwi = kiwi_ref[...]
    for h in range(IDX_HEADS):
        wib_ref[h] = jnp.broadcast_to(kiwi[:, IDX_DIM + h:IDX_DIM + h + 1], (tq, LANES))

    qpos = q0 + lax.broadcasted_iota(I32, (tq, tk), 0)
    lane_k = lax.broadcasted_iota(I32, (tq, tk), 1)

    def score_body(kb, carry):
        kit = kit_ref[kb]
        acc = None
        for h in range(IDX_HEADS):
            s = jnp.dot(qi_ref[:, h * LANES:(h + 1) * LANES], kit, preferred_element_type=F32)
            t = jnp.maximum(s, 0.0) * _tile_lanes(wib_ref[h], reps)
            acc = t if acc is None else acc + t
        sc_ref[kb] = jnp.where(kb * tk + lane_k <= qpos, acc, -jnp.inf)
        return carry

    lax.fori_loop(0, nkb, score_body, 0)

    _select(sc_ref, t_ref, j_ref, nkb, tq, min(64, tq), tk, topk, idx_bits, tq)

    m_ref[...] = jnp.full(m_ref.shape, NEG_BIG, F32)
    l_ref[...] = jnp.zeros(l_ref.shape, F32)
    acc_ref[...] = jnp.zeros(acc_ref.shape, F32)
    thr = _tile_lanes(t_ref[...], reps)
    jmax = _tile_lanes(j_ref[...], reps)

    def attn_body(kb, carry):
        mask = _sel_mask(sc_ref[kb], kb * tk + lane_k, thr, jmax)
        kt = kt_ref[kb]
        vb = v_ref[pl.ds(pl.multiple_of(kb * tk, tk), tk), :]
        for h in range(N_HEADS):
            s = jnp.dot(q_ref[:, h * LANES:(h + 1) * LANES], kt, preferred_element_type=F32)
            s = jnp.where(mask, s, NEG_BIG)
            m_prev = m_ref[h]
            m_new = jnp.maximum(m_prev, jnp.max(s, axis=1, keepdims=True))
            alpha = jnp.exp(m_prev - m_new)
            p = jnp.exp(s - _tile_lanes(m_new, reps))
            l_ref[h] = alpha * l_ref[h] + jnp.sum(p, axis=1, keepdims=True)
            acc_ref[h] = alpha * acc_ref[h] + jnp.dot(p.astype(BF16), vb, preferred_element_type=F32)
            m_ref[h] = m_new
        return carry

    lax.fori_loop(0, nkb, attn_body, 0)

    for h in range(N_HEADS):
        g = h // HEADS_PER_KV
        o = acc_ref[h] / l_ref[h]
        o_ref[:, h * HEAD_DIM:(h + 1) * HEAD_DIM] = o[:, g * HEAD_DIM:(g + 1) * HEAD_DIM]


def _prompt_attn(q, qi, kiwi, kt, v, kit, tq, topk):
    b, s, _ = q.shape
    nb = s // tq
    idx_bits = int(s).bit_length()
    blk = lambda w: pl.BlockSpec((None, tq, w), lambda bi, i: (bi, i, 0))
    full = lambda shp: pl.BlockSpec((None,) + shp, lambda bi, i: (bi,) + (0,) * len(shp))
    return pl.pallas_call(
        functools.partial(_prompt_attn_kernel, tq=tq, topk=topk, idx_bits=idx_bits),
        grid=(b, nb),
        in_specs=[blk(N_HEADS * LANES), blk(IDX_HEADS * LANES), blk(LANES),
                  full((nb, KV_WIDTH, tq)), full((s, KV_WIDTH)), full((nb, 2 * IDX_DIM, tq))],
        out_specs=blk(ATTN_WIDTH),
        out_shape=jax.ShapeDtypeStruct((b, s, ATTN_WIDTH), F32),
        scratch_shapes=[pltpu.VMEM((nb, tq, tq), F32),
                        pltpu.VMEM((IDX_HEADS, tq, LANES), F32),
                        pltpu.VMEM((N_HEADS, tq, LANES), F32),
                        pltpu.VMEM((N_HEADS, tq, LANES), F32),
                        pltpu.VMEM((N_HEADS, tq, KV_WIDTH), F32),
                        pltpu.VMEM((tq, LANES), F32),
                        pltpu.VMEM((tq, LANES), I32)],
        compiler_params=_cparams(("arbitrary", "arbitrary")),
        name="prompt_attn",
    )(q, qi, kiwi, kt, v, kit)


def _sample_attn_kernel(pt_ref, qpad_ref, qih_ref, wib_ref, kin_ref, kn_ref, vn_ref, cik_ref, ck_ref, cv_ref,
                        o_ref, sc_ref, sa_ref, vs_ref, t_ref, j_ref, *, n_pages, n_tok, tpad, page, topk, idx_bits):
    p = pl.program_id(1)
    qih = qih_ref[...]
    wib = wib_ref[...]
    qpad = qpad_ref[...]

    def idx_score(s):
        acc = None
        for h in range(IDX_HEADS):
            t = jnp.maximum(s[h * tpad:(h + 1) * tpad], 0.0) * wib[h * tpad:(h + 1) * tpad]
            acc = t if acc is None else acc + t
        return acc

    s_idx = lax.dot_general(qih, cik_ref[...].astype(BF16), NT_DIMS, preferred_element_type=F32)
    sc_ref[p] = idx_score(s_idx)
    sa_ref[p] = lax.dot_general(qpad, ck_ref[...].astype(BF16), NT_DIMS, preferred_element_type=F32)
    vs_ref[pl.ds(pl.multiple_of(p * page, page), page), :] = cv_ref[...].astype(BF16)

    @pl.when(p == n_pages - 1)
    def _():
        nkb = n_pages + 1
        s_new = idx_score(lax.dot_general(qih, kin_ref[...], NT_DIMS, preferred_element_type=F32))
        tok = lax.broadcasted_iota(I32, (tpad, page), 0)
        col = lax.broadcasted_iota(I32, (tpad, page), 1)
        sc_ref[n_pages] = jnp.where((col <= tok) & (col < n_tok), s_new, -jnp.inf)
        sa_ref[n_pages] = lax.dot_general(qpad, kn_ref[...], NT_DIMS, preferred_element_type=F32)
        vs_ref[pl.ds(n_pages * page, page), :] = vn_ref[...]

        _select(sc_ref, t_ref, j_ref, nkb, tpad, tpad, page, topk, idx_bits, n_tok)
        per_head = lambda a: jnp.concatenate([a] * N_HEADS, axis=0)
        thr, jmax, col_h = per_head(t_ref[...]), per_head(j_ref[...]), per_head(col)

        def block_mask(kb):
            return _sel_mask(per_head(sc_ref[kb]), kb * page + col_h, thr, jmax)

        def max_body(kb, m):
            return jnp.maximum(m, jnp.where(block_mask(kb), sa_ref[kb], NEG_BIG))

        m = lax.fori_loop(0, nkb, max_body, jnp.full((N_HEADS * tpad, page), NEG_BIG, F32))
        m = jnp.broadcast_to(jnp.max(m, axis=1, keepdims=True), m.shape)

        def pv_body(kb, carry):
            l, acc = carry
            pr = jnp.where(block_mask(kb), jnp.exp(sa_ref[kb] - m), 0.0)
            vb = vs_ref[pl.ds(pl.multiple_of(kb * page, page), page), :]
            return l + pr, acc + jnp.dot(pr.astype(BF16), vb, preferred_element_type=F32)

        zeros = jnp.zeros((N_HEADS * tpad, page), F32)
        l, acc = lax.fori_loop(0, nkb, pv_body, (zeros, jnp.zeros((N_HEADS * tpad, KV_WIDTH), F32)))
        o_ref[...] = acc / jnp.sum(l, axis=1, keepdims=True)


def _sample_attn(page_table, qpad, qih, wib, kin, kn, vn, cache_idx_k, cache_k, cache_v, n_tok, tpad, topk):
    n, n_pages = page_table.shape
    page = cache_k.shape[1]
    assert page == LANES and KV_WIDTH == LANES
    idx_bits = int((n_pages + 1) * page).bit_length()
    seq = lambda r, w: pl.BlockSpec((None, r, w), lambda b, p, pt: (b, 0, 0))
    paged = lambda w: pl.BlockSpec((None, page, w), lambda b, p, pt: (pt[b, p], 0, 0))
    grid_spec = pltpu.PrefetchScalarGridSpec(
        num_scalar_prefetch=1,
        grid=(n, n_pages),
        in_specs=[seq(N_HEADS * tpad, KV_WIDTH), seq(IDX_HEADS * tpad, IDX_DIM), seq(IDX_HEADS * tpad, LANES),
                  seq(page, IDX_DIM), seq(page, KV_WIDTH), seq(page, KV_WIDTH),
                  paged(IDX_DIM), paged(KV_WIDTH), paged(KV_WIDTH)],
        out_specs=seq(N_HEADS * tpad, KV_WIDTH),
        scratch_shapes=[pltpu.VMEM((n_pages + 1, tpad, page), F32),
                        pltpu.VMEM((n_pages + 1, N_HEADS * tpad, page), F32),
                        pltpu.VMEM(((n_pages + 1) * page, KV_WIDTH), BF16),
                        pltpu.VMEM((tpad, LANES), F32),
                        pltpu.VMEM((tpad, LANES), I32)],
    )
    return pl.pallas_call(
        functools.partial(_sample_attn_kernel, n_pages=n_pages, n_tok=n_tok, tpad=tpad, page=page, topk=topk,
                          idx_bits=idx_bits),
        grid_spec=grid_spec,
        out_shape=jax.ShapeDtypeStruct((n, N_HEADS * tpad, KV_WIDTH), F32),
        compiler_params=_cparams(("arbitrary", "arbitrary")),
        name="sample_attn",
    )(page_table, qpad, qih, wib, kin, kn, vn, cache_idx_k, cache_k, cache_v)


def _mix_out_tail(x, o, gb, u0, um1, um2, wc_ref, wo_ref, gt, gpost, aw):
    y = wc_ref[0:1, :] * um2 + wc_ref[1:2, :] * um1 + wc_ref[2:3, :] * u0
    mixed = (jnp.dot(o.astype(BF16), wo_ref[0:aw, :], preferred_element_type=F32)
             + jnp.dot((gb * y).astype(BF16), wo_ref[aw:, :], preferred_element_type=F32))
    return x + gt * (_rms(mixed) * gpost)


def _mix_out_prompt_kernel(x_ref, o_ref_in, gb_ref, u_ref, halo_ref, gt_ref, gpost_ref, wc_ref, wo_ref,
                           out_ref, ext_ref, *, tm, tiles_per_seq):
    i = pl.program_id(0)
    halo = halo_ref[...]
    ext_ref[0:SUBLANES, :] = jnp.where(i % tiles_per_seq == 0, jnp.zeros_like(halo), halo)
    ext_ref[SUBLANES:, :] = u_ref[...]
    um2 = ext_ref[SUBLANES - 2:SUBLANES - 2 + tm, :]
    um1 = ext_ref[SUBLANES - 1:SUBLANES - 1 + tm, :]
    out_ref[...] = _mix_out_tail(x_ref[...], o_ref_in[...], gb_ref[...], u_ref[...], um1, um2,
                                 wc_ref, wo_ref, gt_ref[0], gpost_ref[...], ATTN_WIDTH)


def _mix_out_prompt(x, o, gb, u, gate, g_post, w_conv, w_o, tm, tiles_per_seq):
    rows, d = x.shape
    cw = u.shape[1]
    const = lambda i: (0, 0)
    row = lambda i: (i, 0)
    hb = tm // SUBLANES
    return pl.pallas_call(
        functools.partial(_mix_out_prompt_kernel, tm=tm, tiles_per_seq=tiles_per_seq),
        grid=(rows // tm,),
        in_specs=[pl.BlockSpec((tm, d), row),
                  pl.BlockSpec((tm, ATTN_WIDTH), row),
                  pl.BlockSpec((tm, cw), row),
                  pl.BlockSpec((tm, cw), row),
                  pl.BlockSpec((SUBLANES, cw), lambda i: (jnp.maximum(i * hb - 1, 0), 0)),
                  _mod_spec(gate, tm, tiles_per_seq),
                  pl.BlockSpec((1, d), const),
                  pl.BlockSpec(w_conv.shape, const),
                  pl.BlockSpec(w_o.shape, const)],
        out_specs=pl.BlockSpec((tm, d), row),
        out_shape=jax.ShapeDtypeStruct((rows, d), F32),
        scratch_shapes=[pltpu.VMEM((tm + SUBLANES, cw), F32)],
        compiler_params=_cparams(("arbitrary",)),
        name="mix_out_prompt",
    )(x, o, gb, u, u, gate, g_post, w_conv, w_o)


def _mix_out_sample_kernel(x_ref, o_ref_in, gb_ref, um2_ref, um1_ref, u0_ref, gt_ref, gpost_ref, wc_ref, wo_ref,
                           out_ref):
    out_ref[...] = _mix_out_tail(x_ref[...], o_ref_in[...], gb_ref[...], u0_ref[...], um1_ref[...], um2_ref[...],
                                 wc_ref, wo_ref, gt_ref[0], gpost_ref[...], ATTN_WIDTH)


def _mix_out_sample(x, o, gb, ext, gate, g_post, w_conv, w_o, tm):
    rows, d = x.shape
    cw = gb.shape[1]
    const = lambda i: (0, 0)
    row = lambda i: (i, 0)
    return pl.pallas_call(
        _mix_out_sample_kernel,
        grid=(rows // tm,),
        in_specs=[pl.BlockSpec((tm, d), row),
                  pl.BlockSpec((tm, ATTN_WIDTH), row),
                  pl.BlockSpec((tm, cw), row),
                  pl.BlockSpec((tm, cw), lambda i: (i, 0)),
                  pl.BlockSpec((tm, cw), lambda i: (i + 1, 0)),
                  pl.BlockSpec((tm, cw), lambda i: (i + 2, 0)),
                  pl.BlockSpec((1, tm, d), lambda i: (0, 0, 0)),
                  pl.BlockSpec((1, d), const),
                  pl.BlockSpec(w_conv.shape, const),
                  pl.BlockSpec(w_o.shape, const)],
        out_specs=pl.BlockSpec((tm, d), row),
        out_shape=jax.ShapeDtypeStruct((rows, d), F32),
        compiler_params=_cparams(("arbitrary",)),
        name="mix_out_sample",
    )(x, o, gb, ext, ext, ext, gate, g_post, w_conv, w_o)


def _prompt_layer(x, mod, lw, tm, tq):
    b, s, d = x.shape
    conv_ch = lw["w_conv"].shape[1]
    tps = s // tm
    md = lambda j, k: mod[:, j, k][:, None, :]
    xr = x.reshape(b * s, d)
    x1 = _ffn(xr, md(0, 0), md(0, 1), md(0, 2), lw["g_pre"][0:1], lw["g_post"][0:1], lw["ffn1_w13"], lw["ffn1_w2"],
              tm, tps)
    tabs = _rope_tables(jnp.arange(s))
    q, k, v, qi, kiwi, gb, u = _mix_in(x1, md(1, 0), md(1, 1), lw["g_pre"][1:2], lw["w_in_p"], tabs, tm, tps, tps,
                                       conv_ch)
    nb = s // tq
    kt = jnp.swapaxes(k.astype(BF16).reshape(b, nb, tq, KV_WIDTH), 2, 3)
    kit = jnp.swapaxes(kiwi[:, :IDX_DIM].astype(BF16).reshape(b, nb, tq, IDX_DIM), 2, 3)
    kit = jnp.concatenate([kit, kit], axis=2)
    topk = min(TOPK_MAX, s // 4)
    o = _prompt_attn(q.reshape(b, s, -1), qi.reshape(b, s, -1), kiwi.reshape(b, s, -1), kt,
                     v.astype(BF16).reshape(b, s, KV_WIDTH), kit, tq, topk)
    x2 = _mix_out_prompt(x1, o.reshape(b * s, -1), gb, u, md(1, 2), lw["g_post"][1:2], lw["w_conv"], lw["w_o"],
                         tm, tps)
    x3 = _ffn(x2, md(2, 0), md(2, 1), md(2, 2), lw["g_pre"][2:3], lw["g_post"][2:3], lw["ffn2_w13"], lw["ffn2_w2"],
              tm, tps)
    state = (k.reshape(b, s, N_KV_HEADS, HEAD_DIM), v.reshape(b, s, N_KV_HEADS, HEAD_DIM),
             kiwi[:, :IDX_DIM].reshape(b, s, IDX_DIM), u.reshape(b, s, conv_ch)[:, s - (CONV_K - 1):])
    return x3.reshape(b, s, d), state


def _sample_layer(x_tm, mod, lw, cache_k, cache_v, cache_idx_k, state_conv, page_table):
    n = mod.shape[0]
    t_tok = x_tm.shape[0] // n
    d = x_tm.shape[1]
    conv_ch = lw["w_conv"].shape[1]
    n_pages = page_table.shape[1]
    page = cache_k.shape[1]
    past = n_pages * page
    md = lambda j, k: mod[:, j, k][None]
    x1 = _ffn(x_tm, md(0, 0), md(0, 1), md(0, 2), lw["g_pre"][0:1], lw["g_post"][0:1], lw["ffn1_w13"],
              lw["ffn1_w2"], n, t_tok)
    pos = jnp.repeat(past + jnp.arange(t_tok), n)
    tabs = _rope_tables(pos)
    q, k, v, qi, kiwi, gb, u = _mix_in(x1, md(1, 0), md(1, 1), lw["g_pre"][1:2], lw["w_in_p"], tabs, n, t_tok, t_tok,
                                       conv_ch)
    tpad = SUBLANES
    assert t_tok <= tpad

    def seq_major(a, heads, width):
        a = a.reshape(t_tok, n, heads, width).transpose(1, 2, 0, 3)
        a = jnp.pad(a, ((0, 0), (0, 0), (0, tpad - t_tok), (0, 0)))
        return a.reshape(n, heads * tpad, width)

    qpad = seq_major(q, N_HEADS, LANES)
    qi_nat = jnp.concatenate([qi[:, h * LANES + (h % 2) * IDX_DIM:h * LANES + (h % 2 + 1) * IDX_DIM]
                              for h in range(IDX_HEADS)], axis=1)
    qih = seq_major(qi_nat, IDX_HEADS, IDX_DIM)
    wib = jnp.broadcast_to(seq_major(kiwi[:, IDX_DIM:IDX_DIM + IDX_HEADS], IDX_HEADS, 1),
                           (n, IDX_HEADS * tpad, LANES))

    def new_keys(a):
        a = a.reshape(t_tok, n, -1).transpose(1, 0, 2).astype(BF16)
        return jnp.pad(a, ((0, 0), (0, page - t_tok), (0, 0)))

    topk = min(TOPK_MAX, (past + t_tok) // 4)
    o = _sample_attn(page_table, qpad, qih, wib, new_keys(kiwi[:, :IDX_DIM]), new_keys(k), new_keys(v),
                     cache_idx_k, cache_k.reshape(-1, page, KV_WIDTH), cache_v.reshape(-1, page, KV_WIDTH),
                     t_tok, tpad, topk)
    o = o.reshape(n, N_KV_HEADS, HEADS_PER_KV, tpad, N_KV_HEADS, HEAD_DIM)[:, :, :, :t_tok]
    o = jnp.stack([o[:, g, :, :, g] for g in range(N_KV_HEADS)], axis=1)
    o_tm = o.transpose(3, 0, 1, 2, 4).reshape(t_tok * n, ATTN_WIDTH)
    ext = jnp.concatenate([state_conv.transpose(1, 0, 2).reshape(-1, conv_ch), u], axis=0)
    x2 = _mix_out_sample(x1, o_tm, gb, ext, md(1, 2), lw["g_post"][1:2], lw["w_conv"], lw["w_o"], n)
    x3 = _ffn(x2, md(2, 0), md(2, 1), md(2, 2), lw["g_pre"][2:3], lw["g_post"][2:3], lw["ffn2_w13"],
              lw["ffn2_w2"], n, t_tok)
    tm_to_seq = lambda a: a.reshape((t_tok, n) + a.shape[1:]).swapaxes(0, 1)
    state = (tm_to_seq(k).reshape(n, t_tok, N_KV_HEADS, HEAD_DIM), tm_to_seq(v).reshape(n, t_tok, N_KV_HEADS, HEAD_DIM),
             tm_to_seq(kiwi[:, :IDX_DIM]), tm_to_seq(u)[:, t_tok - (CONV_K - 1):])
    return x3, state


def _row_tile(s, target):
    t = min(target, s)
    while s % t:
        t //= 2
    return t


def kernel(x_prompt, x_sample, cache_k, cache_v, cache_idx_k, state_conv, page_table, c_prompt, c_sample, w_ada, b_ada, g_pre, g_post, ffn1_w13, ffn1_w2, w_in, w_conv, w_o, ffn2_w13, ffn2_w2):
    depth = w_ada.shape[0]
    b, s, d = x_prompt.shape
    n, t_tok, _ = x_sample.shape
    conv_ch = w_conv.shape[2]
    tm = _row_tile(s, 512)
    tq = _row_tile(s, 256)

    yp = x_prompt
    ys = x_sample.swapaxes(0, 1).reshape(t_tok * n, d)
    c_all = jnp.concatenate([c_prompt, c_sample], axis=0)
    n_c = c_all.shape[0]
    c_all = jnp.pad(c_all, ((0, -n_c % SUBLANES), (0, 0)))
    outs = [[] for _ in range(8)]
    for l in range(depth):
        lw = dict(g_pre=g_pre[l], g_post=g_post[l], w_conv=w_conv[l],
                  ffn1_w13=ffn1_w13[l].astype(BF16), ffn1_w2=ffn1_w2[l].astype(BF16),
                  ffn2_w13=ffn2_w13[l].astype(BF16), ffn2_w2=ffn2_w2[l].astype(BF16),
                  w_in_p=_pack_w_in(w_in[l], conv_ch), w_o=w_o[l].astype(BF16))
        mod = _ada_mod(c_all, w_ada[l], b_ada[l])[:n_c].reshape(n_c, N_SUB, 3, d)
        yp, st_p = _prompt_layer(yp, mod[:b], lw, tm, tq)
        ys, st_s = _sample_layer(ys, mod[b:], lw, cache_k[l], cache_v[l], cache_idx_k[l], state_conv[l], page_table)
        for lst, a in zip(outs, st_p + st_s):
            lst.append(a)
    ys = ys.reshape(t_tok, n, d).swapaxes(0, 1)
    return (yp, ys) + tuple(jnp.stack(lst) for lst in outs)
```

```python
import functools

import numpy as np
import jax
import jax.numpy as jnp
from jax import lax
from jax.experimental import pallas as pl
from jax.experimental.pallas import tpu as pltpu

F32 = jnp.float32
BF16 = jnp.bfloat16
I32 = jnp.int32

N_HEADS = 8
HEAD_DIM = 64
N_KV_HEADS = 2
HEADS_PER_KV = N_HEADS // N_KV_HEADS
ATTN_WIDTH = N_HEADS * HEAD_DIM
KV_WIDTH = N_KV_HEADS * HEAD_DIM
ROT_DIV = 4
ROT = HEAD_DIM // ROT_DIV
ROT_HALF = ROT // 2
ROPE_THETA = 500000.0
IDX_HEADS = 4
IDX_DIM = 64
IDX_SCALE = (IDX_HEADS * IDX_DIM) ** -0.5
TOPK_MAX = 256
CONV_K = 3
N_SUB = 3
RMS_EPS = 1e-6

LANES = 128
SUBLANES = 8
VMEM_LIMIT = 56 * 1024 * 1024

NEG_BIG = -1e30
F32_MAX = float(np.finfo(np.float32).max)
INT_MIN = -(2 ** 31)
INT_MAX = 2 ** 31 - 1

NT_DIMS = (((1,), (1,)), ((), ()))


def _cparams(sem):
    return pltpu.CompilerParams(dimension_semantics=sem, vmem_limit_bytes=VMEM_LIMIT)


def _silu(x):
    return x * (1.0 / (1.0 + jnp.exp(-x)))


def _rms(x):
    return x * lax.rsqrt(jnp.mean(x * x, axis=-1, keepdims=True) + RMS_EPS)


def _tile_lanes(x, reps):
    return x if reps == 1 else jnp.concatenate([x] * reps, axis=1)


def _ada_kernel(c_ref, w_ref, b_ref, o_ref):
    a = _silu(c_ref[...]).astype(BF16)
    o_ref[...] = jnp.dot(a, w_ref[...].astype(BF16), preferred_element_type=F32) + b_ref[...]


def _ada_mod(c, w_ada, b_ada):
    n, d = c.shape
    nout = w_ada.shape[1]
    tn = 9 * LANES if nout % (9 * LANES) == 0 else LANES
    return pl.pallas_call(
        _ada_kernel,
        grid=(nout // tn,),
        in_specs=[pl.BlockSpec((n, d), lambda j: (0, 0)),
                  pl.BlockSpec((d, tn), lambda j: (0, j)),
                  pl.BlockSpec((1, tn), lambda j: (0, j))],
        out_specs=pl.BlockSpec((n, tn), lambda j: (0, j)),
        out_shape=jax.ShapeDtypeStruct((n, nout), F32),
        compiler_params=_cparams(("arbitrary",)),
        name="ada_mod",
    )(c, w_ada, b_ada.reshape(1, nout))


def _ffn_kernel(x_ref, sh_ref, sc_ref, gt_ref, gpre_ref, gpost_ref, w13_ref, w2_ref, o_ref, *, d_ff, chunk):
    x = x_ref[...]
    h = _rms(x) * gpre_ref[...] * (1.0 + sc_ref[0]) + sh_ref[0]
    hb = h.astype(BF16)
    y = jnp.zeros(x.shape, F32)
    for c in range(d_ff // chunk):
        a = jnp.dot(hb, w13_ref[:, c * chunk:(c + 1) * chunk], preferred_element_type=F32)
        b = jnp.dot(hb, w13_ref[:, d_ff + c * chunk:d_ff + (c + 1) * chunk], preferred_element_type=F32)
        g = (_silu(a) * b).astype(BF16)
        y = y + jnp.dot(g, w2_ref[c * chunk:(c + 1) * chunk, :], preferred_element_type=F32)
    o_ref[...] = x + 0.5 * gt_ref[0] * (_rms(y) * gpost_ref[...])


def _mod_spec(mod, tm, tiles_per_group):
    _, r, d = mod.shape
    return pl.BlockSpec((1, r, d), lambda i: (i // tiles_per_group, 0, 0))


def _ffn(x, shift, scale, gate, g_pre, g_post, w13, w2, tm, tiles_per_group):
    rows, d = x.shape
    d_ff = w2.shape[0]
    const = lambda i: (0, 0)
    return pl.pallas_call(
        functools.partial(_ffn_kernel, d_ff=d_ff, chunk=min(512, d_ff)),
        grid=(rows // tm,),
        in_specs=[pl.BlockSpec((tm, d), lambda i: (i, 0)),
                  _mod_spec(shift, tm, tiles_per_group),
                  _mod_spec(scale, tm, tiles_per_group),
                  _mod_spec(gate, tm, tiles_per_group),
                  pl.BlockSpec((1, d), const),
                  pl.BlockSpec((1, d), const),
                  pl.BlockSpec(w13.shape, const),
                  pl.BlockSpec(w2.shape, const)],
        out_specs=pl.BlockSpec((tm, d), lambda i: (i, 0)),
        out_shape=jax.ShapeDtypeStruct((rows, d), F32),
        compiler_params=_cparams(("arbitrary",)),
        name="ffn",
    )(x, shift, scale, gate, g_pre, g_post, w13, w2)


_OFF_Q = 0
_OFF_K = _OFF_Q + ATTN_WIDTH
_OFF_V = _OFF_K + KV_WIDTH
_OFF_QI = _OFF_V + KV_WIDTH
_OFF_KIWI = _OFF_QI + IDX_HEADS * IDX_DIM
_OFF_GB = _OFF_KIWI + LANES


def _pack_w_in(w_in, conv_ch):
    splits = [ATTN_WIDTH, KV_WIDTH, KV_WIDTH, IDX_HEADS * IDX_DIM, IDX_DIM, IDX_HEADS, conv_ch, conv_ch, conv_ch]
    offs = np.cumsum([0] + splits)
    d = w_in.shape[0]
    pad = jnp.zeros((d, LANES - IDX_DIM - IDX_HEADS), w_in.dtype)
    return jnp.concatenate([w_in[:, :offs[6]], pad, w_in[:, offs[6]:]], axis=1).astype(BF16)


def _rope_tables(pos):
    inv = jnp.power(ROPE_THETA, -jnp.arange(ROT_HALF, dtype=F32) * 2.0 / ROT)
    ang = pos.astype(F32)[:, None] * inv[None, :]
    cos, sin = jnp.cos(ang), jnp.sin(ang)
    n = pos.shape[0]
    one = jnp.ones((n, HEAD_DIM - ROT), F32)
    zero_r = jnp.zeros((n, HEAD_DIM - ROT), F32)
    zero_h = jnp.zeros((n, ROT_HALF), F32)
    c = jnp.concatenate([cos, cos, one], axis=1)
    s1 = jnp.concatenate([-sin, zero_h, zero_r], axis=1)
    s2 = jnp.concatenate([zero_h, sin, zero_r], axis=1)
    rep = LANES // HEAD_DIM
    return jnp.tile(c, (1, rep)), jnp.tile(s1, (1, rep)), jnp.tile(s2, (1, rep))


def _rope128(x, c, s1, s2):
    return x * c + pltpu.roll(x, LANES - ROT_HALF, 1) * s1 + pltpu.roll(x, ROT_HALF, 1) * s2


def _mix_in_kernel(x_ref, sh_ref, sc_ref, gpre_ref, w_ref, rc_ref, rs1_ref, rs2_ref,
                   q_ref, k_ref, v_ref, qi_ref, kiwi_ref, gb_ref, u_ref, *, conv_ch):
    x = x_ref[...]
    hb = (_rms(x) * gpre_ref[...] * (1.0 + sc_ref[0]) + sh_ref[0]).astype(BF16)
    c, s1, s2 = rc_ref[...], rs1_ref[...], rs2_ref[...]

    def proj(off, width):
        return jnp.dot(hb, w_ref[:, off:off + width], preferred_element_type=F32)

    def rope_cols(z):
        return [_rope128(z[:, g * LANES:(g + 1) * LANES], c, s1, s2) for g in range(z.shape[1] // LANES)]

    lane = lax.broadcasted_iota(I32, (x.shape[0], LANES), 1)
    low = lane < HEAD_DIM

    zq = proj(_OFF_Q, ATTN_WIDTH)
    for g, r in enumerate(rope_cols(zq)):
        r = r * (HEAD_DIM ** -0.5)
        swapped = pltpu.roll(r, HEAD_DIM, 1)
        for e in range(2):
            h = 2 * g + e
            kv_group = h // HEADS_PER_KV
            src = r if e == kv_group else swapped
            keep = low if kv_group == 0 else jnp.logical_not(low)
            q_ref[:, h * LANES:(h + 1) * LANES] = jnp.where(keep, src, 0.0).astype(BF16)
    zk = proj(_OFF_K, KV_WIDTH)
    for g, r in enumerate(rope_cols(zk)):
        k_ref[:, g * LANES:(g + 1) * LANES] = r
    v_ref[...] = proj(_OFF_V, KV_WIDTH)
    zqi = proj(_OFF_QI, IDX_HEADS * IDX_DIM)
    for g, r in enumerate(rope_cols(zqi)):
        qi_ref[:, (2 * g) * LANES:(2 * g + 1) * LANES] = jnp.where(low, r, 0.0).astype(BF16)
        qi_ref[:, (2 * g + 1) * LANES:(2 * g + 2) * LANES] = jnp.where(low, 0.0, r).astype(BF16)
    zkw = proj(_OFF_KIWI, LANES)
    kiwi_ref[...] = jnp.where(lane < IDX_DIM, _rope128(zkw, c, s1, s2), zkw * IDX_SCALE)
    gb_ref[...] = proj(_OFF_GB, conv_ch)
    u_ref[...] = proj(_OFF_GB + conv_ch, conv_ch) * proj(_OFF_GB + 2 * conv_ch, conv_ch)


def _mix_in(x, shift, scale, g_pre, w_in_p, rope_tabs, tm, tiles_per_group, rope_tiles, conv_ch):
    rows, d = x.shape
    const = lambda i: (0, 0)
    row = lambda i: (i, 0)
    rope_spec = pl.BlockSpec((tm, LANES), lambda i: (i % rope_tiles, 0))
    assert 2 * HEAD_DIM == LANES and KV_WIDTH == LANES and 2 * IDX_DIM == LANES
    widths = [(N_HEADS * LANES, BF16), (KV_WIDTH, F32), (KV_WIDTH, F32), (IDX_HEADS * LANES, BF16),
              (LANES, F32), (conv_ch, F32), (conv_ch, F32)]
    return pl.pallas_call(
        functools.partial(_mix_in_kernel, conv_ch=conv_ch),
        grid=(rows // tm,),
        in_specs=[pl.BlockSpec((tm, d), row),
                  _mod_spec(shift, tm, tiles_per_group),
                  _mod_spec(scale, tm, tiles_per_group),
                  pl.BlockSpec((1, d), const),
                  pl.BlockSpec(w_in_p.shape, const),
                  rope_spec, rope_spec, rope_spec],
        out_specs=[pl.BlockSpec((tm, w), row) for w, _ in widths],
        out_shape=[jax.ShapeDtypeStruct((rows, w), dt) for w, dt in widths],
        compiler_params=_cparams(("arbitrary",)),
        name="mix_in",
    )(x, shift, scale, g_pre, w_in_p, *rope_tabs)


def _key_to_float(u):
    o = u ^ INT_MIN
    bits = jnp.where(o >= 0, o, o ^ INT_MAX)
    return lax.bitcast_convert_type(bits, F32)


def _count(sc_ref, nkb, r0, rg, tk, pred):
    def body(kb, acc):
        for j in range(tk // LANES):
            x = sc_ref[kb, r0:r0 + rg, j * LANES:(j + 1) * LANES]
            kpos = kb * tk + j * LANES + lax.broadcasted_iota(I32, (rg, LANES), 1)
            acc = acc + jnp.where(pred(x, kpos), 1.0, 0.0)
        return acc
    acc = lax.fori_loop(0, nkb, body, jnp.zeros((rg, LANES), F32))
    return jnp.broadcast_to(jnp.sum(acc, axis=1, keepdims=True), (rg, LANES))


def _select_rows(cnt, rg, topk, idx_bits, row_ok):
    def bit_body(i, key):
        cand = key | jnp.left_shift(jnp.int32(1), 31 - i)
        t = _key_to_float(cand)
        c = cnt(lambda x, kp: x >= t)
        return jnp.where(c >= topk, cand, key)

    key = lax.fori_loop(0, 32, bit_body, jnp.zeros((rg, LANES), I32))
    c_fin = cnt(lambda x, kp: x >= -F32_MAX)
    thr = jnp.where(c_fin >= topk, _key_to_float(key), -F32_MAX)
    c_gt = cnt(lambda x, kp: x > thr)
    c_ge = cnt(lambda x, kp: x >= thr)
    need = topk - c_gt
    excess = jnp.max(jnp.where((c_ge > topk) & row_ok, 1.0, 0.0)) > 0.5

    def tie_search():
        def jbody(i, p):
            cand = p | jnp.left_shift(jnp.int32(1), idx_bits - 1 - i)
            c = cnt(lambda x, kp: (x == thr) & (kp < cand))
            return jnp.where(c < need, cand, p)
        return lax.fori_loop(0, idx_bits, jbody, jnp.zeros((rg, LANES), I32))

    jmax = lax.cond(excess, tie_search, lambda: jnp.full((rg, LANES), INT_MAX, I32))
    return thr, jmax


def _select(sc_ref, t_ref, j_ref, nkb, rows, rg, tk, topk, idx_bits):
    for r0 in range(0, rows, rg):
        cnt = functools.partial(_count, sc_ref, nkb, r0, rg, tk)
        thr, jmax = _select_rows(cnt, rg, topk, idx_bits, True)
        t_ref[r0:r0 + rg, :] = thr
        j_ref[r0:r0 + rg, :] = jmax


def _sel_mask(scb, kpos, thr, jmax):
    return (scb > thr) | ((scb == thr) & (kpos <= jmax))


def _prompt_attn_kernel(q_ref, qi_ref, kiwi_ref, kt_ref, v_ref, kit_ref, o_ref,
                        sc_ref, wib_ref, m_ref, l_ref, acc_ref, t_ref, j_ref,
                        *, tq, topk, idx_bits):
    tk = tq
    reps = tk // LANES
    i = pl.program_id(1)
    nkb = i + 1
    q0 = i * tq

    kiwi = kiwi_ref[...]
    for h in range(IDX_HEADS):
        wib_ref[h] = jnp.broadcast_to(kiwi[:, IDX_DIM + h:IDX_DIM + h + 1], (tq, LANES))

    qpos = q0 + lax.broadcasted_iota(I32, (tq, tk), 0)
    lane_k = lax.broadcasted_iota(I32, (tq, tk), 1)

    def score_body(kb, carry):
        kit = kit_ref[kb]
        acc = None
        for h in range(IDX_HEADS):
            s = jnp.dot(qi_ref[:, h * LANES:(h + 1) * LANES], kit, preferred_element_type=F32)
            t = jnp.maximum(s, 0.0) * _tile_lanes(wib_ref[h], reps)
            acc = t if acc is None else acc + t
        sc_ref[kb] = jnp.where(kb * tk + lane_k <= qpos, acc, -jnp.inf)
        return carry

    lax.fori_loop(0, nkb, score_body, 0)

    _select(sc_ref, t_ref, j_ref, nkb, tq, min(64, tq), tk, topk, idx_bits)

    m_ref[...] = jnp.full(m_ref.shape, NEG_BIG, F32)
    l_ref[...] = jnp.zeros(l_ref.shape, F32)
    acc_ref[...] = jnp.zeros(acc_ref.shape, F32)
    thr = _tile_lanes(t_ref[...], reps)
    jmax = _tile_lanes(j_ref[...], reps)

    def attn_body(kb, carry):
        mask = _sel_mask(sc_ref[kb], kb * tk + lane_k, thr, jmax)
        kt = kt_ref[kb]
        vb = v_ref[pl.ds(pl.multiple_of(kb * tk, tk), tk), :]
        for h in range(N_HEADS):
            s = jnp.dot(q_ref[:, h * LANES:(h + 1) * LANES], kt, preferred_element_type=F32)
            s = jnp.where(mask, s, NEG_BIG)
            m_prev = m_ref[h]
            m_new = jnp.maximum(m_prev, jnp.max(s, axis=1, keepdims=True))
            alpha = jnp.exp(m_prev - m_new)
            p = jnp.exp(s - _tile_lanes(m_new, reps))
            l_ref[h] = alpha * l_ref[h] + jnp.sum(p, axis=1, keepdims=True)
            acc_ref[h] = alpha * acc_ref[h] + jnp.dot(p.astype(BF16), vb, preferred_element_type=F32)
            m_ref[h] = m_new
        return carry

    lax.fori_loop(0, nkb, attn_body, 0)

    for h in range(N_HEADS):
        g = h // HEADS_PER_KV
        o = acc_ref[h] / l_ref[h]
        o_ref[:, h * HEAD_DIM:(h + 1) * HEAD_DIM] = o[:, g * HEAD_DIM:(g + 1) * HEAD_DIM]


def _prompt_attn(q, qi, kiwi, kt, v, kit, tq, topk):
    b, s, _ = q.shape
    nb = s // tq
    idx_bits = int(s).bit_length()
    blk = lambda w: pl.BlockSpec((None, tq, w), lambda bi, i: (bi, i, 0))
    full = lambda shp: pl.BlockSpec((None,) + shp, lambda bi, i: (bi,) + (0,) * len(shp))
    return pl.pallas_call(
        functools.partial(_prompt_attn_kernel, tq=tq, topk=topk, idx_bits=idx_bits),
        grid=(b, nb),
        in_specs=[blk(N_HEADS * LANES), blk(IDX_HEADS * LANES), blk(LANES),
                  full((nb, KV_WIDTH, tq)), full((s, KV_WIDTH)), full((nb, 2 * IDX_DIM, tq))],
        out_specs=blk(ATTN_WIDTH),
        out_shape=jax.ShapeDtypeStruct((b, s, ATTN_WIDTH), F32),
        scratch_shapes=[pltpu.VMEM((nb, tq, tq), F32),
                        pltpu.VMEM((IDX_HEADS, tq, LANES), F32),
                        pltpu.VMEM((N_HEADS, tq, LANES), F32),
                        pltpu.VMEM((N_HEADS, tq, LANES), F32),
                        pltpu.VMEM((N_HEADS, tq, KV_WIDTH), F32),
                        pltpu.VMEM((tq, LANES), F32),
                        pltpu.VMEM((tq, LANES), I32)],
        compiler_params=_cparams(("arbitrary", "arbitrary")),
        name="prompt_attn",
    )(q, qi, kiwi, kt, v, kit)


def _sample_attn_kernel(pt_ref, q_ref, qih_ref, wib_ref, kin_ref, kn_ref, vn_ref, *rest,
                        n_steps, pps, n_tok, tpad, page, topk, idx_bits):
    cik_refs, ck_refs, cv_refs = rest[:pps], rest[pps:2 * pps], rest[2 * pps:3 * pps]
    o_ref, sc_ref, sa_ref, vs_ref = rest[3 * pps:]
    step = pl.program_id(1)
    n_pages = n_steps * pps
    nkb = n_pages + 1
    qih = qih_ref[...]
    wib = wib_ref[...]
    q = q_ref[...]

    def idx_score(s):
        acc = None
        for h in range(IDX_HEADS):
            t = jnp.maximum(s[h * tpad:(h + 1) * tpad], 0.0) * wib[h * tpad:(h + 1) * tpad]
            acc = t if acc is None else acc + t
        return acc

    for j in range(pps):
        pg = step * pps + j
        sc_ref[pg] = idx_score(jnp.dot(qih, cik_refs[j][...].astype(BF16), preferred_element_type=F32))
        sa_ref[pg] = jnp.dot(q, ck_refs[j][...].astype(BF16), preferred_element_type=F32)
        vs_ref[pg] = cv_refs[j][...].astype(BF16)

    @pl.when(step == n_steps - 1)
    def _():
        s_new = idx_score(jnp.dot(qih, kin_ref[...], preferred_element_type=F32))
        tok = lax.broadcasted_iota(I32, (tpad, page), 0)
        col = lax.broadcasted_iota(I32, (tpad, page), 1)
        sc_ref[n_pages] = jnp.where((col <= tok) & (col < n_tok), s_new, -jnp.inf)
        sa_ref[n_pages] = jnp.dot(q, kn_ref[...], preferred_element_type=F32)
        vs_ref[n_pages] = vn_ref[...]

        kpos = (lax.broadcasted_iota(I32, (nkb, tpad, page), 0) * page
                + lax.broadcasted_iota(I32, (nkb, tpad, page), 2))

        def cnt(pred):
            c = jnp.sum(jnp.where(pred(sc_ref[...], kpos), 1.0, 0.0), axis=0)
            return jnp.broadcast_to(jnp.sum(c, axis=1, keepdims=True), c.shape)

        thr, jmax = _select_rows(cnt, tpad, topk, idx_bits, tok[:, :LANES] < n_tok)
        sc_ref[...] = jnp.where(_sel_mask(sc_ref[...], kpos, thr[None], jmax[None]), 0.0, NEG_BIG)

        heads = N_HEADS
        inv_l = []
        for t in range(n_tok):
            s = sa_ref[:, t * heads:(t + 1) * heads, :] + sc_ref[:, t:t + 1, :]
            m = jnp.max(jnp.max(s, axis=0), axis=1, keepdims=True)
            pr = jnp.exp(s - m[None])
            sa_ref[:, t * heads:(t + 1) * heads, :] = pr
            l = jnp.sum(jnp.sum(pr, axis=0), axis=1, keepdims=True)
            inv_l.append(jnp.broadcast_to(1.0 / l, (heads, KV_WIDTH)))

        def pv_body(kb, acc):
            return acc + lax.dot_general(sa_ref[kb].astype(BF16), vs_ref[kb], NT_DIMS, preferred_element_type=F32)

        acc = lax.fori_loop(0, nkb, pv_body, jnp.zeros((n_tok * heads, KV_WIDTH), F32), unroll=5)
        o_ref[...] = acc * jnp.concatenate(inv_l, axis=0)


def _sample_attn(page_table, q, qih, wib, kin_t, kn_t, vn_t, cik_t, ck_t, cv_t, n_tok, tpad, topk, pps):
    n, n_pages = page_table.shape
    page = ck_t.shape[2]
    assert page == LANES and KV_WIDTH == LANES and n_pages % pps == 0
    n_steps = n_pages // pps
    idx_bits = int((n_pages + 1) * page).bit_length()
    rows = n_tok * N_HEADS
    seq = lambda r, w: pl.BlockSpec((None, r, w), lambda b, p, pt: (b, 0, 0))

    def paged(r, j):
        return pl.BlockSpec((None, r, page), lambda b, p, pt: (pt[b, p * pps + j], 0, 0))

    grid_spec = pltpu.PrefetchScalarGridSpec(
        num_scalar_prefetch=1,
        grid=(n, n_steps),
        in_specs=([seq(rows, KV_WIDTH), seq(IDX_HEADS * tpad, IDX_DIM), seq(IDX_HEADS * tpad, LANES),
                   seq(IDX_DIM, page), seq(KV_WIDTH, page), seq(KV_WIDTH, page)]
                  + [paged(IDX_DIM, j) for j in range(pps)]
                  + [paged(KV_WIDTH, j) for j in range(pps)]
                  + [paged(KV_WIDTH, j) for j in range(pps)]),
        out_specs=seq(rows, KV_WIDTH),
        scratch_shapes=[pltpu.VMEM((n_pages + 1, tpad, page), F32),
                        pltpu.VMEM((n_pages + 1, rows, page), F32),
                        pltpu.VMEM((n_pages + 1, KV_WIDTH, page), BF16)],
    )
    return pl.pallas_call(
        functools.partial(_sample_attn_kernel, n_steps=n_steps, pps=pps, n_tok=n_tok, tpad=tpad, page=page,
                          topk=topk, idx_bits=idx_bits),
        grid_spec=grid_spec,
        out_shape=jax.ShapeDtypeStruct((n, rows, KV_WIDTH), F32),
        compiler_params=_cparams(("arbitrary", "arbitrary")),
        name="sample_attn",
    )(page_table, q, qih, wib, kin_t, kn_t, vn_t, *([cik_t] * pps), *([ck_t] * pps), *([cv_t] * pps))


def _mix_out_tail(x, o, gb, u0, um1, um2, wc_ref, wo_ref, gt, gpost, aw):
    y = wc_ref[0:1, :] * um2 + wc_ref[1:2, :] * um1 + wc_ref[2:3, :] * u0
    mixed = (jnp.dot(o.astype(BF16), wo_ref[0:aw, :], preferred_element_type=F32)
             + jnp.dot((gb * y).astype(BF16), wo_ref[aw:, :], preferred_element_type=F32))
    return x + gt * (_rms(mixed) * gpost)


def _mix_out_prompt_kernel(x_ref, o_ref_in, gb_ref, u_ref, halo_ref, gt_ref, gpost_ref, wc_ref, wo_ref,
                           out_ref, ext_ref, *, tm, tiles_per_seq):
    i = pl.program_id(0)
    halo = halo_ref[...]
    ext_ref[0:SUBLANES, :] = jnp.where(i % tiles_per_seq == 0, jnp.zeros_like(halo), halo)
    ext_ref[SUBLANES:, :] = u_ref[...]
    um2 = ext_ref[SUBLANES - 2:SUBLANES - 2 + tm, :]
    um1 = ext_ref[SUBLANES - 1:SUBLANES - 1 + tm, :]
    out_ref[...] = _mix_out_tail(x_ref[...], o_ref_in[...], gb_ref[...], u_ref[...], um1, um2,
                                 wc_ref, wo_ref, gt_ref[0], gpost_ref[...], ATTN_WIDTH)


def _mix_out_prompt(x, o, gb, u, gate, g_post, w_conv, w_o, tm, tiles_per_seq):
    rows, d = x.shape
    cw = u.shape[1]
    const = lambda i: (0, 0)
    row = lambda i: (i, 0)
    hb = tm // SUBLANES
    return pl.pallas_call(
        functools.partial(_mix_out_prompt_kernel, tm=tm, tiles_per_seq=tiles_per_seq),
        grid=(rows // tm,),
        in_specs=[pl.BlockSpec((tm, d), row),
                  pl.BlockSpec((tm, ATTN_WIDTH), row),
                  pl.BlockSpec((tm, cw), row),
                  pl.BlockSpec((tm, cw), row),
                  pl.BlockSpec((SUBLANES, cw), lambda i: (jnp.maximum(i * hb - 1, 0), 0)),
                  _mod_spec(gate, tm, tiles_per_seq),
                  pl.BlockSpec((1, d), const),
                  pl.BlockSpec(w_conv.shape, const),
                  pl.BlockSpec(w_o.shape, const)],
        out_specs=pl.BlockSpec((tm, d), row),
        out_shape=jax.ShapeDtypeStruct((rows, d), F32),
        scratch_shapes=[pltpu.VMEM((tm + SUBLANES, cw), F32)],
        compiler_params=_cparams(("arbitrary",)),
        name="mix_out_prompt",
    )(x, o, gb, u, u, gate, g_post, w_conv, w_o)


def _mix_out_sample_kernel(x_ref, o_ref_in, gb_ref, um2_ref, um1_ref, u0_ref, gt_ref, gpost_ref, wc_ref, wo_ref,
                           out_ref):
    out_ref[...] = _mix_out_tail(x_ref[...], o_ref_in[...], gb_ref[...], u0_ref[...], um1_ref[...], um2_ref[...],
                                 wc_ref, wo_ref, gt_ref[0], gpost_ref[...], ATTN_WIDTH)


def _mix_out_sample(x, o, gb, ext, gate, g_post, w_conv, w_o, tm):
    rows, d = x.shape
    cw = gb.shape[1]
    const = lambda i: (0, 0)
    row = lambda i: (i, 0)
    return pl.pallas_call(
        _mix_out_sample_kernel,
        grid=(rows // tm,),
        in_specs=[pl.BlockSpec((tm, d), row),
                  pl.BlockSpec((tm, ATTN_WIDTH), row),
                  pl.BlockSpec((tm, cw), row),
                  pl.BlockSpec((tm, cw), lambda i: (i, 0)),
                  pl.BlockSpec((tm, cw), lambda i: (i + 1, 0)),
                  pl.BlockSpec((tm, cw), lambda i: (i + 2, 0)),
                  pl.BlockSpec((1, tm, d), lambda i: (0, 0, 0)),
                  pl.BlockSpec((1, d), const),
                  pl.BlockSpec(w_conv.shape, const),
                  pl.BlockSpec(w_o.shape, const)],
        out_specs=pl.BlockSpec((tm, d), row),
        out_shape=jax.ShapeDtypeStruct((rows, d), F32),
        compiler_params=_cparams(("arbitrary",)),
        name="mix_out_sample",
    )(x, o, gb, ext, ext, ext, gate, g_post, w_conv, w_o)


def _prompt_layer(x, mod, lw, tm, tq):
    b, s, d = x.shape
    conv_ch = lw["w_conv"].shape[1]
    tps = s // tm
    md = lambda j, k: mod[:, j, k][:, None, :]
    xr = x.reshape(b * s, d)
    x1 = _ffn(xr, md(0, 0), md(0, 1), md(0, 2), lw["g_pre"][0:1], lw["g_post"][0:1], lw["ffn1_w13"], lw["ffn1_w2"],
              tm, tps)
    tabs = _rope_tables(jnp.arange(s))
    q, k, v, qi, kiwi, gb, u = _mix_in(x1, md(1, 0), md(1, 1), lw["g_pre"][1:2], lw["w_in_p"], tabs, tm, tps, tps,
                                       conv_ch)
    nb = s // tq
    kt = jnp.swapaxes(k.astype(BF16).reshape(b, nb, tq, KV_WIDTH), 2, 3)
    kit = jnp.swapaxes(kiwi[:, :IDX_DIM].astype(BF16).reshape(b, nb, tq, IDX_DIM), 2, 3)
    kit = jnp.concatenate([kit, kit], axis=2)
    topk = min(TOPK_MAX, s // 4)
    o = _prompt_attn(q.reshape(b, s, -1), qi.reshape(b, s, -1), kiwi.reshape(b, s, -1), kt,
                     v.astype(BF16).reshape(b, s, KV_WIDTH), kit, tq, topk)
    x2 = _mix_out_prompt(x1, o.reshape(b * s, -1), gb, u, md(1, 2), lw["g_post"][1:2], lw["w_conv"], lw["w_o"],
                         tm, tps)
    x3 = _ffn(x2, md(2, 0), md(2, 1), md(2, 2), lw["g_pre"][2:3], lw["g_post"][2:3], lw["ffn2_w13"], lw["ffn2_w2"],
              tm, tps)
    state = (k.reshape(b, s, N_KV_HEADS, HEAD_DIM), v.reshape(b, s, N_KV_HEADS, HEAD_DIM),
             kiwi[:, :IDX_DIM].reshape(b, s, IDX_DIM), u.reshape(b, s, conv_ch)[:, s - (CONV_K - 1):])
    return x3.reshape(b, s, d), state


def _sample_layer(x_tm, mod, lw, cache_k, cache_v, cache_idx_k, state_conv, page_table):
    n = mod.shape[0]
    t_tok = x_tm.shape[0] // n
    d = x_tm.shape[1]
    conv_ch = lw["w_conv"].shape[1]
    n_pages = page_table.shape[1]
    page = cache_k.shape[1]
    past = n_pages * page
    md = lambda j, k: mod[:, j, k][None]
    x1 = _ffn(x_tm, md(0, 0), md(0, 1), md(0, 2), lw["g_pre"][0:1], lw["g_post"][0:1], lw["ffn1_w13"],
              lw["ffn1_w2"], n, t_tok)
    pos = jnp.repeat(past + jnp.arange(t_tok), n)
    tabs = _rope_tables(pos)
    q, k, v, qi, kiwi, gb, u = _mix_in(x1, md(1, 0), md(1, 1), lw["g_pre"][1:2], lw["w_in_p"], tabs, n, t_tok, t_tok,
                                       conv_ch)
    tpad = SUBLANES
    assert t_tok <= tpad

    def seq_major(a, heads, width):
        a = a.reshape(t_tok, n, heads, width).transpose(1, 2, 0, 3)
        a = jnp.pad(a, ((0, 0), (0, 0), (0, tpad - t_tok), (0, 0)))
        return a.reshape(n, heads * tpad, width)

    q_seq = q.reshape(t_tok, n, N_HEADS, LANES).transpose(1, 0, 2, 3).reshape(n, t_tok * N_HEADS, LANES)
    qi_nat = jnp.concatenate([qi[:, h * LANES + (h % 2) * IDX_DIM:h * LANES + (h % 2 + 1) * IDX_DIM]
                              for h in range(IDX_HEADS)], axis=1)
    qih = seq_major(qi_nat, IDX_HEADS, IDX_DIM)
    wib = jnp.broadcast_to(seq_major(kiwi[:, IDX_DIM:IDX_DIM + IDX_HEADS], IDX_HEADS, 1),
                           (n, IDX_HEADS * tpad, LANES))

    def new_keys_t(a):
        a = a.reshape(t_tok, n, -1).transpose(1, 2, 0).astype(BF16)
        return jnp.pad(a, ((0, 0), (0, 0), (0, page - t_tok)))

    ck_t = cache_k.transpose(0, 2, 3, 1).reshape(-1, KV_WIDTH, page)
    cv_t = cache_v.transpose(0, 2, 3, 1).reshape(-1, KV_WIDTH, page)
    cik_t = cache_idx_k.transpose(0, 2, 1)
    topk = min(TOPK_MAX, (past + t_tok) // 4)
    pps = max(p for p in (16, 8, 4, 2, 1) if n_pages % p == 0)
    o = _sample_attn(page_table, q_seq, qih, wib, new_keys_t(kiwi[:, :IDX_DIM]), new_keys_t(k), new_keys_t(v),
                     cik_t, ck_t, cv_t, t_tok, tpad, topk, pps)
    o = o.reshape(n, t_tok, N_KV_HEADS, HEADS_PER_KV, N_KV_HEADS, HEAD_DIM)
    o = jnp.stack([o[:, :, g, :, g] for g in range(N_KV_HEADS)], axis=2)
    o_tm = o.transpose(1, 0, 2, 3, 4).reshape(t_tok * n, ATTN_WIDTH)
    ext = jnp.concatenate([state_conv.transpose(1, 0, 2).reshape(-1, conv_ch), u], axis=0)
    x2 = _mix_out_sample(x1, o_tm, gb, ext, md(1, 2), lw["g_post"][1:2], lw["w_conv"], lw["w_o"], n)
    x3 = _ffn(x2, md(2, 0), md(2, 1), md(2, 2), lw["g_pre"][2:3], lw["g_post"][2:3], lw["ffn2_w13"],
              lw["ffn2_w2"], n, t_tok)
    tm_to_seq = lambda a: a.reshape((t_tok, n) + a.shape[1:]).swapaxes(0, 1)
    state = (tm_to_seq(k).reshape(n, t_tok, N_KV_HEADS, HEAD_DIM), tm_to_seq(v).reshape(n, t_tok, N_KV_HEADS, HEAD_DIM),
             tm_to_seq(kiwi[:, :IDX_DIM]), tm_to_seq(u)[:, t_tok - (CONV_K - 1):])
    return x3, state


def _row_tile(s, target):
    t = min(target, s)
    while s % t:
        t //= 2
    return t


def kernel(x_prompt, x_sample, cache_k, cache_v, cache_idx_k, state_conv, page_table, c_prompt, c_sample, w_ada, b_ada, g_pre, g_post, ffn1_w13, ffn1_w2, w_in, w_conv, w_o, ffn2_w13, ffn2_w2):
    depth = w_ada.shape[0]
    b, s, d = x_prompt.shape
    n, t_tok, _ = x_sample.shape
    conv_ch = w_conv.shape[2]
    tm = _row_tile(s, 512)
    tq = _row_tile(s, 256)

    yp = x_prompt
    ys = x_sample.swapaxes(0, 1).reshape(t_tok * n, d)
    c_all = jnp.concatenate([c_prompt, c_sample], axis=0)
    n_c = c_all.shape[0]
    c_all = jnp.pad(c_all, ((0, -n_c % SUBLANES), (0, 0)))
    outs = [[] for _ in range(8)]
    for l in range(depth):
        lw = dict(g_pre=g_pre[l], g_post=g_post[l], w_conv=w_conv[l],
                  ffn1_w13=ffn1_w13[l].astype(BF16), ffn1_w2=ffn1_w2[l].astype(BF16),
                  ffn2_w13=ffn2_w13[l].astype(BF16), ffn2_w2=ffn2_w2[l].astype(BF16),
                  w_in_p=_pack_w_in(w_in[l], conv_ch), w_o=w_o[l].astype(BF16))
        mod = _ada_mod(c_all, w_ada[l], b_ada[l])[:n_c].reshape(n_c, N_SUB, 3, d)
        yp, st_p = _prompt_layer(yp, mod[:b], lw, tm, tq)
        ys, st_s = _sample_layer(ys, mod[b:], lw, cache_k[l], cache_v[l], cache_idx_k[l], state_conv[l], page_table)
        for lst, a in zip(outs, st_p + st_s):
            lst.append(a)
    ys = ys.reshape(t_tok, n, d).swapaxes(0, 1)
    return (yp, ys) + tuple(jnp.stack(lst) for lst in outs)
```

```python
import functools

import numpy as np
import jax
import jax.numpy as jnp
from jax import lax
from jax.experimental import pallas as pl
from jax.experimental.pallas import tpu as pltpu

F32 = jnp.float32
BF16 = jnp.bfloat16
I32 = jnp.int32

N_HEADS = 8
HEAD_DIM = 64
N_KV_HEADS = 2
HEADS_PER_KV = N_HEADS // N_KV_HEADS
ATTN_WIDTH = N_HEADS * HEAD_DIM
KV_WIDTH = N_KV_HEADS * HEAD_DIM
ROT_DIV = 4
ROT = HEAD_DIM // ROT_DIV
ROT_HALF = ROT // 2
ROPE_THETA = 500000.0
IDX_HEADS = 4
IDX_DIM = 64
IDX_SCALE = (IDX_HEADS * IDX_DIM) ** -0.5
TOPK_MAX = 256
CONV_K = 3
N_SUB = 3
RMS_EPS = 1e-6
Q_SCALE = HEAD_DIM ** -0.5 * float(np.log2(np.e))

LANES = 128
SUBLANES = 8
VMEM_LIMIT = 56 * 1024 * 1024

NEG_BIG = -1e30
F32_MAX = float(np.finfo(np.float32).max)
INT_MIN = -(2 ** 31)
INT_MAX = 2 ** 31 - 1

NT_DIMS = (((1,), (1,)), ((), ()))


def _cparams(sem):
    return pltpu.CompilerParams(dimension_semantics=sem, vmem_limit_bytes=VMEM_LIMIT)


def _silu(x):
    return x * (1.0 / (1.0 + jnp.exp(-x)))


def _rms(x):
    return x * lax.rsqrt(jnp.mean(x * x, axis=-1, keepdims=True) + RMS_EPS)


def _tile_lanes(x, reps):
    return x if reps == 1 else jnp.concatenate([x] * reps, axis=1)


def _ada_kernel(c_ref, w_ref, b_ref, o_ref):
    a = _silu(c_ref[...]).astype(BF16)
    o_ref[...] = jnp.dot(a, w_ref[...].astype(BF16), preferred_element_type=F32) + b_ref[...]


def _ada_mod(c, w_ada, b_ada):
    n, d = c.shape
    nout = w_ada.shape[1]
    tn = 9 * LANES if nout % (9 * LANES) == 0 else LANES
    return pl.pallas_call(
        _ada_kernel,
        grid=(nout // tn,),
        in_specs=[pl.BlockSpec((n, d), lambda j: (0, 0)),
                  pl.BlockSpec((d, tn), lambda j: (0, j)),
                  pl.BlockSpec((1, tn), lambda j: (0, j))],
        out_specs=pl.BlockSpec((n, tn), lambda j: (0, j)),
        out_shape=jax.ShapeDtypeStruct((n, nout), F32),
        compiler_params=_cparams(("arbitrary",)),
        name="ada_mod",
    )(c, w_ada, b_ada.reshape(1, nout))


def _ffn_kernel(x_ref, sh_ref, sc_ref, gt_ref, gpre_ref, gpost_ref, w13_ref, w2_ref, o_ref, *, d_ff, chunk):
    x = x_ref[...]
    h = _rms(x) * gpre_ref[...] * (1.0 + sc_ref[0]) + sh_ref[0]
    hb = h.astype(BF16)
    y = jnp.zeros(x.shape, F32)
    for c in range(d_ff // chunk):
        a = jnp.dot(hb, w13_ref[:, c * chunk:(c + 1) * chunk], preferred_element_type=F32)
        b = jnp.dot(hb, w13_ref[:, d_ff + c * chunk:d_ff + (c + 1) * chunk], preferred_element_type=F32)
        g = (_silu(a) * b).astype(BF16)
        y = y + jnp.dot(g, w2_ref[c * chunk:(c + 1) * chunk, :], preferred_element_type=F32)
    o_ref[...] = x + 0.5 * gt_ref[0] * (_rms(y) * gpost_ref[...])


def _mod_spec(mod, tm, tiles_per_group):
    _, r, d = mod.shape
    return pl.BlockSpec((1, r, d), lambda i: (i // tiles_per_group, 0, 0))


def _ffn(x, shift, scale, gate, g_pre, g_post, w13, w2, tm, tiles_per_group):
    rows, d = x.shape
    d_ff = w2.shape[0]
    const = lambda i: (0, 0)
    return pl.pallas_call(
        functools.partial(_ffn_kernel, d_ff=d_ff, chunk=min(512, d_ff)),
        grid=(rows // tm,),
        in_specs=[pl.BlockSpec((tm, d), lambda i: (i, 0)),
                  _mod_spec(shift, tm, tiles_per_group),
                  _mod_spec(scale, tm, tiles_per_group),
                  _mod_spec(gate, tm, tiles_per_group),
                  pl.BlockSpec((1, d), const),
                  pl.BlockSpec((1, d), const),
                  pl.BlockSpec(w13.shape, const),
                  pl.BlockSpec(w2.shape, const)],
        out_specs=pl.BlockSpec((tm, d), lambda i: (i, 0)),
        out_shape=jax.ShapeDtypeStruct((rows, d), F32),
        compiler_params=_cparams(("arbitrary",)),
        name="ffn",
    )(x, shift, scale, gate, g_pre, g_post, w13, w2)


_OFF_Q = 0
_OFF_K = _OFF_Q + ATTN_WIDTH
_OFF_V = _OFF_K + KV_WIDTH
_OFF_QI = _OFF_V + KV_WIDTH
_OFF_KIWI = _OFF_QI + IDX_HEADS * IDX_DIM
_OFF_GB = _OFF_KIWI + LANES


def _pack_w_in(w_in, conv_ch):
    splits = [ATTN_WIDTH, KV_WIDTH, KV_WIDTH, IDX_HEADS * IDX_DIM, IDX_DIM, IDX_HEADS, conv_ch, conv_ch, conv_ch]
    offs = np.cumsum([0] + splits)
    d = w_in.shape[0]
    pad = jnp.zeros((d, LANES - IDX_DIM - IDX_HEADS), w_in.dtype)
    return jnp.concatenate([w_in[:, :offs[6]], pad, w_in[:, offs[6]:]], axis=1).astype(BF16)


def _rope_tables(pos):
    inv = jnp.power(ROPE_THETA, -jnp.arange(ROT_HALF, dtype=F32) * 2.0 / ROT)
    ang = pos.astype(F32)[:, None] * inv[None, :]
    cos, sin = jnp.cos(ang), jnp.sin(ang)
    n = pos.shape[0]
    one = jnp.ones((n, HEAD_DIM - ROT), F32)
    zero_r = jnp.zeros((n, HEAD_DIM - ROT), F32)
    zero_h = jnp.zeros((n, ROT_HALF), F32)
    c = jnp.concatenate([cos, cos, one], axis=1)
    s1 = jnp.concatenate([-sin, zero_h, zero_r], axis=1)
    s2 = jnp.concatenate([zero_h, sin, zero_r], axis=1)
    rep = LANES // HEAD_DIM
    return jnp.tile(c, (1, rep)), jnp.tile(s1, (1, rep)), jnp.tile(s2, (1, rep))


def _rope128(x, c, s1, s2):
    return x * c + pltpu.roll(x, LANES - ROT_HALF, 1) * s1 + pltpu.roll(x, ROT_HALF, 1) * s2


def _mix_in_kernel(x_ref, sh_ref, sc_ref, gpre_ref, w_ref, rc_ref, rs1_ref, rs2_ref,
                   q_ref, k_ref, v_ref, qi_ref, kiwi_ref, gb_ref, u_ref, *, conv_ch):
    x = x_ref[...]
    hb = (_rms(x) * gpre_ref[...] * (1.0 + sc_ref[0]) + sh_ref[0]).astype(BF16)
    c, s1, s2 = rc_ref[...], rs1_ref[...], rs2_ref[...]

    def proj(off, width):
        return jnp.dot(hb, w_ref[:, off:off + width], preferred_element_type=F32)

    def rope_cols(z):
        return [_rope128(z[:, g * LANES:(g + 1) * LANES], c, s1, s2) for g in range(z.shape[1] // LANES)]

    lane = lax.broadcasted_iota(I32, (x.shape[0], LANES), 1)
    low = lane < HEAD_DIM

    zq = proj(_OFF_Q, ATTN_WIDTH)
    for g, r in enumerate(rope_cols(zq)):
        r = r * Q_SCALE
        swapped = pltpu.roll(r, HEAD_DIM, 1)
        for e in range(2):
            h = 2 * g + e
            kv_group = h // HEADS_PER_KV
            src = r if e == kv_group else swapped
            keep = low if kv_group == 0 else jnp.logical_not(low)
            q_ref[h] = jnp.where(keep, src, 0.0).astype(BF16)
    zk = proj(_OFF_K, KV_WIDTH)
    for g, r in enumerate(rope_cols(zk)):
        k_ref[:, g * LANES:(g + 1) * LANES] = r
    v_ref[...] = proj(_OFF_V, KV_WIDTH)
    zqi = proj(_OFF_QI, IDX_HEADS * IDX_DIM)
    for g, r in enumerate(rope_cols(zqi)):
        qi_ref[:, (2 * g) * LANES:(2 * g + 1) * LANES] = jnp.where(low, r, 0.0).astype(BF16)
        qi_ref[:, (2 * g + 1) * LANES:(2 * g + 2) * LANES] = jnp.where(low, 0.0, r).astype(BF16)
    zkw = proj(_OFF_KIWI, LANES)
    kiwi_ref[...] = jnp.where(lane < IDX_DIM, _rope128(zkw, c, s1, s2), zkw * IDX_SCALE)
    gb_ref[...] = proj(_OFF_GB, conv_ch)
    u_ref[...] = proj(_OFF_GB + conv_ch, conv_ch) * proj(_OFF_GB + 2 * conv_ch, conv_ch)


def _mix_in(x, shift, scale, g_pre, w_in_p, rope_tabs, tm, tiles_per_group, rope_tiles, conv_ch):
    rows, d = x.shape
    const = lambda i: (0, 0)
    row = lambda i: (i, 0)
    rope_spec = pl.BlockSpec((tm, LANES), lambda i: (i % rope_tiles, 0))
    assert 2 * HEAD_DIM == LANES and KV_WIDTH == LANES and 2 * IDX_DIM == LANES
    widths = [(KV_WIDTH, F32), (KV_WIDTH, F32), (IDX_HEADS * LANES, BF16),
              (LANES, F32), (conv_ch, F32), (conv_ch, F32)]
    q_spec = pl.BlockSpec((N_HEADS, tm, LANES), lambda i: (0, i, 0))
    q_shape = jax.ShapeDtypeStruct((N_HEADS, rows, LANES), BF16)
    return pl.pallas_call(
        functools.partial(_mix_in_kernel, conv_ch=conv_ch),
        grid=(rows // tm,),
        in_specs=[pl.BlockSpec((tm, d), row),
                  _mod_spec(shift, tm, tiles_per_group),
                  _mod_spec(scale, tm, tiles_per_group),
                  pl.BlockSpec((1, d), const),
                  pl.BlockSpec(w_in_p.shape, const),
                  rope_spec, rope_spec, rope_spec],
        out_specs=[q_spec] + [pl.BlockSpec((tm, w), row) for w, _ in widths],
        out_shape=[q_shape] + [jax.ShapeDtypeStruct((rows, w), dt) for w, dt in widths],
        compiler_params=_cparams(("arbitrary",)),
        name="mix_in",
    )(x, shift, scale, g_pre, w_in_p, *rope_tabs)


def _key_to_float(u):
    o = u ^ INT_MIN
    bits = jnp.where(o >= 0, o, o ^ INT_MAX)
    return lax.bitcast_convert_type(bits, F32)


def _count(sc_ref, nkb, r0, rg, tk, pred):
    def body(kb, acc):
        for j in range(tk // LANES):
            x = sc_ref[kb, r0:r0 + rg, j * LANES:(j + 1) * LANES]
            kpos = kb * tk + j * LANES + lax.broadcasted_iota(I32, (rg, LANES), 1)
            acc = acc + jnp.where(pred(x, kpos), 1.0, 0.0)
        return acc
    acc = lax.fori_loop(0, nkb, body, jnp.zeros((rg, LANES), F32))
    return jnp.broadcast_to(jnp.sum(acc, axis=1, keepdims=True), (rg, LANES))


def _select_rows(cnt, rg, topk, idx_bits, row_ok):
    def bit_body(i, key):
        cand = key | jnp.left_shift(jnp.int32(1), 31 - i)
        t = _key_to_float(cand)
        c = cnt(lambda x, kp: x >= t)
        return jnp.where(c >= topk, cand, key)

    key = lax.fori_loop(0, 32, bit_body, jnp.zeros((rg, LANES), I32))
    c_fin = cnt(lambda x, kp: x >= -F32_MAX)
    thr = jnp.where(c_fin >= topk, _key_to_float(key), -F32_MAX)
    c_gt = cnt(lambda x, kp: x > thr)
    c_ge = cnt(lambda x, kp: x >= thr)
    need = topk - c_gt
    excess = jnp.max(jnp.where((c_ge > topk) & row_ok, 1.0, 0.0)) > 0.5

    def tie_search():
        def jbody(i, p):
            cand = p | jnp.left_shift(jnp.int32(1), idx_bits - 1 - i)
            c = cnt(lambda x, kp: (x == thr) & (kp < cand))
            return jnp.where(c < need, cand, p)
        return lax.fori_loop(0, idx_bits, jbody, jnp.zeros((rg, LANES), I32))

    jmax = lax.cond(excess, tie_search, lambda: jnp.full((rg, LANES), INT_MAX, I32))
    return thr, jmax


def _select(sc_ref, t_ref, j_ref, nkb, rows, rg, tk, topk, idx_bits):
    for r0 in range(0, rows, rg):
        cnt = functools.partial(_count, sc_ref, nkb, r0, rg, tk)
        thr, jmax = _select_rows(cnt, rg, topk, idx_bits, True)
        t_ref[r0:r0 + rg, :] = thr
        j_ref[r0:r0 + rg, :] = jmax


def _sel_mask(scb, kpos, thr, jmax):
    return (scb > thr) | ((scb == thr) & (kpos <= jmax))


def _prompt_attn_kernel(q_ref, qi_ref, kiwi_ref, kt_ref, v_ref, kit_ref, o_ref,
                        sc_ref, wib_ref, m_ref, l_ref, acc_ref, t_ref, j_ref,
                        *, tq, topk, idx_bits):
    tk = tq
    reps = tk // LANES
    i = pl.program_id(1)
    nkb = i + 1

    kiwi = kiwi_ref[...]
    for h in range(IDX_HEADS):
        wib_ref[h] = jnp.broadcast_to(kiwi[:, IDX_DIM + h:IDX_DIM + h + 1], (tq, LANES))

    def score_body(kb, carry):
        kit = kit_ref[kb]
        acc = None
        for h in range(IDX_HEADS):
            s = jnp.dot(qi_ref[:, h * LANES:(h + 1) * LANES], kit, preferred_element_type=F32)
            t = jnp.maximum(s, 0.0) * _tile_lanes(wib_ref[h], reps)
            acc = t if acc is None else acc + t
        sc_ref[kb] = acc
        return carry

    lax.fori_loop(0, nkb, score_body, 0)
    row = lax.broadcasted_iota(I32, (tq, tk), 0)
    col = lax.broadcasted_iota(I32, (tq, tk), 1)
    sc_ref[i] = jnp.where(col <= row, sc_ref[i], -jnp.inf)

    _select(sc_ref, t_ref, j_ref, nkb, tq, min(128, tq), tk, topk, idx_bits)

    def bias_body(kb, carry):
        thr = _tile_lanes(t_ref[...], reps)
        jmax = _tile_lanes(j_ref[...], reps)
        sc_ref[kb] = jnp.where(_sel_mask(sc_ref[kb], kb * tk + col, thr, jmax), 0.0, NEG_BIG)
        return carry

    lax.fori_loop(0, nkb, bias_body, 0)

    m_ref[...] = jnp.full(m_ref.shape, NEG_BIG, F32)
    l_ref[...] = jnp.zeros(l_ref.shape, F32)
    acc_ref[...] = jnp.zeros(acc_ref.shape, F32)

    def attn_body(kb, carry):
        kt = kt_ref[kb]
        vb = v_ref[pl.ds(pl.multiple_of(kb * tk, tk), tk), :]
        bias = sc_ref[kb]
        for h in range(N_HEADS):
            s = jnp.dot(q_ref[h], kt, preferred_element_type=F32) + bias
            m_prev = m_ref[h]
            m_new = jnp.maximum(m_prev, jnp.max(s, axis=1, keepdims=True))
            alpha = jnp.exp2(m_prev - m_new)
            p = jnp.exp2(s - _tile_lanes(m_new, reps))
            l_ref[h] = alpha * l_ref[h] + sum(p[:, j * LANES:(j + 1) * LANES] for j in range(reps))
            acc_ref[h] = alpha * acc_ref[h] + jnp.dot(p.astype(BF16), vb, preferred_element_type=F32)
            m_ref[h] = m_new
        return carry

    lax.fori_loop(0, nkb, attn_body, 0)

    for h in range(N_HEADS):
        g = h // HEADS_PER_KV
        o = acc_ref[h] / jnp.sum(l_ref[h], axis=1, keepdims=True)
        o_ref[:, h * HEAD_DIM:(h + 1) * HEAD_DIM] = o[:, g * HEAD_DIM:(g + 1) * HEAD_DIM]


def _prompt_attn(q, qi, kiwi, kt, v, kit, tq, topk):
    b, s, _ = qi.shape
    nb = s // tq
    idx_bits = int(s).bit_length()
    blk = lambda w: pl.BlockSpec((None, tq, w), lambda bi, i: (bi, i, 0))
    full = lambda shp: pl.BlockSpec((None,) + shp, lambda bi, i: (bi,) + (0,) * len(shp))
    return pl.pallas_call(
        functools.partial(_prompt_attn_kernel, tq=tq, topk=topk, idx_bits=idx_bits),
        grid=(b, nb),
        in_specs=[pl.BlockSpec((N_HEADS, tq, LANES), lambda bi, i: (0, bi * nb + i, 0)),
                  blk(IDX_HEADS * LANES), blk(LANES),
                  full((nb, KV_WIDTH, tq)), full((s, KV_WIDTH)), full((nb, 2 * IDX_DIM, tq))],
        out_specs=blk(ATTN_WIDTH),
        out_shape=jax.ShapeDtypeStruct((b, s, ATTN_WIDTH), F32),
        scratch_shapes=[pltpu.VMEM((nb, tq, tq), F32),
                        pltpu.VMEM((IDX_HEADS, tq, LANES), F32),
                        pltpu.VMEM((N_HEADS, tq, LANES), F32),
                        pltpu.VMEM((N_HEADS, tq, LANES), F32),
                        pltpu.VMEM((N_HEADS, tq, KV_WIDTH), F32),
                        pltpu.VMEM((tq, LANES), F32),
                        pltpu.VMEM((tq, LANES), I32)],
        compiler_params=_cparams(("arbitrary", "arbitrary")),
        name="prompt_attn",
    )(q, qi, kiwi, kt, v, kit)


def _sample_attn_kernel(pt_ref, q_ref, qih_ref, wib_ref, kin_ref, kn_ref, vn_ref, *rest,
                        n_steps, pps, n_tok, tpad, page, topk, idx_bits):
    cik_refs, ck_refs, cv_refs = rest[:pps], rest[pps:2 * pps], rest[2 * pps:3 * pps]
    o_ref, sc_ref, sa_ref, vs_ref = rest[3 * pps:]
    step = pl.program_id(1)
    n_pages = n_steps * pps
    nkb = n_pages + 1
    qih = qih_ref[...]
    wib = wib_ref[...]
    q = q_ref[...]

    def idx_score(s):
        acc = None
        for h in range(IDX_HEADS):
            t = jnp.maximum(s[h * tpad:(h + 1) * tpad], 0.0) * wib[h * tpad:(h + 1) * tpad]
            acc = t if acc is None else acc + t
        return acc

    for j in range(pps):
        pg = step * pps + j
        sc_ref[pg] = idx_score(jnp.dot(qih, cik_refs[j][...].astype(BF16), preferred_element_type=F32))
        sa_ref[pg] = jnp.dot(q, ck_refs[j][...].astype(BF16), preferred_element_type=F32)
        vs_ref[pg] = cv_refs[j][...].astype(BF16)

    @pl.when(step == n_steps - 1)
    def _():
        s_new = idx_score(jnp.dot(qih, kin_ref[...], preferred_element_type=F32))
        tok = lax.broadcasted_iota(I32, (tpad, page), 0)
        col = lax.broadcasted_iota(I32, (tpad, page), 1)
        sc_ref[n_pages] = jnp.where((col <= tok) & (col < n_tok), s_new, -jnp.inf)
        sa_ref[n_pages] = jnp.dot(q, kn_ref[...], preferred_element_type=F32)
        vs_ref[n_pages] = vn_ref[...]

        kpos = (lax.broadcasted_iota(I32, (nkb, tpad, page), 0) * page
                + lax.broadcasted_iota(I32, (nkb, tpad, page), 2))

        def cnt(pred):
            c = jnp.sum(jnp.where(pred(sc_ref[...], kpos), 1.0, 0.0), axis=0)
            return jnp.broadcast_to(jnp.sum(c, axis=1, keepdims=True), c.shape)

        thr, jmax = _select_rows(cnt, tpad, topk, idx_bits, tok[:, :LANES] < n_tok)
        sc_ref[...] = jnp.where(_sel_mask(sc_ref[...], kpos, thr[None], jmax[None]), 0.0, NEG_BIG)

        heads = N_HEADS
        inv_l = []
        for t in range(n_tok):
            s = sa_ref[:, t * heads:(t + 1) * heads, :] + sc_ref[:, t:t + 1, :]
            m = jnp.max(jnp.max(s, axis=0), axis=1, keepdims=True)
            pr = jnp.exp2(s - m[None])
            sa_ref[:, t * heads:(t + 1) * heads, :] = pr
            l = jnp.sum(jnp.sum(pr, axis=0), axis=1, keepdims=True)
            inv_l.append(jnp.broadcast_to(1.0 / l, (heads, KV_WIDTH)))

        def pv_body(kb, acc):
            return acc + lax.dot_general(sa_ref[kb].astype(BF16), vs_ref[kb], NT_DIMS, preferred_element_type=F32)

        acc = lax.fori_loop(0, nkb, pv_body, jnp.zeros((n_tok * heads, KV_WIDTH), F32), unroll=5)
        o_ref[...] = acc * jnp.concatenate(inv_l, axis=0)


def _sample_attn(page_table, q, qih, wib, kin_t, kn_t, vn_t, cik_t, ck_t, cv_t, n_tok, tpad, topk, pps):
    n, n_pages = page_table.shape
    page = ck_t.shape[2]
    assert page == LANES and KV_WIDTH == LANES and n_pages % pps == 0
    n_steps = n_pages // pps
    idx_bits = int((n_pages + 1) * page).bit_length()
    rows = n_tok * N_HEADS
    seq = lambda r, w: pl.BlockSpec((None, r, w), lambda b, p, pt: (b, 0, 0))

    def paged(r, j):
        return pl.BlockSpec((None, r, page), lambda b, p, pt: (pt[b, p * pps + j], 0, 0))

    grid_spec = pltpu.PrefetchScalarGridSpec(
        num_scalar_prefetch=1,
        grid=(n, n_steps),
        in_specs=([seq(rows, KV_WIDTH), seq(IDX_HEADS * tpad, IDX_DIM), seq(IDX_HEADS * tpad, LANES),
                   seq(IDX_DIM, page), seq(KV_WIDTH, page), seq(KV_WIDTH, page)]
                  + [paged(IDX_DIM, j) for j in range(pps)]
                  + [paged(KV_WIDTH, j) for j in range(pps)]
                  + [paged(KV_WIDTH, j) for j in range(pps)]),
        out_specs=seq(rows, KV_WIDTH),
        scratch_shapes=[pltpu.VMEM((n_pages + 1, tpad, page), F32),
                        pltpu.VMEM((n_pages + 1, rows, page), F32),
                        pltpu.VMEM((n_pages + 1, KV_WIDTH, page), BF16)],
    )
    return pl.pallas_call(
        functools.partial(_sample_attn_kernel, n_steps=n_steps, pps=pps, n_tok=n_tok, tpad=tpad, page=page,
                          topk=topk, idx_bits=idx_bits),
        grid_spec=grid_spec,
        out_shape=jax.ShapeDtypeStruct((n, rows, KV_WIDTH), F32),
        compiler_params=_cparams(("arbitrary", "arbitrary")),
        name="sample_attn",
    )(page_table, q, qih, wib, kin_t, kn_t, vn_t, *([cik_t] * pps), *([ck_t] * pps), *([cv_t] * pps))


def _mix_out_tail(x, o, gb, u0, um1, um2, wc_ref, wo_ref, gt, gpost, aw):
    y = wc_ref[0:1, :] * um2 + wc_ref[1:2, :] * um1 + wc_ref[2:3, :] * u0
    mixed = (jnp.dot(o.astype(BF16), wo_ref[0:aw, :], preferred_element_type=F32)
             + jnp.dot((gb * y).astype(BF16), wo_ref[aw:, :], preferred_element_type=F32))
    return x + gt * (_rms(mixed) * gpost)


def _mix_out_prompt_kernel(x_ref, o_ref_in, gb_ref, u_ref, halo_ref, gt_ref, gpost_ref, wc_ref, wo_ref,
                           out_ref, ext_ref, *, tm, tiles_per_seq):
    i = pl.program_id(0)
    halo = halo_ref[...]
    ext_ref[0:SUBLANES, :] = jnp.where(i % tiles_per_seq == 0, jnp.zeros_like(halo), halo)
    ext_ref[SUBLANES:, :] = u_ref[...]
    um2 = ext_ref[SUBLANES - 2:SUBLANES - 2 + tm, :]
    um1 = ext_ref[SUBLANES - 1:SUBLANES - 1 + tm, :]
    out_ref[...] = _mix_out_tail(x_ref[...], o_ref_in[...], gb_ref[...], u_ref[...], um1, um2,
                                 wc_ref, wo_ref, gt_ref[0], gpost_ref[...], ATTN_WIDTH)


def _mix_out_prompt(x, o, gb, u, gate, g_post, w_conv, w_o, tm, tiles_per_seq):
    rows, d = x.shape
    cw = u.shape[1]
    const = lambda i: (0, 0)
    row = lambda i: (i, 0)
    hb = tm // SUBLANES
    return pl.pallas_call(
        functools.partial(_mix_out_prompt_kernel, tm=tm, tiles_per_seq=tiles_per_seq),
        grid=(rows // tm,),
        in_specs=[pl.BlockSpec((tm, d), row),
                  pl.BlockSpec((tm, ATTN_WIDTH), row),
                  pl.BlockSpec((tm, cw), row),
                  pl.BlockSpec((tm, cw), row),
                  pl.BlockSpec((SUBLANES, cw), lambda i: (jnp.maximum(i * hb - 1, 0), 0)),
                  _mod_spec(gate, tm, tiles_per_seq),
                  pl.BlockSpec((1, d), const),
                  pl.BlockSpec(w_conv.shape, const),
                  pl.BlockSpec(w_o.shape, const)],
        out_specs=pl.BlockSpec((tm, d), row),
        out_shape=jax.ShapeDtypeStruct((rows, d), F32),
        scratch_shapes=[pltpu.VMEM((tm + SUBLANES, cw), F32)],
        compiler_params=_cparams(("arbitrary",)),
        name="mix_out_prompt",
    )(x, o, gb, u, u, gate, g_post, w_conv, w_o)


def _mix_out_sample_kernel(x_ref, o_ref_in, gb_ref, um2_ref, um1_ref, u0_ref, gt_ref, gpost_ref, wc_ref, wo_ref,
                           out_ref):
    out_ref[...] = _mix_out_tail(x_ref[...], o_ref_in[...], gb_ref[...], u0_ref[...], um1_ref[...], um2_ref[...],
                                 wc_ref, wo_ref, gt_ref[0], gpost_ref[...], ATTN_WIDTH)


def _mix_out_sample(x, o, gb, ext, gate, g_post, w_conv, w_o, tm):
    rows, d = x.shape
    cw = gb.shape[1]
    const = lambda i: (0, 0)
    row = lambda i: (i, 0)
    return pl.pallas_call(
        _mix_out_sample_kernel,
        grid=(rows // tm,),
        in_specs=[pl.BlockSpec((tm, d), row),
                  pl.BlockSpec((tm, ATTN_WIDTH), row),
                  pl.BlockSpec((tm, cw), row),
                  pl.BlockSpec((tm, cw), lambda i: (i, 0)),
                  pl.BlockSpec((tm, cw), lambda i: (i + 1, 0)),
                  pl.BlockSpec((tm, cw), lambda i: (i + 2, 0)),
                  pl.BlockSpec((1, tm, d), lambda i: (0, 0, 0)),
                  pl.BlockSpec((1, d), const),
                  pl.BlockSpec(w_conv.shape, const),
                  pl.BlockSpec(w_o.shape, const)],
        out_specs=pl.BlockSpec((tm, d), row),
        out_shape=jax.ShapeDtypeStruct((rows, d), F32),
        compiler_params=_cparams(("arbitrary",)),
        name="mix_out_sample",
    )(x, o, gb, ext, ext, ext, gate, g_post, w_conv, w_o)


def _prompt_layer(x, mod, lw, tm, tq):
    b, s, d = x.shape
    conv_ch = lw["w_conv"].shape[1]
    tps = s // tm
    md = lambda j, k: mod[:, j, k][:, None, :]
    xr = x.reshape(b * s, d)
    x1 = _ffn(xr, md(0, 0), md(0, 1), md(0, 2), lw["g_pre"][0:1], lw["g_post"][0:1], lw["ffn1_w13"], lw["ffn1_w2"],
              tm, tps)
    tabs = _rope_tables(jnp.arange(s))
    q, k, v, qi, kiwi, gb, u = _mix_in(x1, md(1, 0), md(1, 1), lw["g_pre"][1:2], lw["w_in_p"], tabs, tm, tps, tps,
                                       conv_ch)
    nb = s // tq
    kt = jnp.swapaxes(k.astype(BF16).reshape(b, nb, tq, KV_WIDTH), 2, 3)
    kit = jnp.swapaxes(kiwi[:, :IDX_DIM].astype(BF16).reshape(b, nb, tq, IDX_DIM), 2, 3)
    kit = jnp.concatenate([kit, kit], axis=2)
    topk = min(TOPK_MAX, s // 4)
    o = _prompt_attn(q, qi.reshape(b, s, -1), kiwi.reshape(b, s, -1), kt,
                     v.astype(BF16).reshape(b, s, KV_WIDTH), kit, tq, topk)
    x2 = _mix_out_prompt(x1, o.reshape(b * s, -1), gb, u, md(1, 2), lw["g_post"][1:2], lw["w_conv"], lw["w_o"],
                         tm, tps)
    x3 = _ffn(x2, md(2, 0), md(2, 1), md(2, 2), lw["g_pre"][2:3], lw["g_post"][2:3], lw["ffn2_w13"], lw["ffn2_w2"],
              tm, tps)
    state = (k.reshape(b, s, N_KV_HEADS, HEAD_DIM), v.reshape(b, s, N_KV_HEADS, HEAD_DIM),
             kiwi[:, :IDX_DIM].reshape(b, s, IDX_DIM), u.reshape(b, s, conv_ch)[:, s - (CONV_K - 1):])
    return x3.reshape(b, s, d), state


def _sample_layer(x_tm, mod, lw, cache_k, cache_v, cache_idx_k, state_conv, page_table):
    n = mod.shape[0]
    t_tok = x_tm.shape[0] // n
    d = x_tm.shape[1]
    conv_ch = lw["w_conv"].shape[1]
    n_pages = page_table.shape[1]
    page = cache_k.shape[1]
    past = n_pages * page
    md = lambda j, k: mod[:, j, k][None]
    x1 = _ffn(x_tm, md(0, 0), md(0, 1), md(0, 2), lw["g_pre"][0:1], lw["g_post"][0:1], lw["ffn1_w13"],
              lw["ffn1_w2"], n, t_tok)
    pos = jnp.repeat(past + jnp.arange(t_tok), n)
    tabs = _rope_tables(pos)
    q, k, v, qi, kiwi, gb, u = _mix_in(x1, md(1, 0), md(1, 1), lw["g_pre"][1:2], lw["w_in_p"], tabs, n, t_tok, t_tok,
                                       conv_ch)
    tpad = SUBLANES
    assert t_tok <= tpad

    def seq_major(a, heads, width):
        a = a.reshape(t_tok, n, heads, width).transpose(1, 2, 0, 3)
        a = jnp.pad(a, ((0, 0), (0, 0), (0, tpad - t_tok), (0, 0)))
        return a.reshape(n, heads * tpad, width)

    q_seq = q.reshape(N_HEADS, t_tok, n, LANES).transpose(2, 1, 0, 3).reshape(n, t_tok * N_HEADS, LANES)
    qi_nat = jnp.concatenate([qi[:, h * LANES + (h % 2) * IDX_DIM:h * LANES + (h % 2 + 1) * IDX_DIM]
                              for h in range(IDX_HEADS)], axis=1)
    qih = seq_major(qi_nat, IDX_HEADS, IDX_DIM)
    wib = jnp.broadcast_to(seq_major(kiwi[:, IDX_DIM:IDX_DIM + IDX_HEADS], IDX_HEADS, 1),
                           (n, IDX_HEADS * tpad, LANES))

    def new_keys_t(a):
        a = a.reshape(t_tok, n, -1).transpose(1, 2, 0).astype(BF16)
        return jnp.pad(a, ((0, 0), (0, 0), (0, page - t_tok)))

    ck_t = cache_k.transpose(0, 2, 3, 1).reshape(-1, KV_WIDTH, page)
    cv_t = cache_v.transpose(0, 2, 3, 1).reshape(-1, KV_WIDTH, page)
    cik_t = cache_idx_k.transpose(0, 2, 1)
    topk = min(TOPK_MAX, (past + t_tok) // 4)
    pps = max(p for p in (16, 8, 4, 2, 1) if n_pages % p == 0)
    o = _sample_attn(page_table, q_seq, qih, wib, new_keys_t(kiwi[:, :IDX_DIM]), new_keys_t(k), new_keys_t(v),
                     cik_t, ck_t, cv_t, t_tok, tpad, topk, pps)
    o = o.reshape(n, t_tok, N_KV_HEADS, HEADS_PER_KV, N_KV_HEADS, HEAD_DIM)
    o = jnp.stack([o[:, :, g, :, g] for g in range(N_KV_HEADS)], axis=2)
    o_tm = o.transpose(1, 0, 2, 3, 4).reshape(t_tok * n, ATTN_WIDTH)
    ext = jnp.concatenate([state_conv.transpose(1, 0, 2).reshape(-1, conv_ch), u], axis=0)
    x2 = _mix_out_sample(x1, o_tm, gb, ext, md(1, 2), lw["g_post"][1:2], lw["w_conv"], lw["w_o"], n)
    x3 = _ffn(x2, md(2, 0), md(2, 1), md(2, 2), lw["g_pre"][2:3], lw["g_post"][2:3], lw["ffn2_w13"],
              lw["ffn2_w2"], n, t_tok)
    tm_to_seq = lambda a: a.reshape((t_tok, n) + a.shape[1:]).swapaxes(0, 1)
    state = (tm_to_seq(k).reshape(n, t_tok, N_KV_HEADS, HEAD_DIM), tm_to_seq(v).reshape(n, t_tok, N_KV_HEADS, HEAD_DIM),
             tm_to_seq(kiwi[:, :IDX_DIM]), tm_to_seq(u)[:, t_tok - (CONV_K - 1):])
    return x3, state


def _row_tile(s, target):
    t = min(target, s)
    while s % t:
        t //= 2
    return t


def kernel(x_prompt, x_sample, cache_k, cache_v, cache_idx_k, state_conv, page_table, c_prompt, c_sample, w_ada, b_ada, g_pre, g_post, ffn1_w13, ffn1_w2, w_in, w_conv, w_o, ffn2_w13, ffn2_w2):
    depth = w_ada.shape[0]
    b, s, d = x_prompt.shape
    n, t_tok, _ = x_sample.shape
    conv_ch = w_conv.shape[2]
    tm = _row_tile(s, 512)
    tq = _row_tile(s, 256)

    yp = x_prompt
    ys = x_sample.swapaxes(0, 1).reshape(t_tok * n, d)
    c_all = jnp.concatenate([c_prompt, c_sample], axis=0)
    n_c = c_all.shape[0]
    c_all = jnp.pad(c_all, ((0, -n_c % SUBLANES), (0, 0)))
    outs = [[] for _ in range(8)]
    for l in range(depth):
        lw = dict(g_pre=g_pre[l], g_post=g_post[l], w_conv=w_conv[l],
                  ffn1_w13=ffn1_w13[l].astype(BF16), ffn1_w2=ffn1_w2[l].astype(BF16),
                  ffn2_w13=ffn2_w13[l].astype(BF16), ffn2_w2=ffn2_w2[l].astype(BF16),
                  w_in_p=_pack_w_in(w_in[l], conv_ch), w_o=w_o[l].astype(BF16))
        mod = _ada_mod(c_all, w_ada[l], b_ada[l])[:n_c].reshape(n_c, N_SUB, 3, d)
        yp, st_p = _prompt_layer(yp, mod[:b], lw, tm, tq)
        ys, st_s = _sample_layer(ys, mod[b:], lw, cache_k[l], cache_v[l], cache_idx_k[l], state_conv[l], page_table)
        for lst, a in zip(outs, st_p + st_s):
            lst.append(a)
    ys = ys.reshape(t_tok, n, d).swapaxes(0, 1)
    return (yp, ys) + tuple(jnp.stack(lst) for lst in outs)
```

```python
import functools

import numpy as np
import jax
import jax.numpy as jnp
from jax import lax
from jax.experimental import pallas as pl
from jax.experimental.pallas import tpu as pltpu

F32 = jnp.float32
BF16 = jnp.bfloat16
I32 = jnp.int32
I16 = jnp.int16

N_HEADS = 8
HEAD_DIM = 64
N_KV_HEADS = 2
HEADS_PER_KV = N_HEADS // N_KV_HEADS
ATTN_WIDTH = N_HEADS * HEAD_DIM
KV_WIDTH = N_KV_HEADS * HEAD_DIM
ROT_DIV = 4
ROT = HEAD_DIM // ROT_DIV
ROT_HALF = ROT // 2
ROPE_THETA = 500000.0
IDX_HEADS = 4
IDX_DIM = 64
IDX_SCALE = (IDX_HEADS * IDX_DIM) ** -0.5
TOPK_MAX = 256
CONV_K = 3
N_SUB = 3
RMS_EPS = 1e-6
Q_SCALE = HEAD_DIM ** -0.5 * float(np.log2(np.e))

LANES = 128
SUBLANES = 8
VMEM_LIMIT = 56 * 1024 * 1024

NEG_BIG = -1e30
F32_MAX = float(np.finfo(np.float32).max)
INT_MIN = -(2 ** 31)
INT_MAX = 2 ** 31 - 1

NT_DIMS = (((1,), (1,)), ((), ()))


def _cparams(sem):
    return pltpu.CompilerParams(dimension_semantics=sem, vmem_limit_bytes=VMEM_LIMIT)


def _silu(x):
    return x * (1.0 / (1.0 + jnp.exp(-x)))


def _rms(x):
    return x * lax.rsqrt(jnp.mean(x * x, axis=-1, keepdims=True) + RMS_EPS)


def _tile_lanes(x, reps):
    return x if reps == 1 else jnp.concatenate([x] * reps, axis=1)


def _ada_kernel(c_ref, w_ref, b_ref, o_ref):
    a = _silu(c_ref[...]).astype(BF16)
    o_ref[...] = jnp.dot(a, w_ref[...].astype(BF16), preferred_element_type=F32) + b_ref[...]


def _ada_mod(c, w_ada, b_ada):
    n, d = c.shape
    nout = w_ada.shape[1]
    tn = 9 * LANES if nout % (9 * LANES) == 0 else LANES
    return pl.pallas_call(
        _ada_kernel,
        grid=(nout // tn,),
        in_specs=[pl.BlockSpec((n, d), lambda j: (0, 0)),
                  pl.BlockSpec((d, tn), lambda j: (0, j)),
                  pl.BlockSpec((1, tn), lambda j: (0, j))],
        out_specs=pl.BlockSpec((n, tn), lambda j: (0, j)),
        out_shape=jax.ShapeDtypeStruct((n, nout), F32),
        compiler_params=_cparams(("arbitrary",)),
        name="ada_mod",
    )(c, w_ada, b_ada.reshape(1, nout))


def _ffn_kernel(x_ref, sh_ref, sc_ref, gt_ref, gpre_ref, gpost_ref, w13_ref, w2_ref, o_ref, *, d_ff, chunk):
    x = x_ref[...]
    h = _rms(x) * gpre_ref[...] * (1.0 + sc_ref[0]) + sh_ref[0]
    hb = h.astype(BF16)
    y = jnp.zeros(x.shape, F32)
    for c in range(d_ff // chunk):
        a = jnp.dot(hb, w13_ref[:, c * chunk:(c + 1) * chunk], preferred_element_type=F32)
        b = jnp.dot(hb, w13_ref[:, d_ff + c * chunk:d_ff + (c + 1) * chunk], preferred_element_type=F32)
        g = (_silu(a) * b).astype(BF16)
        y = y + jnp.dot(g, w2_ref[c * chunk:(c + 1) * chunk, :], preferred_element_type=F32)
    o_ref[...] = x + 0.5 * gt_ref[0] * (_rms(y) * gpost_ref[...])


def _mod_spec(mod, tm, tiles_per_group):
    _, r, d = mod.shape
    return pl.BlockSpec((1, r, d), lambda i: (i // tiles_per_group, 0, 0))


def _ffn(x, shift, scale, gate, g_pre, g_post, w13, w2, tm, tiles_per_group):
    rows, d = x.shape
    d_ff = w2.shape[0]
    const = lambda i: (0, 0)
    return pl.pallas_call(
        functools.partial(_ffn_kernel, d_ff=d_ff, chunk=min(512, d_ff)),
        grid=(rows // tm,),
        in_specs=[pl.BlockSpec((tm, d), lambda i: (i, 0)),
                  _mod_spec(shift, tm, tiles_per_group),
                  _mod_spec(scale, tm, tiles_per_group),
                  _mod_spec(gate, tm, tiles_per_group),
                  pl.BlockSpec((1, d), const),
                  pl.BlockSpec((1, d), const),
                  pl.BlockSpec(w13.shape, const),
                  pl.BlockSpec(w2.shape, const)],
        out_specs=pl.BlockSpec((tm, d), lambda i: (i, 0)),
        out_shape=jax.ShapeDtypeStruct((rows, d), F32),
        compiler_params=_cparams(("arbitrary",)),
        name="ffn",
    )(x, shift, scale, gate, g_pre, g_post, w13, w2)


_OFF_Q = 0
_OFF_K = _OFF_Q + ATTN_WIDTH
_OFF_V = _OFF_K + KV_WIDTH
_OFF_QI = _OFF_V + KV_WIDTH
_OFF_KIWI = _OFF_QI + IDX_HEADS * IDX_DIM
_OFF_GB = _OFF_KIWI + LANES


def _pack_w_in(w_in, conv_ch):
    splits = [ATTN_WIDTH, KV_WIDTH, KV_WIDTH, IDX_HEADS * IDX_DIM, IDX_DIM, IDX_HEADS, conv_ch, conv_ch, conv_ch]
    offs = np.cumsum([0] + splits)
    d = w_in.shape[0]
    pad = jnp.zeros((d, LANES - IDX_DIM - IDX_HEADS), w_in.dtype)
    return jnp.concatenate([w_in[:, :offs[6]], pad, w_in[:, offs[6]:]], axis=1).astype(BF16)


def _rope_tables(pos):
    inv = jnp.power(ROPE_THETA, -jnp.arange(ROT_HALF, dtype=F32) * 2.0 / ROT)
    ang = pos.astype(F32)[:, None] * inv[None, :]
    cos, sin = jnp.cos(ang), jnp.sin(ang)
    n = pos.shape[0]
    one = jnp.ones((n, HEAD_DIM - ROT), F32)
    zero_r = jnp.zeros((n, HEAD_DIM - ROT), F32)
    zero_h = jnp.zeros((n, ROT_HALF), F32)
    c = jnp.concatenate([cos, cos, one], axis=1)
    s1 = jnp.concatenate([-sin, zero_h, zero_r], axis=1)
    s2 = jnp.concatenate([zero_h, sin, zero_r], axis=1)
    rep = LANES // HEAD_DIM
    return jnp.tile(c, (1, rep)), jnp.tile(s1, (1, rep)), jnp.tile(s2, (1, rep))


def _rope128(x, c, s1, s2):
    return x * c + pltpu.roll(x, LANES - ROT_HALF, 1) * s1 + pltpu.roll(x, ROT_HALF, 1) * s2


def _mix_in_kernel(x_ref, sh_ref, sc_ref, gpre_ref, w_ref, rc_ref, rs1_ref, rs2_ref,
                   q_ref, k_ref, v_ref, qi_ref, kiwi_ref, gb_ref, u_ref, *, conv_ch):
    x = x_ref[...]
    hb = (_rms(x) * gpre_ref[...] * (1.0 + sc_ref[0]) + sh_ref[0]).astype(BF16)
    c, s1, s2 = rc_ref[...], rs1_ref[...], rs2_ref[...]

    def proj(off, width):
        return jnp.dot(hb, w_ref[:, off:off + width], preferred_element_type=F32)

    def rope_cols(z):
        return [_rope128(z[:, g * LANES:(g + 1) * LANES], c, s1, s2) for g in range(z.shape[1] // LANES)]

    lane = lax.broadcasted_iota(I32, (x.shape[0], LANES), 1)
    low = lane < HEAD_DIM

    zq = proj(_OFF_Q, ATTN_WIDTH)
    for g, r in enumerate(rope_cols(zq)):
        r = r * Q_SCALE
        swapped = pltpu.roll(r, HEAD_DIM, 1)
        for e in range(2):
            h = 2 * g + e
            kv_group = h // HEADS_PER_KV
            src = r if e == kv_group else swapped
            keep = low if kv_group == 0 else jnp.logical_not(low)
            q_ref[h] = jnp.where(keep, src, 0.0).astype(BF16)
    zk = proj(_OFF_K, KV_WIDTH)
    for g, r in enumerate(rope_cols(zk)):
        k_ref[:, g * LANES:(g + 1) * LANES] = r
    v_ref[...] = proj(_OFF_V, KV_WIDTH)
    zqi = proj(_OFF_QI, IDX_HEADS * IDX_DIM)
    for g, r in enumerate(rope_cols(zqi)):
        qi_ref[:, (2 * g) * LANES:(2 * g + 1) * LANES] = jnp.where(low, r, 0.0).astype(BF16)
        qi_ref[:, (2 * g + 1) * LANES:(2 * g + 2) * LANES] = jnp.where(low, 0.0, r).astype(BF16)
    zkw = proj(_OFF_KIWI, LANES)
    kiwi_ref[...] = jnp.where(lane < IDX_DIM, _rope128(zkw, c, s1, s2), zkw * IDX_SCALE)
    gb_ref[...] = proj(_OFF_GB, conv_ch)
    u_ref[...] = proj(_OFF_GB + conv_ch, conv_ch) * proj(_OFF_GB + 2 * conv_ch, conv_ch)


def _mix_in(x, shift, scale, g_pre, w_in_p, rope_tabs, tm, tiles_per_group, rope_tiles, conv_ch):
    rows, d = x.shape
    const = lambda i: (0, 0)
    row = lambda i: (i, 0)
    rope_spec = pl.BlockSpec((tm, LANES), lambda i: (i % rope_tiles, 0))
    assert 2 * HEAD_DIM == LANES and KV_WIDTH == LANES and 2 * IDX_DIM == LANES
    widths = [(KV_WIDTH, F32), (KV_WIDTH, F32), (IDX_HEADS * LANES, BF16),
              (LANES, F32), (conv_ch, F32), (conv_ch, F32)]
    q_spec = pl.BlockSpec((N_HEADS, tm, LANES), lambda i: (0, i, 0))
    q_shape = jax.ShapeDtypeStruct((N_HEADS, rows, LANES), BF16)
    return pl.pallas_call(
        functools.partial(_mix_in_kernel, conv_ch=conv_ch),
        grid=(rows // tm,),
        in_specs=[pl.BlockSpec((tm, d), row),
                  _mod_spec(shift, tm, tiles_per_group),
                  _mod_spec(scale, tm, tiles_per_group),
                  pl.BlockSpec((1, d), const),
                  pl.BlockSpec(w_in_p.shape, const),
                  rope_spec, rope_spec, rope_spec],
        out_specs=[q_spec] + [pl.BlockSpec((tm, w), row) for w, _ in widths],
        out_shape=[q_shape] + [jax.ShapeDtypeStruct((rows, w), dt) for w, dt in widths],
        compiler_params=_cparams(("arbitrary",)),
        name="mix_in",
    )(x, shift, scale, g_pre, w_in_p, *rope_tabs)


def _key_to_float(u):
    o = u ^ INT_MIN
    bits = jnp.where(o >= 0, o, o ^ INT_MAX)
    return lax.bitcast_convert_type(bits, F32)


def _count(sc_ref, nkb, r0, rg, tk, pred):
    def body(kb, acc):
        for j in range(tk // LANES):
            x = sc_ref[kb, r0:r0 + rg, j * LANES:(j + 1) * LANES]
            kpos = kb * tk + j * LANES + lax.broadcasted_iota(I32, (rg, LANES), 1)
            acc = acc + jnp.where(pred(x, kpos), 1.0, 0.0)
        return acc
    acc = lax.fori_loop(0, nkb, body, jnp.zeros((rg, LANES), F32))
    return jnp.broadcast_to(jnp.sum(acc, axis=1, keepdims=True), (rg, LANES))


def _trunc_bf16(x):
    bits = lax.bitcast_convert_type(x, I32) & (-65536)
    return lax.bitcast_convert_type(bits, F32).astype(BF16)


def _select_prefix16(xb_ref, nkb, rows, tk, topk):
    reps = tk // LANES
    one, zero = jnp.ones((), BF16), jnp.zeros((), BF16)

    def bit_body(i, u):
        cand = u | jnp.left_shift(jnp.int32(1), 15 - i)
        tb = _prefix_to_bf16(cand)

        def body(kb, acc):
            x = xb_ref[kb]
            for j in range(reps):
                acc = acc + jnp.where(x[:, j * LANES:(j + 1) * LANES] >= tb, one, zero)
            return acc

        acc = lax.fori_loop(0, nkb, body, jnp.zeros((rows, LANES), BF16))
        c = jnp.sum(acc.astype(F32), axis=1, keepdims=True)
        return jnp.where(c >= topk, cand, u)

    return lax.fori_loop(0, 16, bit_body, jnp.zeros((rows, LANES), I32))


def _prefix_to_bf16(u):
    hi = u - 32768
    fbits = jnp.left_shift(jnp.where(hi >= 0, hi, hi ^ 0x7FFF), 16)
    return lax.bitcast_convert_type(fbits, F32).astype(BF16)


def _low_key16(x):
    bits = lax.bitcast_convert_type(x, I32)
    key = bits ^ ((bits >> 31) & INT_MAX)
    return ((key & 0xFFFF) - 32768).astype(I16)


def _select_low16(xb_ref, lo_ref, u16, nkb, rows, tk, topk):
    reps = tk // LANES
    tb = _prefix_to_bf16(u16)
    tbt = _tile_lanes(tb, reps)
    one, zero = jnp.ones((), BF16), jnp.zeros((), BF16)
    min16 = jnp.full((), -32768, I16)

    def prep_body(kb, acc):
        x = xb_ref[kb]
        lo_ref[kb] = jnp.where(x == tbt, lo_ref[kb], min16)
        for j in range(reps):
            acc = acc + jnp.where(x[:, j * LANES:(j + 1) * LANES] > tb, one, zero)
        return acc

    above = lax.fori_loop(0, nkb, prep_body, jnp.zeros((rows, LANES), BF16))
    need = topk - jnp.sum(above.astype(F32), axis=1, keepdims=True)
    i_one, i_zero = jnp.ones((), I16), jnp.zeros((), I16)

    def bit_body(i, v):
        cand = v | jnp.left_shift(jnp.int32(1), 15 - i)
        c16 = (cand - 32768).astype(I16)

        def body(kb, acc):
            lo = lo_ref[kb]
            for j in range(reps):
                acc = acc + jnp.where(lo[:, j * LANES:(j + 1) * LANES] >= c16, i_one, i_zero)
            return acc

        acc = lax.fori_loop(0, nkb, body, jnp.zeros((rows, LANES), I16))
        c = jnp.sum(acc.astype(I32).astype(F32), axis=1, keepdims=True)
        return jnp.where(c >= need, cand, v)

    return lax.fori_loop(0, 16, bit_body, jnp.zeros((rows, LANES), I32))


def _select_rows(cnt, rg, topk, idx_bits, row_ok, key0=None, low_bits=32):
    def bit_body(i, key):
        cand = key | jnp.left_shift(jnp.int32(1), low_bits - 1 - i)
        t = _key_to_float(cand)
        c = cnt(lambda x, kp: x >= t)
        return jnp.where(c >= topk, cand, key)

    key = lax.fori_loop(0, low_bits, bit_body, jnp.zeros((rg, LANES), I32) if key0 is None else key0)
    c_fin = cnt(lambda x, kp: x >= -F32_MAX)
    thr = jnp.where(c_fin >= topk, _key_to_float(key), -F32_MAX)
    c_gt = cnt(lambda x, kp: x > thr)
    c_ge = cnt(lambda x, kp: x >= thr)
    need = topk - c_gt
    excess = jnp.max(jnp.where((c_ge > topk) & row_ok, 1.0, 0.0)) > 0.5

    def tie_search():
        def jbody(i, p):
            cand = p | jnp.left_shift(jnp.int32(1), idx_bits - 1 - i)
            c = cnt(lambda x, kp: (x == thr) & (kp < cand))
            return jnp.where(c < need, cand, p)
        return lax.fori_loop(0, idx_bits, jbody, jnp.zeros((rg, LANES), I32))

    jmax = lax.cond(excess, tie_search, lambda: jnp.full((rg, LANES), INT_MAX, I32))
    return thr, jmax


def _select(sc_ref, xb_ref, lo_ref, t_ref, j_ref, nkb, rows, rg, tk, topk, idx_bits):
    u16 = _select_prefix16(xb_ref, nkb, rows, tk, topk)
    j_ref[...] = jnp.left_shift(u16, 16) | _select_low16(xb_ref, lo_ref, u16, nkb, rows, tk, topk)
    for r0 in range(0, rows, rg):
        cnt = functools.partial(_count, sc_ref, nkb, r0, rg, tk)
        thr, jmax = _select_rows(cnt, rg, topk, idx_bits, True, j_ref[r0:r0 + rg, :], 0)
        t_ref[r0:r0 + rg, :] = thr
        j_ref[r0:r0 + rg, :] = jmax


def _sel_mask(scb, kpos, thr, jmax):
    return (scb > thr) | ((scb == thr) & (kpos <= jmax))


def _prompt_attn_kernel(q_ref, qi_ref, kiwi_ref, kt_ref, v_ref, kit_ref, o_ref,
                        sc_ref, xb_ref, lo_ref, wib_ref, m_ref, l_ref, acc_ref, t_ref, j_ref,
                        *, tq, topk, idx_bits):
    tk = tq
    reps = tk // LANES
    i = pl.program_id(1)
    nkb = i + 1

    kiwi = kiwi_ref[...]
    for h in range(IDX_HEADS):
        wib_ref[h] = jnp.broadcast_to(kiwi[:, IDX_DIM + h:IDX_DIM + h + 1], (tq, LANES))

    def score_body(kb, carry):
        kit = kit_ref[kb]
        acc = None
        for h in range(IDX_HEADS):
            s = jnp.dot(qi_ref[:, h * LANES:(h + 1) * LANES], kit, preferred_element_type=F32)
            t = jnp.maximum(s, 0.0) * _tile_lanes(wib_ref[h], reps)
            acc = t if acc is None else acc + t
        acc = jnp.where(acc == 0.0, 0.0, acc)
        sc_ref[kb] = acc
        xb_ref[kb] = _trunc_bf16(acc)
        lo_ref[kb] = _low_key16(acc)
        return carry

    lax.fori_loop(0, nkb, score_body, 0)
    row = lax.broadcasted_iota(I32, (tq, tk), 0)
    col = lax.broadcasted_iota(I32, (tq, tk), 1)
    diag = jnp.where(col <= row, sc_ref[i], -jnp.inf)
    sc_ref[i] = diag
    xb_ref[i] = _trunc_bf16(diag)
    lo_ref[i] = _low_key16(diag)

    _select(sc_ref, xb_ref, lo_ref, t_ref, j_ref, nkb, tq, min(128, tq), tk, topk, idx_bits)

    def bias_body(kb, carry):
        thr = _tile_lanes(t_ref[...], reps)
        jmax = _tile_lanes(j_ref[...], reps)
        sc_ref[kb] = jnp.where(_sel_mask(sc_ref[kb], kb * tk + col, thr, jmax), 0.0, NEG_BIG)
        return carry

    lax.fori_loop(0, nkb, bias_body, 0)

    m_ref[...] = jnp.full(m_ref.shape, NEG_BIG, F32)
    l_ref[...] = jnp.zeros(l_ref.shape, F32)
    acc_ref[...] = jnp.zeros(acc_ref.shape, F32)

    def attn_body(kb, carry):
        kt = kt_ref[kb]
        vb = v_ref[pl.ds(pl.multiple_of(kb * tk, tk), tk), :]
        bias = sc_ref[kb]
        for h in range(N_HEADS):
            s = jnp.dot(q_ref[h], kt, preferred_element_type=F32) + bias
            m_prev = m_ref[h]
            m_new = jnp.maximum(m_prev, jnp.max(s, axis=1, keepdims=True))
            alpha = jnp.exp2(m_prev - m_new)
            p = jnp.exp2(s - _tile_lanes(m_new, reps))
            l_ref[h] = alpha * l_ref[h] + sum(p[:, j * LANES:(j + 1) * LANES] for j in range(reps))
            acc_ref[h] = alpha * acc_ref[h] + jnp.dot(p.astype(BF16), vb, preferred_element_type=F32)
            m_ref[h] = m_new
        return carry

    lax.fori_loop(0, nkb, attn_body, 0)

    for h in range(N_HEADS):
        g = h // HEADS_PER_KV
        o = acc_ref[h] / jnp.sum(l_ref[h], axis=1, keepdims=True)
        o_ref[:, h * HEAD_DIM:(h + 1) * HEAD_DIM] = o[:, g * HEAD_DIM:(g + 1) * HEAD_DIM]


def _prompt_attn(q, qi, kiwi, kt, v, kit, tq, topk):
    b, s, _ = qi.shape
    nb = s // tq
    idx_bits = int(s).bit_length()
    blk = lambda w: pl.BlockSpec((None, tq, w), lambda bi, i: (bi, i, 0))
    full = lambda shp: pl.BlockSpec((None,) + shp, lambda bi, i: (bi,) + (0,) * len(shp),
                                    pipeline_mode=pl.Buffered(1))
    return pl.pallas_call(
        functools.partial(_prompt_attn_kernel, tq=tq, topk=topk, idx_bits=idx_bits),
        grid=(b, nb),
        in_specs=[pl.BlockSpec((N_HEADS, tq, LANES), lambda bi, i: (0, bi * nb + i, 0)),
                  blk(IDX_HEADS * LANES), blk(LANES),
                  full((nb, KV_WIDTH, tq)), full((s, KV_WIDTH)), full((nb, 2 * IDX_DIM, tq))],
        out_specs=blk(ATTN_WIDTH),
        out_shape=jax.ShapeDtypeStruct((b, s, ATTN_WIDTH), F32),
        scratch_shapes=[pltpu.VMEM((nb, tq, tq), F32),
                        pltpu.VMEM((nb, tq, tq), BF16),
                        pltpu.VMEM((nb, tq, tq), I16),
                        pltpu.VMEM((IDX_HEADS, tq, LANES), F32),
                        pltpu.VMEM((N_HEADS, tq, LANES), F32),
                        pltpu.VMEM((N_HEADS, tq, LANES), F32),
                        pltpu.VMEM((N_HEADS, tq, KV_WIDTH), F32),
                        pltpu.VMEM((tq, LANES), F32),
                        pltpu.VMEM((tq, LANES), I32)],
        compiler_params=_cparams(("arbitrary", "arbitrary")),
        name="prompt_attn",
    )(q, qi, kiwi, kt, v, kit)


def _sample_attn_kernel(pt_ref, q_ref, qih_ref, wib_ref, kin_ref, kn_ref, vn_ref, *rest,
                        n_steps, pps, n_tok, tpad, page, topk, idx_bits):
    cik_refs, ck_refs, cv_refs = rest[:pps], rest[pps:2 * pps], rest[2 * pps:3 * pps]
    o_ref, sc_ref, sa_ref, vs_ref = rest[3 * pps:]
    step = pl.program_id(1)
    n_pages = n_steps * pps
    nkb = n_pages + 1
    qih = qih_ref[...]
    wib = wib_ref[...]
    q = q_ref[...]

    def idx_score(s):
        acc = None
        for h in range(IDX_HEADS):
            t = jnp.maximum(s[h * tpad:(h + 1) * tpad], 0.0) * wib[h * tpad:(h + 1) * tpad]
            acc = t if acc is None else acc + t
        return acc

    for j in range(pps):
        pg = step * pps + j
        sc_ref[pg] = idx_score(jnp.dot(qih, cik_refs[j][...].astype(BF16), preferred_element_type=F32))
        sa_ref[pg] = jnp.dot(q, ck_refs[j][...].astype(BF16), preferred_element_type=F32)
        vs_ref[pg] = cv_refs[j][...].astype(BF16)

    @pl.when(step == n_steps - 1)
    def _():
        s_new = idx_score(jnp.dot(qih, kin_ref[...], preferred_element_type=F32))
        tok = lax.broadcasted_iota(I32, (tpad, page), 0)
        col = lax.broadcasted_iota(I32, (tpad, page), 1)
        sc_ref[n_pages] = jnp.where((col <= tok) & (col < n_tok), s_new, -jnp.inf)
        sa_ref[n_pages] = jnp.dot(q, kn_ref[...], preferred_element_type=F32)
        vs_ref[n_pages] = vn_ref[...]

        kpos = (lax.broadcasted_iota(I32, (nkb, tpad, page), 0) * page
                + lax.broadcasted_iota(I32, (nkb, tpad, page), 2))

        def cnt(pred):
            c = jnp.sum(jnp.where(pred(sc_ref[...], kpos), 1.0, 0.0), axis=0)
            return jnp.broadcast_to(jnp.sum(c, axis=1, keepdims=True), c.shape)

        thr, jmax = _select_rows(cnt, tpad, topk, idx_bits, tok[:, :LANES] < n_tok)
        sc_ref[...] = jnp.where(_sel_mask(sc_ref[...], kpos, thr[None], jmax[None]), 0.0, NEG_BIG)

        heads = N_HEADS
        inv_l = []
        for t in range(n_tok):
            s = sa_ref[:, t * heads:(t + 1) * heads, :] + sc_ref[:, t:t + 1, :]
            m = jnp.max(jnp.max(s, axis=0), axis=1, keepdims=True)
            pr = jnp.exp2(s - m[None])
            sa_ref[:, t * heads:(t + 1) * heads, :] = pr
            l = jnp.sum(jnp.sum(pr, axis=0), axis=1, keepdims=True)
            inv_l.append(jnp.broadcast_to(1.0 / l, (heads, KV_WIDTH)))

        def pv_body(kb, acc):
            return acc + lax.dot_general(sa_ref[kb].astype(BF16), vs_ref[kb], NT_DIMS, preferred_element_type=F32)

        acc = lax.fori_loop(0, nkb, pv_body, jnp.zeros((n_tok * heads, KV_WIDTH), F32), unroll=5)
        o_ref[...] = acc * jnp.concatenate(inv_l, axis=0)


def _sample_attn(page_table, q, qih, wib, kin_t, kn_t, vn_t, cik_t, ck_t, cv_t, n_tok, tpad, topk, pps):
    n, n_pages = page_table.shape
    page = ck_t.shape[2]
    assert page == LANES and KV_WIDTH == LANES and n_pages % pps == 0
    n_steps = n_pages // pps
    idx_bits = int((n_pages + 1) * page).bit_length()
    rows = n_tok * N_HEADS
    seq = lambda r, w: pl.BlockSpec((None, r, w), lambda b, p, pt: (b, 0, 0))

    def paged(r, j):
        return pl.BlockSpec((None, r, page), lambda b, p, pt: (pt[b, p * pps + j], 0, 0))

    grid_spec = pltpu.PrefetchScalarGridSpec(
        num_scalar_prefetch=1,
        grid=(n, n_steps),
        in_specs=([seq(rows, KV_WIDTH), seq(IDX_HEADS * tpad, IDX_DIM), seq(IDX_HEADS * tpad, LANES),
                   seq(IDX_DIM, page), seq(KV_WIDTH, page), seq(KV_WIDTH, page)]
                  + [paged(IDX_DIM, j) for j in range(pps)]
                  + [paged(KV_WIDTH, j) for j in range(pps)]
                  + [paged(KV_WIDTH, j) for j in range(pps)]),
        out_specs=seq(rows, KV_WIDTH),
        scratch_shapes=[pltpu.VMEM((n_pages + 1, tpad, page), F32),
                        pltpu.VMEM((n_pages + 1, rows, page), F32),
                        pltpu.VMEM((n_pages + 1, KV_WIDTH, page), BF16)],
    )
    return pl.pallas_call(
        functools.partial(_sample_attn_kernel, n_steps=n_steps, pps=pps, n_tok=n_tok, tpad=tpad, page=page,
                          topk=topk, idx_bits=idx_bits),
        grid_spec=grid_spec,
        out_shape=jax.ShapeDtypeStruct((n, rows, KV_WIDTH), F32),
        compiler_params=_cparams(("arbitrary", "arbitrary")),
        name="sample_attn",
    )(page_table, q, qih, wib, kin_t, kn_t, vn_t, *([cik_t] * pps), *([ck_t] * pps), *([cv_t] * pps))


def _mix_out_tail(x, o, gb, u0, um1, um2, wc_ref, wo_ref, gt, gpost, aw):
    y = wc_ref[0:1, :] * um2 + wc_ref[1:2, :] * um1 + wc_ref[2:3, :] * u0
    mixed = (jnp.dot(o.astype(BF16), wo_ref[0:aw, :], preferred_element_type=F32)
             + jnp.dot((gb * y).astype(BF16), wo_ref[aw:, :], preferred_element_type=F32))
    return x + gt * (_rms(mixed) * gpost)


def _mix_out_prompt_kernel(x_ref, o_ref_in, gb_ref, u_ref, halo_ref, gt_ref, gpost_ref, wc_ref, wo_ref,
                           out_ref, ext_ref, *, tm, tiles_per_seq):
    i = pl.program_id(0)
    halo = halo_ref[...]
    ext_ref[0:SUBLANES, :] = jnp.where(i % tiles_per_seq == 0, jnp.zeros_like(halo), halo)
    ext_ref[SUBLANES:, :] = u_ref[...]
    um2 = ext_ref[SUBLANES - 2:SUBLANES - 2 + tm, :]
    um1 = ext_ref[SUBLANES - 1:SUBLANES - 1 + tm, :]
    out_ref[...] = _mix_out_tail(x_ref[...], o_ref_in[...], gb_ref[...], u_ref[...], um1, um2,
                                 wc_ref, wo_ref, gt_ref[0], gpost_ref[...], ATTN_WIDTH)


def _mix_out_prompt(x, o, gb, u, gate, g_post, w_conv, w_o, tm, tiles_per_seq):
    rows, d = x.shape
    cw = u.shape[1]
    const = lambda i: (0, 0)
    row = lambda i: (i, 0)
    hb = tm // SUBLANES
    return pl.pallas_call(
        functools.partial(_mix_out_prompt_kernel, tm=tm, tiles_per_seq=tiles_per_seq),
        grid=(rows // tm,),
        in_specs=[pl.BlockSpec((tm, d), row),
                  pl.BlockSpec((tm, ATTN_WIDTH), row),
                  pl.BlockSpec((tm, cw), row),
                  pl.BlockSpec((tm, cw), row),
                  pl.BlockSpec((SUBLANES, cw), lambda i: (jnp.maximum(i * hb - 1, 0), 0)),
                  _mod_spec(gate, tm, tiles_per_seq),
                  pl.BlockSpec((1, d), const),
                  pl.BlockSpec(w_conv.shape, const),
                  pl.BlockSpec(w_o.shape, const)],
        out_specs=pl.BlockSpec((tm, d), row),
        out_shape=jax.ShapeDtypeStruct((rows, d), F32),
        scratch_shapes=[pltpu.VMEM((tm + SUBLANES, cw), F32)],
        compiler_params=_cparams(("arbitrary",)),
        name="mix_out_prompt",
    )(x, o, gb, u, u, gate, g_post, w_conv, w_o)


def _mix_out_sample_kernel(x_ref, o_ref_in, gb_ref, um2_ref, um1_ref, u0_ref, gt_ref, gpost_ref, wc_ref, wo_ref,
                           out_ref):
    out_ref[...] = _mix_out_tail(x_ref[...], o_ref_in[...], gb_ref[...], u0_ref[...], um1_ref[...], um2_ref[...],
                                 wc_ref, wo_ref, gt_ref[0], gpost_ref[...], ATTN_WIDTH)


def _mix_out_sample(x, o, gb, ext, gate, g_post, w_conv, w_o, tm):
    rows, d = x.shape
    cw = gb.shape[1]
    const = lambda i: (0, 0)
    row = lambda i: (i, 0)
    return pl.pallas_call(
        _mix_out_sample_kernel,
        grid=(rows // tm,),
        in_specs=[pl.BlockSpec((tm, d), row),
                  pl.BlockSpec((tm, ATTN_WIDTH), row),
                  pl.BlockSpec((tm, cw), row),
                  pl.BlockSpec((tm, cw), lambda i: (i, 0)),
                  pl.BlockSpec((tm, cw), lambda i: (i + 1, 0)),
                  pl.BlockSpec((tm, cw), lambda i: (i + 2, 0)),
                  pl.BlockSpec((1, tm, d), lambda i: (0, 0, 0)),
                  pl.BlockSpec((1, d), const),
                  pl.BlockSpec(w_conv.shape, const),
                  pl.BlockSpec(w_o.shape, const)],
        out_specs=pl.BlockSpec((tm, d), row),
        out_shape=jax.ShapeDtypeStruct((rows, d), F32),
        compiler_params=_cparams(("arbitrary",)),
        name="mix_out_sample",
    )(x, o, gb, ext, ext, ext, gate, g_post, w_conv, w_o)


def _prompt_layer(x, mod, lw, tm, tq):
    b, s, d = x.shape
    conv_ch = lw["w_conv"].shape[1]
    tps = s // tm
    md = lambda j, k: mod[:, j, k][:, None, :]
    xr = x.reshape(b * s, d)
    x1 = _ffn(xr, md(0, 0), md(0, 1), md(0, 2), lw["g_pre"][0:1], lw["g_post"][0:1], lw["ffn1_w13"], lw["ffn1_w2"],
              tm, tps)
    tabs = _rope_tables(jnp.arange(s))
    q, k, v, qi, kiwi, gb, u = _mix_in(x1, md(1, 0), md(1, 1), lw["g_pre"][1:2], lw["w_in_p"], tabs, tm, tps, tps,
                                       conv_ch)
    nb = s // tq
    kt = jnp.swapaxes(k.astype(BF16).reshape(b, nb, tq, KV_WIDTH), 2, 3)
    kit = jnp.swapaxes(kiwi[:, :IDX_DIM].astype(BF16).reshape(b, nb, tq, IDX_DIM), 2, 3)
    kit = jnp.concatenate([kit, kit], axis=2)
    topk = min(TOPK_MAX, s // 4)
    o = _prompt_attn(q, qi.reshape(b, s, -1), kiwi.reshape(b, s, -1), kt,
                     v.astype(BF16).reshape(b, s, KV_WIDTH), kit, tq, topk)
    x2 = _mix_out_prompt(x1, o.reshape(b * s, -1), gb, u, md(1, 2), lw["g_post"][1:2], lw["w_conv"], lw["w_o"],
                         tm, tps)
    x3 = _ffn(x2, md(2, 0), md(2, 1), md(2, 2), lw["g_pre"][2:3], lw["g_post"][2:3], lw["ffn2_w13"], lw["ffn2_w2"],
              tm, tps)
    state = (k.reshape(b, s, N_KV_HEADS, HEAD_DIM), v.reshape(b, s, N_KV_HEADS, HEAD_DIM),
             kiwi[:, :IDX_DIM].reshape(b, s, IDX_DIM), u.reshape(b, s, conv_ch)[:, s - (CONV_K - 1):])
    return x3.reshape(b, s, d), state


def _sample_layer(x_tm, mod, lw, cache_k, cache_v, cache_idx_k, state_conv, page_table):
    n = mod.shape[0]
    t_tok = x_tm.shape[0] // n
    d = x_tm.shape[1]
    conv_ch = lw["w_conv"].shape[1]
    n_pages = page_table.shape[1]
    page = cache_k.shape[1]
    past = n_pages * page
    md = lambda j, k: mod[:, j, k][None]
    x1 = _ffn(x_tm, md(0, 0), md(0, 1), md(0, 2), lw["g_pre"][0:1], lw["g_post"][0:1], lw["ffn1_w13"],
              lw["ffn1_w2"], n, t_tok)
    pos = jnp.repeat(past + jnp.arange(t_tok), n)
    tabs = _rope_tables(pos)
    q, k, v, qi, kiwi, gb, u = _mix_in(x1, md(1, 0), md(1, 1), lw["g_pre"][1:2], lw["w_in_p"], tabs, n, t_tok, t_tok,
                                       conv_ch)
    tpad = SUBLANES
    assert t_tok <= tpad

    def seq_major(a, heads, width):
        a = a.reshape(t_tok, n, heads, width).transpose(1, 2, 0, 3)
        a = jnp.pad(a, ((0, 0), (0, 0), (0, tpad - t_tok), (0, 0)))
        return a.reshape(n, heads * tpad, width)

    q_seq = q.reshape(N_HEADS, t_tok, n, LANES).transpose(2, 1, 0, 3).reshape(n, t_tok * N_HEADS, LANES)
    qi_nat = jnp.concatenate([qi[:, h * LANES + (h % 2) * IDX_DIM:h * LANES + (h % 2 + 1) * IDX_DIM]
                              for h in range(IDX_HEADS)], axis=1)
    qih = seq_major(qi_nat, IDX_HEADS, IDX_DIM)
    wib = jnp.broadcast_to(seq_major(kiwi[:, IDX_DIM:IDX_DIM + IDX_HEADS], IDX_HEADS, 1),
                           (n, IDX_HEADS * tpad, LANES))

    def new_keys_t(a):
        a = a.reshape(t_tok, n, -1).transpose(1, 2, 0).astype(BF16)
        return jnp.pad(a, ((0, 0), (0, 0), (0, page - t_tok)))

    ck_t = cache_k.transpose(0, 2, 3, 1).reshape(-1, KV_WIDTH, page)
    cv_t = cache_v.transpose(0, 2, 3, 1).reshape(-1, KV_WIDTH, page)
    cik_t = cache_idx_k.transpose(0, 2, 1)
    topk = min(TOPK_MAX, (past + t_tok) // 4)
    pps = max(p for p in (16, 8, 4, 2, 1) if n_pages % p == 0)
    o = _sample_attn(page_table, q_seq, qih, wib, new_keys_t(kiwi[:, :IDX_DIM]), new_keys_t(k), new_keys_t(v),
                     cik_t, ck_t, cv_t, t_tok, tpad, topk, pps)
    o = o.reshape(n, t_tok, N_KV_HEADS, HEADS_PER_KV, N_KV_HEADS, HEAD_DIM)
    o = jnp.stack([o[:, :, g, :, g] for g in range(N_KV_HEADS)], axis=2)
    o_tm = o.transpose(1, 0, 2, 3, 4).reshape(t_tok * n, ATTN_WIDTH)
    ext = jnp.concatenate([state_conv.transpose(1, 0, 2).reshape(-1, conv_ch), u], axis=0)
    x2 = _mix_out_sample(x1, o_tm, gb, ext, md(1, 2), lw["g_post"][1:2], lw["w_conv"], lw["w_o"], n)
    x3 = _ffn(x2, md(2, 0), md(2, 1), md(2, 2), lw["g_pre"][2:3], lw["g_post"][2:3], lw["ffn2_w13"],
              lw["ffn2_w2"], n, t_tok)
    tm_to_seq = lambda a: a.reshape((t_tok, n) + a.shape[1:]).swapaxes(0, 1)
    state = (tm_to_seq(k).reshape(n, t_tok, N_KV_HEADS, HEAD_DIM), tm_to_seq(v).reshape(n, t_tok, N_KV_HEADS, HEAD_DIM),
             tm_to_seq(kiwi[:, :IDX_DIM]), tm_to_seq(u)[:, t_tok - (CONV_K - 1):])
    return x3, state


def _row_tile(s, target):
    t = min(target, s)
    while s % t:
        t //= 2
    return t


def kernel(x_prompt, x_sample, cache_k, cache_v, cache_idx_k, state_conv, page_table, c_prompt, c_sample, w_ada, b_ada, g_pre, g_post, ffn1_w13, ffn1_w2, w_in, w_conv, w_o, ffn2_w13, ffn2_w2):
    depth = w_ada.shape[0]
    b, s, d = x_prompt.shape
    n, t_tok, _ = x_sample.shape
    conv_ch = w_conv.shape[2]
    tm = _row_tile(s, 512)
    tq = _row_tile(s, 256)

    yp = x_prompt
    ys = x_sample.swapaxes(0, 1).reshape(t_tok * n, d)
    c_all = jnp.concatenate([c_prompt, c_sample], axis=0)
    n_c = c_all.shape[0]
    c_all = jnp.pad(c_all, ((0, -n_c % SUBLANES), (0, 0)))
    outs = [[] for _ in range(8)]
    for l in range(depth):
        lw = dict(g_pre=g_pre[l], g_post=g_post[l], w_conv=w_conv[l],
                  ffn1_w13=ffn1_w13[l].astype(BF16), ffn1_w2=ffn1_w2[l].astype(BF16),
                  ffn2_w13=ffn2_w13[l].astype(BF16), ffn2_w2=ffn2_w2[l].astype(BF16),
                  w_in_p=_pack_w_in(w_in[l], conv_ch), w_o=w_o[l].astype(BF16))
        mod = _ada_mod(c_all, w_ada[l], b_ada[l])[:n_c].reshape(n_c, N_SUB, 3, d)
        yp, st_p = _prompt_layer(yp, mod[:b], lw, tm, tq)
        ys, st_s = _sample_layer(ys, mod[b:], lw, cache_k[l], cache_v[l], cache_idx_k[l], state_conv[l], page_table)
        for lst, a in zip(outs, st_p + st_s):
            lst.append(a)
    ys = ys.reshape(t_tok, n, d).swapaxes(0, 1)
    return (yp, ys) + tuple(jnp.stack(lst) for lst in outs)
```

```python
import functools

import numpy as np
import jax
import jax.numpy as jnp
from jax import lax
from jax.experimental import pallas as pl
from jax.experimental.pallas import tpu as pltpu

F32 = jnp.float32
BF16 = jnp.bfloat16
I32 = jnp.int32

N_HEADS = 8
HEAD_DIM = 64
N_KV_HEADS = 2
HEADS_PER_KV = N_HEADS // N_KV_HEADS
ATTN_WIDTH = N_HEADS * HEAD_DIM
KV_WIDTH = N_KV_HEADS * HEAD_DIM
ROT_DIV = 4
ROT = HEAD_DIM // ROT_DIV
ROT_HALF = ROT // 2
ROPE_THETA = 500000.0
IDX_HEADS = 4
IDX_DIM = 64
IDX_SCALE = (IDX_HEADS * IDX_DIM) ** -0.5
TOPK_MAX = 256
CONV_K = 3
N_SUB = 3
RMS_EPS = 1e-6
Q_SCALE = HEAD_DIM ** -0.5 * float(np.log2(np.e))

LANES = 128
SUBLANES = 8
VMEM_LIMIT = 56 * 1024 * 1024

NEG_BIG = -1e30
F32_MAX = float(np.finfo(np.float32).max)
INT_MIN = -(2 ** 31)
INT_MAX = 2 ** 31 - 1

NT_DIMS = (((1,), (1,)), ((), ()))


def _cparams(sem):
    return pltpu.CompilerParams(dimension_semantics=sem, vmem_limit_bytes=VMEM_LIMIT)


def _silu(x):
    return x * (1.0 / (1.0 + jnp.exp(-x)))


def _rms(x):
    return x * lax.rsqrt(jnp.mean(x * x, axis=-1, keepdims=True) + RMS_EPS)


def _tile_lanes(x, reps):
    return x if reps == 1 else jnp.concatenate([x] * reps, axis=1)


def _ada_kernel(c_ref, w_ref, b_ref, o_ref):
    a = _silu(c_ref[...]).astype(BF16)
    o_ref[...] = jnp.dot(a, w_ref[...].astype(BF16), preferred_element_type=F32) + b_ref[...]


def _ada_mod(c, w_ada, b_ada):
    n, d = c.shape
    nout = w_ada.shape[1]
    tn = 9 * LANES if nout % (9 * LANES) == 0 else LANES
    return pl.pallas_call(
        _ada_kernel,
        grid=(nout // tn,),
        in_specs=[pl.BlockSpec((n, d), lambda j: (0, 0)),
                  pl.BlockSpec((d, tn), lambda j: (0, j)),
                  pl.BlockSpec((1, tn), lambda j: (0, j))],
        out_specs=pl.BlockSpec((n, tn), lambda j: (0, j)),
        out_shape=jax.ShapeDtypeStruct((n, nout), F32),
        compiler_params=_cparams(("arbitrary",)),
        name="ada_mod",
    )(c, w_ada, b_ada.reshape(1, nout))


def _ffn_kernel(x_ref, sh_ref, sc_ref, gt_ref, gpre_ref, gpost_ref, w13_ref, w2_ref, o_ref, *, d_ff, chunk):
    x = x_ref[...]
    h = _rms(x) * gpre_ref[...] * (1.0 + sc_ref[0]) + sh_ref[0]
    hb = h.astype(BF16)
    y = jnp.zeros(x.shape, F32)
    for c in range(d_ff // chunk):
        a = jnp.dot(hb, w13_ref[:, c * chunk:(c + 1) * chunk], preferred_element_type=F32)
        b = jnp.dot(hb, w13_ref[:, d_ff + c * chunk:d_ff + (c + 1) * chunk], preferred_element_type=F32)
        g = (_silu(a) * b).astype(BF16)
        y = y + jnp.dot(g, w2_ref[c * chunk:(c + 1) * chunk, :], preferred_element_type=F32)
    o_ref[...] = x + 0.5 * gt_ref[0] * (_rms(y) * gpost_ref[...])


def _mod_spec(mod, tm, tiles_per_group):
    _, r, d = mod.shape
    return pl.BlockSpec((1, r, d), lambda i: (i // tiles_per_group, 0, 0))


def _ffn(x, shift, scale, gate, g_pre, g_post, w13, w2, tm, tiles_per_group):
    rows, d = x.shape
    d_ff = w2.shape[0]
    const = lambda i: (0, 0)
    return pl.pallas_call(
        functools.partial(_ffn_kernel, d_ff=d_ff, chunk=min(512, d_ff)),
        grid=(rows // tm,),
        in_specs=[pl.BlockSpec((tm, d), lambda i: (i, 0)),
                  _mod_spec(shift, tm, tiles_per_group),
                  _mod_spec(scale, tm, tiles_per_group),
                  _mod_spec(gate, tm, tiles_per_group),
                  pl.BlockSpec((1, d), const),
                  pl.BlockSpec((1, d), const),
                  pl.BlockSpec(w13.shape, const),
                  pl.BlockSpec(w2.shape, const)],
        out_specs=pl.BlockSpec((tm, d), lambda i: (i, 0)),
        out_shape=jax.ShapeDtypeStruct((rows, d), F32),
        compiler_params=_cparams(("arbitrary",)),
        name="ffn",
    )(x, shift, scale, gate, g_pre, g_post, w13, w2)


_OFF_Q = 0
_OFF_K = _OFF_Q + ATTN_WIDTH
_OFF_V = _OFF_K + KV_WIDTH
_OFF_QI = _OFF_V + KV_WIDTH
_OFF_KIWI = _OFF_QI + IDX_HEADS * IDX_DIM
_OFF_GB = _OFF_KIWI + LANES


def _pack_w_in(w_in, conv_ch):
    splits = [ATTN_WIDTH, KV_WIDTH, KV_WIDTH, IDX_HEADS * IDX_DIM, IDX_DIM, IDX_HEADS, conv_ch, conv_ch, conv_ch]
    offs = np.cumsum([0] + splits)
    d = w_in.shape[0]
    pad = jnp.zeros((d, LANES - IDX_DIM - IDX_HEADS), w_in.dtype)
    return jnp.concatenate([w_in[:, :offs[6]], pad, w_in[:, offs[6]:]], axis=1).astype(BF16)


def _rope_tables(pos):
    inv = jnp.power(ROPE_THETA, -jnp.arange(ROT_HALF, dtype=F32) * 2.0 / ROT)
    ang = pos.astype(F32)[:, None] * inv[None, :]
    cos, sin = jnp.cos(ang), jnp.sin(ang)
    n = pos.shape[0]
    one = jnp.ones((n, HEAD_DIM - ROT), F32)
    zero_r = jnp.zeros((n, HEAD_DIM - ROT), F32)
    zero_h = jnp.zeros((n, ROT_HALF), F32)
    c = jnp.concatenate([cos, cos, one], axis=1)
    s1 = jnp.concatenate([-sin, zero_h, zero_r], axis=1)
    s2 = jnp.concatenate([zero_h, sin, zero_r], axis=1)
    rep = LANES // HEAD_DIM
    return jnp.tile(c, (1, rep)), jnp.tile(s1, (1, rep)), jnp.tile(s2, (1, rep))


def _rope128(x, c, s1, s2):
    return x * c + pltpu.roll(x, LANES - ROT_HALF, 1) * s1 + pltpu.roll(x, ROT_HALF, 1) * s2


def _mix_in_kernel(x_ref, sh_ref, sc_ref, gpre_ref, w_ref, rc_ref, rs1_ref, rs2_ref,
                   q_ref, k_ref, v_ref, qi_ref, kiwi_ref, gb_ref, u_ref, *, conv_ch):
    x = x_ref[...]
    hb = (_rms(x) * gpre_ref[...] * (1.0 + sc_ref[0]) + sh_ref[0]).astype(BF16)
    c, s1, s2 = rc_ref[...], rs1_ref[...], rs2_ref[...]

    def proj(off, width):
        return jnp.dot(hb, w_ref[:, off:off + width], preferred_element_type=F32)

    def rope_cols(z):
        return [_rope128(z[:, g * LANES:(g + 1) * LANES], c, s1, s2) for g in range(z.shape[1] // LANES)]

    lane = lax.broadcasted_iota(I32, (x.shape[0], LANES), 1)
    low = lane < HEAD_DIM

    zq = proj(_OFF_Q, ATTN_WIDTH)
    for g, r in enumerate(rope_cols(zq)):
        r = r * Q_SCALE
        swapped = pltpu.roll(r, HEAD_DIM, 1)
        for e in range(2):
            h = 2 * g + e
            kv_group = h // HEADS_PER_KV
            src = r if e == kv_group else swapped
            keep = low if kv_group == 0 else jnp.logical_not(low)
            q_ref[h] = jnp.where(keep, src, 0.0).astype(BF16)
    zk = proj(_OFF_K, KV_WIDTH)
    for g, r in enumerate(rope_cols(zk)):
        k_ref[:, g * LANES:(g + 1) * LANES] = r
    v_ref[...] = proj(_OFF_V, KV_WIDTH)
    zqi = proj(_OFF_QI, IDX_HEADS * IDX_DIM)
    for g, r in enumerate(rope_cols(zqi)):
        qi_ref[:, (2 * g) * LANES:(2 * g + 1) * LANES] = jnp.where(low, r, 0.0).astype(BF16)
        qi_ref[:, (2 * g + 1) * LANES:(2 * g + 2) * LANES] = jnp.where(low, 0.0, r).astype(BF16)
    zkw = proj(_OFF_KIWI, LANES)
    kiwi_ref[...] = jnp.where(lane < IDX_DIM, _rope128(zkw, c, s1, s2), zkw * IDX_SCALE)
    gb_ref[...] = proj(_OFF_GB, conv_ch)
    u_ref[...] = proj(_OFF_GB + conv_ch, conv_ch) * proj(_OFF_GB + 2 * conv_ch, conv_ch)


def _mix_in(x, shift, scale, g_pre, w_in_p, rope_tabs, tm, tiles_per_group, rope_tiles, conv_ch):
    rows, d = x.shape
    const = lambda i: (0, 0)
    row = lambda i: (i, 0)
    rope_spec = pl.BlockSpec((tm, LANES), lambda i: (i % rope_tiles, 0))
    assert 2 * HEAD_DIM == LANES and KV_WIDTH == LANES and 2 * IDX_DIM == LANES
    widths = [(KV_WIDTH, F32), (KV_WIDTH, F32), (IDX_HEADS * LANES, BF16),
              (LANES, F32), (conv_ch, F32), (conv_ch, F32)]
    q_spec = pl.BlockSpec((N_HEADS, tm, LANES), lambda i: (0, i, 0))
    q_shape = jax.ShapeDtypeStruct((N_HEADS, rows, LANES), BF16)
    return pl.pallas_call(
        functools.partial(_mix_in_kernel, conv_ch=conv_ch),
        grid=(rows // tm,),
        in_specs=[pl.BlockSpec((tm, d), row),
                  _mod_spec(shift, tm, tiles_per_group),
                  _mod_spec(scale, tm, tiles_per_group),
                  pl.BlockSpec((1, d), const),
                  pl.BlockSpec(w_in_p.shape, const),
                  rope_spec, rope_spec, rope_spec],
        out_specs=[q_spec] + [pl.BlockSpec((tm, w), row) for w, _ in widths],
        out_shape=[q_shape] + [jax.ShapeDtypeStruct((rows, w), dt) for w, dt in widths],
        compiler_params=_cparams(("arbitrary",)),
        name="mix_in",
    )(x, shift, scale, g_pre, w_in_p, *rope_tabs)


def _key_to_float(u):
    o = u ^ INT_MIN
    bits = jnp.where(o >= 0, o, o ^ INT_MAX)
    return lax.bitcast_convert_type(bits, F32)


def _count(sc_ref, nkb, r0, rg, tk, pred):
    def body(kb, acc):
        for j in range(tk // LANES):
            x = sc_ref[kb, r0:r0 + rg, j * LANES:(j + 1) * LANES]
            kpos = kb * tk + j * LANES + lax.broadcasted_iota(I32, (rg, LANES), 1)
            acc = acc + jnp.where(pred(x, kpos), 1.0, 0.0)
        return acc
    acc = lax.fori_loop(0, nkb, body, jnp.zeros((rg, LANES), F32))
    return jnp.broadcast_to(jnp.sum(acc, axis=1, keepdims=True), (rg, LANES))


def _select_rows(cnt, rg, topk, idx_bits, row_ok, tie_fn=None):
    def bit_body(i, key):
        cand = key | jnp.left_shift(jnp.int32(1), 31 - i)
        t = _key_to_float(cand)
        c = cnt(lambda x, kp: x >= t)
        return jnp.where(c >= topk, cand, key)

    key = lax.fori_loop(0, 32, bit_body, jnp.zeros((rg, LANES), I32))
    c_fin = cnt(lambda x, kp: x >= -F32_MAX)
    thr = jnp.where(c_fin >= topk, _key_to_float(key), -F32_MAX)
    c_gt = cnt(lambda x, kp: x > thr)
    c_ge = cnt(lambda x, kp: x >= thr)
    need = topk - c_gt
    excess = jnp.max(jnp.where((c_ge > topk) & row_ok, 1.0, 0.0)) > 0.5

    def tie_search():
        if tie_fn is not None:
            return tie_fn(thr, need)

        def jbody(i, p):
            cand = p | jnp.left_shift(jnp.int32(1), idx_bits - 1 - i)
            c = cnt(lambda x, kp: (x == thr) & (kp < cand))
            return jnp.where(c < need, cand, p)
        return lax.fori_loop(0, idx_bits, jbody, jnp.zeros((rg, LANES), I32))

    jmax = lax.cond(excess, tie_search, lambda: jnp.full((rg, LANES), INT_MAX, I32))
    return thr, jmax


def _tie_bound_blocked(sc_ref, cnt_ref, nkb, r0, rg, tk, thr, need):
    reps = tk // LANES
    thr_t = _tile_lanes(thr, reps)

    group = 4

    def count_body(i, carry):
        for g in range(group):
            kb = jnp.minimum(group * i + g, nkb - 1)
            eq = jnp.where(sc_ref[kb, r0:r0 + rg, :] == thr_t, 1.0, 0.0)
            c = sum(eq[:, j * LANES:(j + 1) * LANES] for j in range(reps))
            cnt_ref[kb] = jnp.broadcast_to(jnp.sum(c, axis=1, keepdims=True), (rg, LANES))
        return carry

    lax.fori_loop(0, (nkb + group - 1) // group, count_body, 0)

    def locate_body(kb, carry):
        run, kstar, before = carry
        new_run = run + cnt_ref[kb]
        hit = (run < need) & (new_run >= need)
        return new_run, jnp.where(hit, kb, kstar), jnp.where(hit, run, before)

    zeros = jnp.zeros((rg, LANES), F32)
    _, kstar, before = lax.fori_loop(0, nkb, locate_body, (zeros, jnp.full((rg, LANES), -1, I32), zeros))
    kstar_t = _tile_lanes(kstar, reps)

    def gather_body(kb, own):
        eq = jnp.where(sc_ref[kb, r0:r0 + rg, :] == thr_t, 1.0, 0.0)
        return jnp.where(kstar_t == kb, eq, own)

    own = lax.fori_loop(0, nkb, gather_body, jnp.zeros((rg, tk), F32))
    need_in = need - before
    col = lax.broadcasted_iota(I32, (rg, tk), 1)
    p = jnp.zeros((rg, LANES), I32)
    for b in reversed(range(int(tk - 1).bit_length())):
        cand = p | (1 << b)
        c = jnp.sum(jnp.where(col < _tile_lanes(cand, reps), own, 0.0), axis=1, keepdims=True)
        p = jnp.where(c < need_in, cand, p)
    return jnp.where(kstar >= 0, kstar * tk + p, INT_MAX)


def _select(sc_ref, cnt_ref, t_ref, j_ref, nkb, rows, rg, tk, topk, idx_bits):
    for r0 in range(0, rows, rg):
        cnt = functools.partial(_count, sc_ref, nkb, r0, rg, tk)
        tie_fn = functools.partial(_tie_bound_blocked, sc_ref, cnt_ref, nkb, r0, rg, tk)
        thr, jmax = _select_rows(cnt, rg, topk, idx_bits, True, tie_fn)
        t_ref[r0:r0 + rg, :] = thr
        j_ref[r0:r0 + rg, :] = jmax


def _sel_mask(scb, kpos, thr, jmax):
    return (scb > thr) | ((scb == thr) & (kpos <= jmax))


def _prompt_attn_kernel(q_ref, qi_ref, kiwi_ref, kt_ref, v_ref, kit_ref, o_ref,
                        sc_ref, cnt_ref, wib_ref, m_ref, l_ref, acc_ref, t_ref, j_ref,
                        *, tq, topk, idx_bits):
    tk = tq
    reps = tk // LANES
    i = pl.program_id(1)
    nkb = i + 1

    kiwi = kiwi_ref[...]
    for h in range(IDX_HEADS):
        wib_ref[h] = jnp.broadcast_to(kiwi[:, IDX_DIM + h:IDX_DIM + h + 1], (tq, LANES))

    def score_body(kb, carry):
        kit = kit_ref[kb]
        acc = None
        for h in range(IDX_HEADS):
            s = jnp.dot(qi_ref[:, h * LANES:(h + 1) * LANES], kit, preferred_element_type=F32)
            t = jnp.maximum(s, 0.0) * _tile_lanes(wib_ref[h], reps)
            acc = t if acc is None else acc + t
        sc_ref[kb] = acc
        return carry

    lax.fori_loop(0, nkb, score_body, 0)
    row = lax.broadcasted_iota(I32, (tq, tk), 0)
    col = lax.broadcasted_iota(I32, (tq, tk), 1)
    sc_ref[i] = jnp.where(col <= row, sc_ref[i], -jnp.inf)

    _select(sc_ref, cnt_ref, t_ref, j_ref, nkb, tq, min(128, tq), tk, topk, idx_bits)

    def bias_body(kb, carry):
        thr = _tile_lanes(t_ref[...], reps)
        jmax = _tile_lanes(j_ref[...], reps)
        sc_ref[kb] = jnp.where(_sel_mask(sc_ref[kb], kb * tk + col, thr, jmax), 0.0, NEG_BIG)
        return carry

    lax.fori_loop(0, nkb, bias_body, 0)

    m_ref[...] = jnp.full(m_ref.shape, NEG_BIG, F32)
    l_ref[...] = jnp.zeros(l_ref.shape, F32)
    acc_ref[...] = jnp.zeros(acc_ref.shape, F32)

    def attn_body(kb, carry):
        kt = kt_ref[kb]
        vb = v_ref[pl.ds(pl.multiple_of(kb * tk, tk), tk), :]
        bias = sc_ref[kb]
        for h in range(N_HEADS):
            s = jnp.dot(q_ref[h], kt, preferred_element_type=F32) + bias
            m_prev = m_ref[h]
            m_new = jnp.maximum(m_prev, jnp.max(s, axis=1, keepdims=True))
            alpha = jnp.exp2(m_prev - m_new)
            p = jnp.exp2(s - _tile_lanes(m_new, reps))
            l_ref[h] = alpha * l_ref[h] + sum(p[:, j * LANES:(j + 1) * LANES] for j in range(reps))
            acc_ref[h] = alpha * acc_ref[h] + jnp.dot(p.astype(BF16), vb, preferred_element_type=F32)
            m_ref[h] = m_new
        return carry

    lax.fori_loop(0, nkb, attn_body, 0)

    for h in range(N_HEADS):
        g = h // HEADS_PER_KV
        o = acc_ref[h] / jnp.sum(l_ref[h], axis=1, keepdims=True)
        o_ref[:, h * HEAD_DIM:(h + 1) * HEAD_DIM] = o[:, g * HEAD_DIM:(g + 1) * HEAD_DIM]


def _prompt_attn(q, qi, kiwi, kt, v, kit, tq, topk):
    b, s, _ = qi.shape
    nb = s // tq
    idx_bits = int(s).bit_length()
    blk = lambda w: pl.BlockSpec((None, tq, w), lambda bi, i: (bi, i, 0))
    full = lambda shp: pl.BlockSpec((None,) + shp, lambda bi, i: (bi,) + (0,) * len(shp),
                                    pipeline_mode=pl.Buffered(1))
    return pl.pallas_call(
        functools.partial(_prompt_attn_kernel, tq=tq, topk=topk, idx_bits=idx_bits),
        grid=(b, nb),
        in_specs=[pl.BlockSpec((N_HEADS, tq, LANES), lambda bi, i: (0, bi * nb + i, 0)),
                  blk(IDX_HEADS * LANES), blk(LANES),
                  full((nb, KV_WIDTH, tq)), full((s, KV_WIDTH)), full((nb, 2 * IDX_DIM, tq))],
        out_specs=blk(ATTN_WIDTH),
        out_shape=jax.ShapeDtypeStruct((b, s, ATTN_WIDTH), F32),
        scratch_shapes=[pltpu.VMEM((nb, tq, tq), F32),
                        pltpu.VMEM((nb, min(128, tq), LANES), F32),
                        pltpu.VMEM((IDX_HEADS, tq, LANES), F32),
                        pltpu.VMEM((N_HEADS, tq, LANES), F32),
                        pltpu.VMEM((N_HEADS, tq, LANES), F32),
                        pltpu.VMEM((N_HEADS, tq, KV_WIDTH), F32),
                        pltpu.VMEM((tq, LANES), F32),
                        pltpu.VMEM((tq, LANES), I32)],
        compiler_params=_cparams(("arbitrary", "arbitrary")),
        name="prompt_attn",
    )(q, qi, kiwi, kt, v, kit)


def _sample_attn_kernel(pt_ref, q_ref, qih_ref, wib_ref, kin_ref, kn_ref, vn_ref, *rest,
                        n_steps, pps, n_tok, tpad, page, topk, idx_bits):
    cik_refs, ck_refs, cv_refs = rest[:pps], rest[pps:2 * pps], rest[2 * pps:3 * pps]
    o_ref, sc_ref, sa_ref, vs_ref = rest[3 * pps:]
    step = pl.program_id(1)
    n_pages = n_steps * pps
    nkb = n_pages + 1
    qih = qih_ref[...]
    wib = wib_ref[...]
    q = q_ref[...]

    def idx_score(s):
        acc = None
        for h in range(IDX_HEADS):
            t = jnp.maximum(s[h * tpad:(h + 1) * tpad], 0.0) * wib[h * tpad:(h + 1) * tpad]
            acc = t if acc is None else acc + t
        return acc

    for j in range(pps):
        pg = step * pps + j
        sc_ref[pg] = idx_score(jnp.dot(qih, cik_refs[j][...].astype(BF16), preferred_element_type=F32))
        sa_ref[pg] = jnp.dot(q, ck_refs[j][...].astype(BF16), preferred_element_type=F32)
        vs_ref[pg] = cv_refs[j][...].astype(BF16)

    @pl.when(step == n_steps - 1)
    def _():
        s_new = idx_score(jnp.dot(qih, kin_ref[...], preferred_element_type=F32))
        tok = lax.broadcasted_iota(I32, (tpad, page), 0)
        col = lax.broadcasted_iota(I32, (tpad, page), 1)
        sc_ref[n_pages] = jnp.where((col <= tok) & (col < n_tok), s_new, -jnp.inf)
        sa_ref[n_pages] = jnp.dot(q, kn_ref[...], preferred_element_type=F32)
        vs_ref[n_pages] = vn_ref[...]

        kpos = (lax.broadcasted_iota(I32, (nkb, tpad, page), 0) * page
                + lax.broadcasted_iota(I32, (nkb, tpad, page), 2))

        def cnt(pred):
            c = jnp.sum(jnp.where(pred(sc_ref[...], kpos), 1.0, 0.0), axis=0)
            return jnp.broadcast_to(jnp.sum(c, axis=1, keepdims=True), c.shape)

        thr, jmax = _select_rows(cnt, tpad, topk, idx_bits, tok[:, :LANES] < n_tok)
        sc_ref[...] = jnp.where(_sel_mask(sc_ref[...], kpos, thr[None], jmax[None]), 0.0, NEG_BIG)

        heads = N_HEADS
        inv_l = []
        for t in range(n_tok):
            s = sa_ref[:, t * heads:(t + 1) * heads, :] + sc_ref[:, t:t + 1, :]
            m = jnp.max(jnp.max(s, axis=0), axis=1, keepdims=True)
            pr = jnp.exp2(s - m[None])
            sa_ref[:, t * heads:(t + 1) * heads, :] = pr
            l = jnp.sum(jnp.sum(pr, axis=0), axis=1, keepdims=True)
            inv_l.append(jnp.broadcast_to(1.0 / l, (heads, KV_WIDTH)))

        def pv_body(kb, acc):
            return acc + lax.dot_general(sa_ref[kb].astype(BF16), vs_ref[kb], NT_DIMS, preferred_element_type=F32)

        acc = lax.fori_loop(0, nkb, pv_body, jnp.zeros((n_tok * heads, KV_WIDTH), F32), unroll=5)
        o_ref[...] = acc * jnp.concatenate(inv_l, axis=0)


def _sample_attn(page_table, q, qih, wib, kin_t, kn_t, vn_t, cik_t, ck_t, cv_t, n_tok, tpad, topk, pps):
    n, n_pages = page_table.shape
    page = ck_t.shape[2]
    assert page == LANES and KV_WIDTH == LANES and n_pages % pps == 0
    n_steps = n_pages // pps
    idx_bits = int((n_pages + 1) * page).bit_length()
    rows = n_tok * N_HEADS
    seq = lambda r, w: pl.BlockSpec((None, r, w), lambda b, p, pt: (b, 0, 0))

    def paged(r, j):
        return pl.BlockSpec((None, r, page), lambda b, p, pt: (pt[b, p * pps + j], 0, 0))

    grid_spec = pltpu.PrefetchScalarGridSpec(
        num_scalar_prefetch=1,
        grid=(n, n_steps),
        in_specs=([seq(rows, KV_WIDTH), seq(IDX_HEADS * tpad, IDX_DIM), seq(IDX_HEADS * tpad, LANES),
                   seq(IDX_DIM, page), seq(KV_WIDTH, page), seq(KV_WIDTH, page)]
                  + [paged(IDX_DIM, j) for j in range(pps)]
                  + [paged(KV_WIDTH, j) for j in range(pps)]
                  + [paged(KV_WIDTH, j) for j in range(pps)]),
        out_specs=seq(rows, KV_WIDTH),
        scratch_shapes=[pltpu.VMEM((n_pages + 1, tpad, page), F32),
                        pltpu.VMEM((n_pages + 1, rows, page), F32),
                        pltpu.VMEM((n_pages + 1, KV_WIDTH, page), BF16)],
    )
    return pl.pallas_call(
        functools.partial(_sample_attn_kernel, n_steps=n_steps, pps=pps, n_tok=n_tok, tpad=tpad, page=page,
                          topk=topk, idx_bits=idx_bits),
        grid_spec=grid_spec,
        out_shape=jax.ShapeDtypeStruct((n, rows, KV_WIDTH), F32),
        compiler_params=_cparams(("arbitrary", "arbitrary")),
        name="sample_attn",
    )(page_table, q, qih, wib, kin_t, kn_t, vn_t, *([cik_t] * pps), *([ck_t] * pps), *([cv_t] * pps))


def _mix_out_tail(x, o, gb, u0, um1, um2, wc_ref, wo_ref, gt, gpost, aw):
    y = wc_ref[0:1, :] * um2 + wc_ref[1:2, :] * um1 + wc_ref[2:3, :] * u0
    mixed = (jnp.dot(o.astype(BF16), wo_ref[0:aw, :], preferred_element_type=F32)
             + jnp.dot((gb * y).astype(BF16), wo_ref[aw:, :], preferred_element_type=F32))
    return x + gt * (_rms(mixed) * gpost)


def _mix_out_prompt_kernel(x_ref, o_ref_in, gb_ref, u_ref, halo_ref, gt_ref, gpost_ref, wc_ref, wo_ref,
                           out_ref, ext_ref, *, tm, tiles_per_seq):
    i = pl.program_id(0)
    halo = halo_ref[...]
    ext_ref[0:SUBLANES, :] = jnp.where(i % tiles_per_seq == 0, jnp.zeros_like(halo), halo)
    ext_ref[SUBLANES:, :] = u_ref[...]
    um2 = ext_ref[SUBLANES - 2:SUBLANES - 2 + tm, :]
    um1 = ext_ref[SUBLANES - 1:SUBLANES - 1 + tm, :]
    out_ref[...] = _mix_out_tail(x_ref[...], o_ref_in[...], gb_ref[...], u_ref[...], um1, um2,
                                 wc_ref, wo_ref, gt_ref[0], gpost_ref[...], ATTN_WIDTH)


def _mix_out_prompt(x, o, gb, u, gate, g_post, w_conv, w_o, tm, tiles_per_seq):
    rows, d = x.shape
    cw = u.shape[1]
    const = lambda i: (0, 0)
    row = lambda i: (i, 0)
    hb = tm // SUBLANES
    return pl.pallas_call(
        functools.partial(_mix_out_prompt_kernel, tm=tm, tiles_per_seq=tiles_per_seq),
        grid=(rows // tm,),
        in_specs=[pl.BlockSpec((tm, d), row),
                  pl.BlockSpec((tm, ATTN_WIDTH), row),
                  pl.BlockSpec((tm, cw), row),
                  pl.BlockSpec((tm, cw), row),
                  pl.BlockSpec((SUBLANES, cw), lambda i: (jnp.maximum(i * hb - 1, 0), 0)),
                  _mod_spec(gate, tm, tiles_per_seq),
                  pl.BlockSpec((1, d), const),
                  pl.BlockSpec(w_conv.shape, const),
                  pl.BlockSpec(w_o.shape, const)],
        out_specs=pl.BlockSpec((tm, d), row),
        out_shape=jax.ShapeDtypeStruct((rows, d), F32),
        scratch_shapes=[pltpu.VMEM((tm + SUBLANES, cw), F32)],
        compiler_params=_cparams(("arbitrary",)),
        name="mix_out_prompt",
    )(x, o, gb, u, u, gate, g_post, w_conv, w_o)


def _mix_out_sample_kernel(x_ref, o_ref_in, gb_ref, um2_ref, um1_ref, u0_ref, gt_ref, gpost_ref, wc_ref, wo_ref,
                           out_ref):
    out_ref[...] = _mix_out_tail(x_ref[...], o_ref_in[...], gb_ref[...], u0_ref[...], um1_ref[...], um2_ref[...],
                                 wc_ref, wo_ref, gt_ref[0], gpost_ref[...], ATTN_WIDTH)


def _mix_out_sample(x, o, gb, ext, gate, g_post, w_conv, w_o, tm):
    rows, d = x.shape
    cw = gb.shape[1]
    const = lambda i: (0, 0)
    row = lambda i: (i, 0)
    return pl.pallas_call(
        _mix_out_sample_kernel,
        grid=(rows // tm,),
        in_specs=[pl.BlockSpec((tm, d), row),
                  pl.BlockSpec((tm, ATTN_WIDTH), row),
                  pl.BlockSpec((tm, cw), row),
                  pl.BlockSpec((tm, cw), lambda i: (i, 0)),
                  pl.BlockSpec((tm, cw), lambda i: (i + 1, 0)),
                  pl.BlockSpec((tm, cw), lambda i: (i + 2, 0)),
                  pl.BlockSpec((1, tm, d), lambda i: (0, 0, 0)),
                  pl.BlockSpec((1, d), const),
                  pl.BlockSpec(w_conv.shape, const),
                  pl.BlockSpec(w_o.shape, const)],
        out_specs=pl.BlockSpec((tm, d), row),
        out_shape=jax.ShapeDtypeStruct((rows, d), F32),
        compiler_params=_cparams(("arbitrary",)),
        name="mix_out_sample",
    )(x, o, gb, ext, ext, ext, gate, g_post, w_conv, w_o)


def _prompt_layer(x, mod, lw, tm, tq):
    b, s, d = x.shape
    conv_ch = lw["w_conv"].shape[1]
    tps = s // tm
    md = lambda j, k: mod[:, j, k][:, None, :]
    xr = x.reshape(b * s, d)
    x1 = _ffn(xr, md(0, 0), md(0, 1), md(0, 2), lw["g_pre"][0:1], lw["g_post"][0:1], lw["ffn1_w13"], lw["ffn1_w2"],
              tm, tps)
    tabs = _rope_tables(jnp.arange(s))
    q, k, v, qi, kiwi, gb, u = _mix_in(x1, md(1, 0), md(1, 1), lw["g_pre"][1:2], lw["w_in_p"], tabs, tm, tps, tps,
                                       conv_ch)
    nb = s // tq
    kt = jnp.swapaxes(k.astype(BF16).reshape(b, nb, tq, KV_WIDTH), 2, 3)
    kit = jnp.swapaxes(kiwi[:, :IDX_DIM].astype(BF16).reshape(b, nb, tq, IDX_DIM), 2, 3)
    kit = jnp.concatenate([kit, kit], axis=2)
    topk = min(TOPK_MAX, s // 4)
    o = _prompt_attn(q, qi.reshape(b, s, -1), kiwi.reshape(b, s, -1), kt,
                     v.astype(BF16).reshape(b, s, KV_WIDTH), kit, tq, topk)
    x2 = _mix_out_prompt(x1, o.reshape(b * s, -1), gb, u, md(1, 2), lw["g_post"][1:2], lw["w_conv"], lw["w_o"],
                         tm, tps)
    x3 = _ffn(x2, md(2, 0), md(2, 1), md(2, 2), lw["g_pre"][2:3], lw["g_post"][2:3], lw["ffn2_w13"], lw["ffn2_w2"],
              tm, tps)
    state = (k.reshape(b, s, N_KV_HEADS, HEAD_DIM), v.reshape(b, s, N_KV_HEADS, HEAD_DIM),
             kiwi[:, :IDX_DIM].reshape(b, s, IDX_DIM), u.reshape(b, s, conv_ch)[:, s - (CONV_K - 1):])
    return x3.reshape(b, s, d), state


def _sample_layer(x_tm, mod, lw, cache_k, cache_v, cache_idx_k, state_conv, page_table):
    n = mod.shape[0]
    t_tok = x_tm.shape[0] // n
    d = x_tm.shape[1]
    conv_ch = lw["w_conv"].shape[1]
    n_pages = page_table.shape[1]
    page = cache_k.shape[1]
    past = n_pages * page
    md = lambda j, k: mod[:, j, k][None]
    x1 = _ffn(x_tm, md(0, 0), md(0, 1), md(0, 2), lw["g_pre"][0:1], lw["g_post"][0:1], lw["ffn1_w13"],
              lw["ffn1_w2"], n, t_tok)
    pos = jnp.repeat(past + jnp.arange(t_tok), n)
    tabs = _rope_tables(pos)
    q, k, v, qi, kiwi, gb, u = _mix_in(x1, md(1, 0), md(1, 1), lw["g_pre"][1:2], lw["w_in_p"], tabs, n, t_tok, t_tok,
                                       conv_ch)
    tpad = SUBLANES
    assert t_tok <= tpad

    def seq_major(a, heads, width):
        a = a.reshape(t_tok, n, heads, width).transpose(1, 2, 0, 3)
        a = jnp.pad(a, ((0, 0), (0, 0), (0, tpad - t_tok), (0, 0)))
        return a.reshape(n, heads * tpad, width)

    q_seq = q.reshape(N_HEADS, t_tok, n, LANES).transpose(2, 1, 0, 3).reshape(n, t_tok * N_HEADS, LANES)
    qi_nat = jnp.concatenate([qi[:, h * LANES + (h % 2) * IDX_DIM:h * LANES + (h % 2 + 1) * IDX_DIM]
                              for h in range(IDX_HEADS)], axis=1)
    qih = seq_major(qi_nat, IDX_HEADS, IDX_DIM)
    wib = jnp.broadcast_to(seq_major(kiwi[:, IDX_DIM:IDX_DIM + IDX_HEADS], IDX_HEADS, 1),
                           (n, IDX_HEADS * tpad, LANES))

    def new_keys_t(a):
        a = a.reshape(t_tok, n, -1).transpose(1, 2, 0).astype(BF16)
        return jnp.pad(a, ((0, 0), (0, 0), (0, page - t_tok)))

    ck_t = cache_k.transpose(0, 2, 3, 1).reshape(-1, KV_WIDTH, page)
    cv_t = cache_v.transpose(0, 2, 3, 1).reshape(-1, KV_WIDTH, page)
    cik_t = cache_idx_k.transpose(0, 2, 1)
    topk = min(TOPK_MAX, (past + t_tok) // 4)
    pps = max(p for p in (16, 8, 4, 2, 1) if n_pages % p == 0)
    o = _sample_attn(page_table, q_seq, qih, wib, new_keys_t(kiwi[:, :IDX_DIM]), new_keys_t(k), new_keys_t(v),
                     cik_t, ck_t, cv_t, t_tok, tpad, topk, pps)
    o = o.reshape(n, t_tok, N_KV_HEADS, HEADS_PER_KV, N_KV_HEADS, HEAD_DIM)
    o = jnp.stack([o[:, :, g, :, g] for g in range(N_KV_HEADS)], axis=2)
    o_tm = o.transpose(1, 0, 2, 3, 4).reshape(t_tok * n, ATTN_WIDTH)
    ext = jnp.concatenate([state_conv.transpose(1, 0, 2).reshape(-1, conv_ch), u], axis=0)
    x2 = _mix_out_sample(x1, o_tm, gb, ext, md(1, 2), lw["g_post"][1:2], lw["w_conv"], lw["w_o"], n)
    x3 = _ffn(x2, md(2, 0), md(2, 1), md(2, 2), lw["g_pre"][2:3], lw["g_post"][2:3], lw["ffn2_w13"],
              lw["ffn2_w2"], n, t_tok)
    tm_to_seq = lambda a: a.reshape((t_tok, n) + a.shape[1:]).swapaxes(0, 1)
    state = (tm_to_seq(k).reshape(n, t_tok, N_KV_HEADS, HEAD_DIM), tm_to_seq(v).reshape(n, t_tok, N_KV_HEADS, HEAD_DIM),
             tm_to_seq(kiwi[:, :IDX_DIM]), tm_to_seq(u)[:, t_tok - (CONV_K - 1):])
    return x3, state


def _row_tile(s, target):
    t = min(target, s)
    while s % t:
        t //= 2
    return t


def kernel(x_prompt, x_sample, cache_k, cache_v, cache_idx_k, state_conv, page_table, c_prompt, c_sample, w_ada, b_ada, g_pre, g_post, ffn1_w13, ffn1_w2, w_in, w_conv, w_o, ffn2_w13, ffn2_w2):
    depth = w_ada.shape[0]
    b, s, d = x_prompt.shape
    n, t_tok, _ = x_sample.shape
    conv_ch = w_conv.shape[2]
    tm = _row_tile(s, 512)
    tq = _row_tile(s, 256)

    yp = x_prompt
    ys = x_sample.swapaxes(0, 1).reshape(t_tok * n, d)
    c_all = jnp.concatenate([c_prompt, c_sample], axis=0)
    n_c = c_all.shape[0]
    c_all = jnp.pad(c_all, ((0, -n_c % SUBLANES), (0, 0)))
    outs = [[] for _ in range(8)]
    for l in range(depth):
        lw = dict(g_pre=g_pre[l], g_post=g_post[l], w_conv=w_conv[l],
                  ffn1_w13=ffn1_w13[l].astype(BF16), ffn1_w2=ffn1_w2[l].astype(BF16),
                  ffn2_w13=ffn2_w13[l].astype(BF16), ffn2_w2=ffn2_w2[l].astype(BF16),
                  w_in_p=_pack_w_in(w_in[l], conv_ch), w_o=w_o[l].astype(BF16))
        mod = _ada_mod(c_all, w_ada[l], b_ada[l])[:n_c].reshape(n_c, N_SUB, 3, d)
        yp, st_p = _prompt_layer(yp, mod[:b], lw, tm, tq)
        ys, st_s = _sample_layer(ys, mod[b:], lw, cache_k[l], cache_v[l], cache_idx_k[l], state_conv[l], page_table)
        for lst, a in zip(outs, st_p + st_s):
            lst.append(a)
    ys = ys.reshape(t_tok, n, d).swapaxes(0, 1)
    return (yp, ys) + tuple(jnp.stack(lst) for lst in outs)
```

```python
import functools

import numpy as np
import jax
import jax.numpy as jnp
from jax import lax
from jax.experimental import pallas as pl
from jax.experimental.pallas import tpu as pltpu

F32 = jnp.float32
BF16 = jnp.bfloat16
I32 = jnp.int32

N_HEADS = 8
HEAD_DIM = 64
N_KV_HEADS = 2
HEADS_PER_KV = N_HEADS // N_KV_HEADS
ATTN_WIDTH = N_HEADS * HEAD_DIM
KV_WIDTH = N_KV_HEADS * HEAD_DIM
ROT_DIV = 4
ROT = HEAD_DIM // ROT_DIV
ROT_HALF = ROT // 2
ROPE_THETA = 500000.0
IDX_HEADS = 4
IDX_DIM = 64
IDX_SCALE = (IDX_HEADS * IDX_DIM) ** -0.5
TOPK_MAX = 256
CONV_K = 3
N_SUB = 3
RMS_EPS = 1e-6
Q_SCALE = HEAD_DIM ** -0.5 * float(np.log2(np.e))

LANES = 128
SUBLANES = 8
VMEM_LIMIT = 56 * 1024 * 1024

NEG_BIG = -1e30
F32_MAX = float(np.finfo(np.float32).max)
INT_MIN = -(2 ** 31)
INT_MAX = 2 ** 31 - 1

NT_DIMS = (((1,), (1,)), ((), ()))


def _cparams(sem):
    return pltpu.CompilerParams(dimension_semantics=sem, vmem_limit_bytes=VMEM_LIMIT)


def _silu(x):
    return x * (1.0 / (1.0 + jnp.exp(-x)))


def _rms(x):
    return x * lax.rsqrt(jnp.mean(x * x, axis=-1, keepdims=True) + RMS_EPS)


def _tile_lanes(x, reps):
    return x if reps == 1 else jnp.concatenate([x] * reps, axis=1)


def _ada_kernel(c_ref, w_ref, b_ref, o_ref):
    a = _silu(c_ref[...]).astype(BF16)
    o_ref[...] = jnp.dot(a, w_ref[...].astype(BF16), preferred_element_type=F32) + b_ref[...]


def _ada_mod(c, w_ada, b_ada):
    n, d = c.shape
    nout = w_ada.shape[1]
    tn = 9 * LANES if nout % (9 * LANES) == 0 else LANES
    return pl.pallas_call(
        _ada_kernel,
        grid=(nout // tn,),
        in_specs=[pl.BlockSpec((n, d), lambda j: (0, 0)),
                  pl.BlockSpec((d, tn), lambda j: (0, j)),
                  pl.BlockSpec((1, tn), lambda j: (0, j))],
        out_specs=pl.BlockSpec((n, tn), lambda j: (0, j)),
        out_shape=jax.ShapeDtypeStruct((n, nout), F32),
        compiler_params=_cparams(("arbitrary",)),
        name="ada_mod",
    )(c, w_ada, b_ada.reshape(1, nout))


def _ffn_kernel(x_ref, sh_ref, sc_ref, gt_ref, gpre_ref, gpost_ref, w13_ref, w2_ref, o_ref, *, d_ff, chunk):
    x = x_ref[...]
    h = _rms(x) * gpre_ref[...] * (1.0 + sc_ref[0]) + sh_ref[0]
    hb = h.astype(BF16)
    y = jnp.zeros(x.shape, F32)
    for c in range(d_ff // chunk):
        a = jnp.dot(hb, w13_ref[:, c * chunk:(c + 1) * chunk], preferred_element_type=F32)
        b = jnp.dot(hb, w13_ref[:, d_ff + c * chunk:d_ff + (c + 1) * chunk], preferred_element_type=F32)
        g = (_silu(a) * b).astype(BF16)
        y = y + jnp.dot(g, w2_ref[c * chunk:(c + 1) * chunk, :], preferred_element_type=F32)
    o_ref[...] = x + 0.5 * gt_ref[0] * (_rms(y) * gpost_ref[...])


def _mod_spec(mod, tm, tiles_per_group):
    _, r, d = mod.shape
    return pl.BlockSpec((1, r, d), lambda i: (i // tiles_per_group, 0, 0))


def _ffn(x, shift, scale, gate, g_pre, g_post, w13, w2, tm, tiles_per_group):
    rows, d = x.shape
    d_ff = w2.shape[0]
    const = lambda i: (0, 0)
    return pl.pallas_call(
        functools.partial(_ffn_kernel, d_ff=d_ff, chunk=min(512, d_ff)),
        grid=(rows // tm,),
        in_specs=[pl.BlockSpec((tm, d), lambda i: (i, 0)),
                  _mod_spec(shift, tm, tiles_per_group),
                  _mod_spec(scale, tm, tiles_per_group),
                  _mod_spec(gate, tm, tiles_per_group),
                  pl.BlockSpec((1, d), const),
                  pl.BlockSpec((1, d), const),
                  pl.BlockSpec(w13.shape, const),
                  pl.BlockSpec(w2.shape, const)],
        out_specs=pl.BlockSpec((tm, d), lambda i: (i, 0)),
        out_shape=jax.ShapeDtypeStruct((rows, d), F32),
        compiler_params=_cparams(("arbitrary",)),
        name="ffn",
    )(x, shift, scale, gate, g_pre, g_post, w13, w2)


_OFF_Q = 0
_OFF_K = _OFF_Q + ATTN_WIDTH
_OFF_V = _OFF_K + KV_WIDTH
_OFF_QI = _OFF_V + KV_WIDTH
_OFF_KIWI = _OFF_QI + IDX_HEADS * IDX_DIM
_OFF_GB = _OFF_KIWI + LANES


def _pack_w_in(w_in, conv_ch):
    splits = [ATTN_WIDTH, KV_WIDTH, KV_WIDTH, IDX_HEADS * IDX_DIM, IDX_DIM, IDX_HEADS, conv_ch, conv_ch, conv_ch]
    offs = np.cumsum([0] + splits)
    d = w_in.shape[0]
    pad = jnp.zeros((d, LANES - IDX_DIM - IDX_HEADS), w_in.dtype)
    return jnp.concatenate([w_in[:, :offs[6]], pad, w_in[:, offs[6]:]], axis=1).astype(BF16)


def _rope_tables(pos):
    inv = jnp.power(ROPE_THETA, -jnp.arange(ROT_HALF, dtype=F32) * 2.0 / ROT)
    ang = pos.astype(F32)[:, None] * inv[None, :]
    cos, sin = jnp.cos(ang), jnp.sin(ang)
    n = pos.shape[0]
    one = jnp.ones((n, HEAD_DIM - ROT), F32)
    zero_r = jnp.zeros((n, HEAD_DIM - ROT), F32)
    zero_h = jnp.zeros((n, ROT_HALF), F32)
    c = jnp.concatenate([cos, cos, one], axis=1)
    s1 = jnp.concatenate([-sin, zero_h, zero_r], axis=1)
    s2 = jnp.concatenate([zero_h, sin, zero_r], axis=1)
    rep = LANES // HEAD_DIM
    return jnp.tile(c, (1, rep)), jnp.tile(s1, (1, rep)), jnp.tile(s2, (1, rep))


def _rope128(x, c, s1, s2):
    return x * c + pltpu.roll(x, LANES - ROT_HALF, 1) * s1 + pltpu.roll(x, ROT_HALF, 1) * s2


def _mix_in_kernel(x_ref, sh_ref, sc_ref, gpre_ref, w_ref, rc_ref, rs1_ref, rs2_ref,
                   q_ref, k_ref, v_ref, qi_ref, kiwi_ref, gb_ref, u_ref, *, conv_ch):
    x = x_ref[...]
    hb = (_rms(x) * gpre_ref[...] * (1.0 + sc_ref[0]) + sh_ref[0]).astype(BF16)
    c, s1, s2 = rc_ref[...], rs1_ref[...], rs2_ref[...]

    def proj(off, width):
        return jnp.dot(hb, w_ref[:, off:off + width], preferred_element_type=F32)

    def rope_cols(z):
        return [_rope128(z[:, g * LANES:(g + 1) * LANES], c, s1, s2) for g in range(z.shape[1] // LANES)]

    lane = lax.broadcasted_iota(I32, (x.shape[0], LANES), 1)
    low = lane < HEAD_DIM

    zq = proj(_OFF_Q, ATTN_WIDTH)
    for g, r in enumerate(rope_cols(zq)):
        r = r * Q_SCALE
        swapped = pltpu.roll(r, HEAD_DIM, 1)
        for e in range(2):
            h = 2 * g + e
            kv_group = h // HEADS_PER_KV
            src = r if e == kv_group else swapped
            keep = low if kv_group == 0 else jnp.logical_not(low)
            q_ref[h] = jnp.where(keep, src, 0.0).astype(BF16)
    zk = proj(_OFF_K, KV_WIDTH)
    for g, r in enumerate(rope_cols(zk)):
        k_ref[:, g * LANES:(g + 1) * LANES] = r
    v_ref[...] = proj(_OFF_V, KV_WIDTH)
    zqi = proj(_OFF_QI, IDX_HEADS * IDX_DIM)
    for g, r in enumerate(rope_cols(zqi)):
        qi_ref[:, (2 * g) * LANES:(2 * g + 1) * LANES] = jnp.where(low, r, 0.0).astype(BF16)
        qi_ref[:, (2 * g + 1) * LANES:(2 * g + 2) * LANES] = jnp.where(low, 0.0, r).astype(BF16)
    zkw = proj(_OFF_KIWI, LANES)
    kiwi_ref[...] = jnp.where(lane < IDX_DIM, _rope128(zkw, c, s1, s2), zkw * IDX_SCALE)
    gb_ref[...] = proj(_OFF_GB, conv_ch)
    u_ref[...] = proj(_OFF_GB + conv_ch, conv_ch) * proj(_OFF_GB + 2 * conv_ch, conv_ch)


def _mix_in(x, shift, scale, g_pre, w_in_p, rope_tabs, tm, tiles_per_group, rope_tiles, conv_ch):
    rows, d = x.shape
    const = lambda i: (0, 0)
    row = lambda i: (i, 0)
    rope_spec = pl.BlockSpec((tm, LANES), lambda i: (i % rope_tiles, 0))
    assert 2 * HEAD_DIM == LANES and KV_WIDTH == LANES and 2 * IDX_DIM == LANES
    widths = [(KV_WIDTH, F32), (KV_WIDTH, F32), (IDX_HEADS * LANES, BF16),
              (LANES, F32), (conv_ch, F32), (conv_ch, F32)]
    q_spec = pl.BlockSpec((N_HEADS, tm, LANES), lambda i: (0, i, 0))
    q_shape = jax.ShapeDtypeStruct((N_HEADS, rows, LANES), BF16)
    return pl.pallas_call(
        functools.partial(_mix_in_kernel, conv_ch=conv_ch),
        grid=(rows // tm,),
        in_specs=[pl.BlockSpec((tm, d), row),
                  _mod_spec(shift, tm, tiles_per_group),
                  _mod_spec(scale, tm, tiles_per_group),
                  pl.BlockSpec((1, d), const),
                  pl.BlockSpec(w_in_p.shape, const),
                  rope_spec, rope_spec, rope_spec],
        out_specs=[q_spec] + [pl.BlockSpec((tm, w), row) for w, _ in widths],
        out_shape=[q_shape] + [jax.ShapeDtypeStruct((rows, w), dt) for w, dt in widths],
        compiler_params=_cparams(("arbitrary",)),
        name="mix_in",
    )(x, shift, scale, g_pre, w_in_p, *rope_tabs)


def _key_to_float(u):
    o = u ^ INT_MIN
    bits = jnp.where(o >= 0, o, o ^ INT_MAX)
    return lax.bitcast_convert_type(bits, F32)


def _count(sc_ref, nkb, r0, rg, tk, pred):
    def body(kb, acc):
        for j in range(tk // LANES):
            x = sc_ref[kb, r0:r0 + rg, j * LANES:(j + 1) * LANES]
            kpos = kb * tk + j * LANES + lax.broadcasted_iota(I32, (rg, LANES), 1)
            acc = acc + jnp.where(pred(x, kpos), 1.0, 0.0)
        return acc

    group = 4

    def group_body(i, acc):
        for g in range(group):
            acc = body(group * i + g, acc)
        return acc

    acc = lax.fori_loop(0, nkb // group, group_body, jnp.zeros((rg, LANES), F32))
    acc = lax.fori_loop(group * (nkb // group), nkb, body, acc)
    return jnp.broadcast_to(jnp.sum(acc, axis=1, keepdims=True), (rg, LANES))


def _select_rows(cnt, rg, topk, idx_bits, row_ok, tie_fn=None):
    def bit_body(i, key):
        cand = key | jnp.left_shift(jnp.int32(1), 31 - i)
        t = _key_to_float(cand)
        c = cnt(lambda x, kp: x >= t)
        return jnp.where(c >= topk, cand, key)

    key = lax.fori_loop(0, 32, bit_body, jnp.zeros((rg, LANES), I32))
    c_fin = cnt(lambda x, kp: x >= -F32_MAX)
    thr = jnp.where(c_fin >= topk, _key_to_float(key), -F32_MAX)
    c_gt = cnt(lambda x, kp: x > thr)
    c_ge = cnt(lambda x, kp: x >= thr)
    need = topk - c_gt
    excess = jnp.max(jnp.where((c_ge > topk) & row_ok, 1.0, 0.0)) > 0.5

    def tie_search():
        if tie_fn is not None:
            return tie_fn(thr, need)

        def jbody(i, p):
            cand = p | jnp.left_shift(jnp.int32(1), idx_bits - 1 - i)
            c = cnt(lambda x, kp: (x == thr) & (kp < cand))
            return jnp.where(c < need, cand, p)
        return lax.fori_loop(0, idx_bits, jbody, jnp.zeros((rg, LANES), I32))

    jmax = lax.cond(excess, tie_search, lambda: jnp.full((rg, LANES), INT_MAX, I32))
    return thr, jmax


def _tie_bound_blocked(sc_ref, cnt_ref, nkb, r0, rg, tk, thr, need):
    reps = tk // LANES
    thr_t = _tile_lanes(thr, reps)

    group = 4

    def count_body(i, carry):
        for g in range(group):
            kb = jnp.minimum(group * i + g, nkb - 1)
            eq = jnp.where(sc_ref[kb, r0:r0 + rg, :] == thr_t, 1.0, 0.0)
            c = sum(eq[:, j * LANES:(j + 1) * LANES] for j in range(reps))
            cnt_ref[kb] = jnp.broadcast_to(jnp.sum(c, axis=1, keepdims=True), (rg, LANES))
        return carry

    lax.fori_loop(0, (nkb + group - 1) // group, count_body, 0)

    def locate_body(kb, carry):
        run, kstar, before = carry
        new_run = run + cnt_ref[kb]
        hit = (run < need) & (new_run >= need)
        return new_run, jnp.where(hit, kb, kstar), jnp.where(hit, run, before)

    zeros = jnp.zeros((rg, LANES), F32)
    _, kstar, before = lax.fori_loop(0, nkb, locate_body, (zeros, jnp.full((rg, LANES), -1, I32), zeros))
    kstar_t = _tile_lanes(kstar, reps)

    def gather_body(kb, own):
        eq = jnp.where(sc_ref[kb, r0:r0 + rg, :] == thr_t, 1.0, 0.0)
        return jnp.where(kstar_t == kb, eq, own)

    own = lax.fori_loop(0, nkb, gather_body, jnp.zeros((rg, tk), F32))
    need_in = need - before
    col = lax.broadcasted_iota(I32, (rg, tk), 1)
    p = jnp.zeros((rg, LANES), I32)
    for b in reversed(range(int(tk - 1).bit_length())):
        cand = p | (1 << b)
        c = jnp.sum(jnp.where(col < _tile_lanes(cand, reps), own, 0.0), axis=1, keepdims=True)
        p = jnp.where(c < need_in, cand, p)
    return jnp.where(kstar >= 0, kstar * tk + p, INT_MAX)


def _select(sc_ref, cnt_ref, t_ref, j_ref, nkb, rows, rg, tk, topk, idx_bits):
    for r0 in range(0, rows, rg):
        cnt = functools.partial(_count, sc_ref, nkb, r0, rg, tk)
        tie_fn = functools.partial(_tie_bound_blocked, sc_ref, cnt_ref, nkb, r0, rg, tk)
        thr, jmax = _select_rows(cnt, rg, topk, idx_bits, True, tie_fn)
        t_ref[r0:r0 + rg, :] = thr
        j_ref[r0:r0 + rg, :] = jmax


def _sel_mask(scb, kpos, thr, jmax):
    return (scb > thr) | ((scb == thr) & (kpos <= jmax))


def _prompt_attn_kernel(q_ref, qi_ref, kiwi_ref, kt_ref, v_ref, kit_ref, o_ref,
                        sc_ref, cnt_ref, wib_ref, m_ref, l_ref, acc_ref, t_ref, j_ref,
                        *, tq, topk, idx_bits):
    tk = tq
    reps = tk // LANES
    i = pl.program_id(1)
    nkb = i + 1

    kiwi = kiwi_ref[...]
    for h in range(IDX_HEADS):
        wib_ref[h] = jnp.broadcast_to(kiwi[:, IDX_DIM + h:IDX_DIM + h + 1], (tq, LANES))

    def score_body(kb, carry):
        kit = kit_ref[kb]
        acc = None
        for h in range(IDX_HEADS):
            s = jnp.dot(qi_ref[:, h * LANES:(h + 1) * LANES], kit, preferred_element_type=F32)
            t = jnp.maximum(s, 0.0) * _tile_lanes(wib_ref[h], reps)
            acc = t if acc is None else acc + t
        sc_ref[kb] = acc
        return carry

    lax.fori_loop(0, nkb, score_body, 0)
    row = lax.broadcasted_iota(I32, (tq, tk), 0)
    col = lax.broadcasted_iota(I32, (tq, tk), 1)
    sc_ref[i] = jnp.where(col <= row, sc_ref[i], -jnp.inf)

    _select(sc_ref, cnt_ref, t_ref, j_ref, nkb, tq, min(128, tq), tk, topk, idx_bits)

    def bias_body(kb, carry):
        thr = _tile_lanes(t_ref[...], reps)
        jmax = _tile_lanes(j_ref[...], reps)
        sc_ref[kb] = jnp.where(_sel_mask(sc_ref[kb], kb * tk + col, thr, jmax), 0.0, NEG_BIG)
        return carry

    lax.fori_loop(0, nkb, bias_body, 0)

    m_ref[...] = jnp.full(m_ref.shape, NEG_BIG, F32)
    l_ref[...] = jnp.zeros(l_ref.shape, F32)
    acc_ref[...] = jnp.zeros(acc_ref.shape, F32)

    def attn_body(kb, carry):
        kt = kt_ref[kb]
        vb = v_ref[pl.ds(pl.multiple_of(kb * tk, tk), tk), :]
        bias = sc_ref[kb]
        for h in range(N_HEADS):
            s = jnp.dot(q_ref[h], kt, preferred_element_type=F32) + bias
            m_prev = m_ref[h]
            m_new = jnp.maximum(m_prev, jnp.max(s, axis=1, keepdims=True))
            alpha = jnp.exp2(m_prev - m_new)
            p = jnp.exp2(s - _tile_lanes(m_new, reps))
            l_ref[h] = alpha * l_ref[h] + sum(p[:, j * LANES:(j + 1) * LANES] for j in range(reps))
            acc_ref[h] = alpha * acc_ref[h] + jnp.dot(p.astype(BF16), vb, preferred_element_type=F32)
            m_ref[h] = m_new
        return carry

    lax.fori_loop(0, nkb, attn_body, 0)

    for h in range(N_HEADS):
        g = h // HEADS_PER_KV
        o = acc_ref[h] / jnp.sum(l_ref[h], axis=1, keepdims=True)
        o_ref[:, h * HEAD_DIM:(h + 1) * HEAD_DIM] = o[:, g * HEAD_DIM:(g + 1) * HEAD_DIM]


def _prompt_attn(q, qi, kiwi, kt, v, kit, tq, topk):
    b, s, _ = qi.shape
    nb = s // tq
    idx_bits = int(s).bit_length()
    blk = lambda w: pl.BlockSpec((None, tq, w), lambda bi, i: (bi, i, 0))
    full = lambda shp: pl.BlockSpec((None,) + shp, lambda bi, i: (bi,) + (0,) * len(shp),
                                    pipeline_mode=pl.Buffered(1))
    return pl.pallas_call(
        functools.partial(_prompt_attn_kernel, tq=tq, topk=topk, idx_bits=idx_bits),
        grid=(b, nb),
        in_specs=[pl.BlockSpec((N_HEADS, tq, LANES), lambda bi, i: (0, bi * nb + i, 0)),
                  blk(IDX_HEADS * LANES), blk(LANES),
                  full((nb, KV_WIDTH, tq)), full((s, KV_WIDTH)), full((nb, 2 * IDX_DIM, tq))],
        out_specs=blk(ATTN_WIDTH),
        out_shape=jax.ShapeDtypeStruct((b, s, ATTN_WIDTH), F32),
        scratch_shapes=[pltpu.VMEM((nb, tq, tq), F32),
                        pltpu.VMEM((nb, min(128, tq), LANES), F32),
                        pltpu.VMEM((IDX_HEADS, tq, LANES), F32),
                        pltpu.VMEM((N_HEADS, tq, LANES), F32),
                        pltpu.VMEM((N_HEADS, tq, LANES), F32),
                        pltpu.VMEM((N_HEADS, tq, KV_WIDTH), F32),
                        pltpu.VMEM((tq, LANES), F32),
                        pltpu.VMEM((tq, LANES), I32)],
        compiler_params=_cparams(("arbitrary", "arbitrary")),
        name="prompt_attn",
    )(q, qi, kiwi, kt, v, kit)


def _sample_attn_kernel(pt_ref, q_ref, qih_ref, wib_ref, kin_ref, kn_ref, vn_ref, *rest,
                        n_steps, pps, n_tok, tpad, page, topk, idx_bits):
    cik_refs, ck_refs, cv_refs = rest[:pps], rest[pps:2 * pps], rest[2 * pps:3 * pps]
    o_ref, sc_ref, sa_ref, vs_ref = rest[3 * pps:]
    step = pl.program_id(1)
    n_pages = n_steps * pps
    nkb = n_pages + 1
    qih = qih_ref[...]
    wib = wib_ref[...]
    q = q_ref[...]

    def idx_score(s):
        acc = None
        for h in range(IDX_HEADS):
            t = jnp.maximum(s[h * tpad:(h + 1) * tpad], 0.0) * wib[h * tpad:(h + 1) * tpad]
            acc = t if acc is None else acc + t
        return acc

    for j in range(pps):
        pg = step * pps + j
        sc_ref[pg] = idx_score(jnp.dot(qih, cik_refs[j][...].astype(BF16), preferred_element_type=F32))
        sa_ref[pg] = jnp.dot(q, ck_refs[j][...].astype(BF16), preferred_element_type=F32)
        vs_ref[pg] = cv_refs[j][...].astype(BF16)

    @pl.when(step == n_steps - 1)
    def _():
        s_new = idx_score(jnp.dot(qih, kin_ref[...], preferred_element_type=F32))
        tok = lax.broadcasted_iota(I32, (tpad, page), 0)
        col = lax.broadcasted_iota(I32, (tpad, page), 1)
        sc_ref[n_pages] = jnp.where((col <= tok) & (col < n_tok), s_new, -jnp.inf)
        sa_ref[n_pages] = jnp.dot(q, kn_ref[...], preferred_element_type=F32)
        vs_ref[n_pages] = vn_ref[...]

        kpos = (lax.broadcasted_iota(I32, (nkb, tpad, page), 0) * page
                + lax.broadcasted_iota(I32, (nkb, tpad, page), 2))

        def cnt(pred):
            w = jnp.where(pred(sc_ref[...], kpos), 1.0, 0.0)
            step = -(-nkb // SUBLANES)
            c = sum(jnp.sum(w[a:a + step], axis=0) for a in range(0, nkb, step))
            return jnp.broadcast_to(jnp.sum(c, axis=1, keepdims=True), c.shape)

        thr, jmax = _select_rows(cnt, tpad, topk, idx_bits, tok[:, :LANES] < n_tok)
        sc_ref[...] = jnp.where(_sel_mask(sc_ref[...], kpos, thr[None], jmax[None]), 0.0, NEG_BIG)

        heads = N_HEADS
        inv_l = []
        for t in range(n_tok):
            s = sa_ref[:, t * heads:(t + 1) * heads, :] + sc_ref[:, t:t + 1, :]
            m = jnp.max(jnp.max(s, axis=0), axis=1, keepdims=True)
            pr = jnp.exp2(s - m[None])
            sa_ref[:, t * heads:(t + 1) * heads, :] = pr
            l = jnp.sum(jnp.sum(pr, axis=0), axis=1, keepdims=True)
            inv_l.append(jnp.broadcast_to(1.0 / l, (heads, KV_WIDTH)))

        lanes_acc = next(g for g in (5, 4, 3, 2, 1) if nkb % g == 0)

        def pv_body(i, accs):
            return tuple(
                a + lax.dot_general(sa_ref[lanes_acc * i + g].astype(BF16), vs_ref[lanes_acc * i + g], NT_DIMS,
                                    preferred_element_type=F32)
                for g, a in enumerate(accs))

        zero = jnp.zeros((n_tok * heads, KV_WIDTH), F32)
        accs = lax.fori_loop(0, nkb // lanes_acc, pv_body, (zero,) * lanes_acc)
        o_ref[...] = sum(accs) * jnp.concatenate(inv_l, axis=0)


def _sample_attn(page_table, q, qih, wib, kin_t, kn_t, vn_t, cik_t, ck_t, cv_t, n_tok, tpad, topk, pps):
    n, n_pages = page_table.shape
    page = ck_t.shape[2]
    assert page == LANES and KV_WIDTH == LANES and n_pages % pps == 0
    n_steps = n_pages // pps
    idx_bits = int((n_pages + 1) * page).bit_length()
    rows = n_tok * N_HEADS
    seq = lambda r, w: pl.BlockSpec((None, r, w), lambda b, p, pt: (b, 0, 0))

    def paged(r, j):
        return pl.BlockSpec((None, r, page), lambda b, p, pt: (pt[b, p * pps + j], 0, 0))

    grid_spec = pltpu.PrefetchScalarGridSpec(
        num_scalar_prefetch=1,
        grid=(n, n_steps),
        in_specs=([seq(rows, KV_WIDTH), seq(IDX_HEADS * tpad, IDX_DIM), seq(IDX_HEADS * tpad, LANES),
                   seq(IDX_DIM, page), seq(KV_WIDTH, page), seq(KV_WIDTH, page)]
                  + [paged(IDX_DIM, j) for j in range(pps)]
                  + [paged(KV_WIDTH, j) for j in range(pps)]
                  + [paged(KV_WIDTH, j) for j in range(pps)]),
        out_specs=seq(rows, KV_WIDTH),
        scratch_shapes=[pltpu.VMEM((n_pages + 1, tpad, page), F32),
                        pltpu.VMEM((n_pages + 1, rows, page), F32),
                        pltpu.VMEM((n_pages + 1, KV_WIDTH, page), BF16)],
    )
    return pl.pallas_call(
        functools.partial(_sample_attn_kernel, n_steps=n_steps, pps=pps, n_tok=n_tok, tpad=tpad, page=page,
                          topk=topk, idx_bits=idx_bits),
        grid_spec=grid_spec,
        out_shape=jax.ShapeDtypeStruct((n, rows, KV_WIDTH), F32),
        compiler_params=_cparams(("arbitrary", "arbitrary")),
        name="sample_attn",
    )(page_table, q, qih, wib, kin_t, kn_t, vn_t, *([cik_t] * pps), *([ck_t] * pps), *([cv_t] * pps))


def _mix_out_tail(x, o, gb, u0, um1, um2, wc_ref, wo_ref, gt, gpost, aw):
    y = wc_ref[0:1, :] * um2 + wc_ref[1:2, :] * um1 + wc_ref[2:3, :] * u0
    mixed = (jnp.dot(o.astype(BF16), wo_ref[0:aw, :], preferred_element_type=F32)
             + jnp.dot((gb * y).astype(BF16), wo_ref[aw:, :], preferred_element_type=F32))
    return x + gt * (_rms(mixed) * gpost)


def _mix_out_prompt_kernel(x_ref, o_ref_in, gb_ref, u_ref, halo_ref, gt_ref, gpost_ref, wc_ref, wo_ref,
                           out_ref, ext_ref, *, tm, tiles_per_seq):
    i = pl.program_id(0)
    halo = halo_ref[...]
    ext_ref[0:SUBLANES, :] = jnp.where(i % tiles_per_seq == 0, jnp.zeros_like(halo), halo)
    ext_ref[SUBLANES:, :] = u_ref[...]
    um2 = ext_ref[SUBLANES - 2:SUBLANES - 2 + tm, :]
    um1 = ext_ref[SUBLANES - 1:SUBLANES - 1 + tm, :]
    out_ref[...] = _mix_out_tail(x_ref[...], o_ref_in[...], gb_ref[...], u_ref[...], um1, um2,
                                 wc_ref, wo_ref, gt_ref[0], gpost_ref[...], ATTN_WIDTH)


def _mix_out_prompt(x, o, gb, u, gate, g_post, w_conv, w_o, tm, tiles_per_seq):
    rows, d = x.shape
    cw = u.shape[1]
    const = lambda i: (0, 0)
    row = lambda i: (i, 0)
    hb = tm // SUBLANES
    return pl.pallas_call(
        functools.partial(_mix_out_prompt_kernel, tm=tm, tiles_per_seq=tiles_per_seq),
        grid=(rows // tm,),
        in_specs=[pl.BlockSpec((tm, d), row),
                  pl.BlockSpec((tm, ATTN_WIDTH), row),
                  pl.BlockSpec((tm, cw), row),
                  pl.BlockSpec((tm, cw), row),
                  pl.BlockSpec((SUBLANES, cw), lambda i: (jnp.maximum(i * hb - 1, 0), 0)),
                  _mod_spec(gate, tm, tiles_per_seq),
                  pl.BlockSpec((1, d), const),
                  pl.BlockSpec(w_conv.shape, const),
                  pl.BlockSpec(w_o.shape, const)],
        out_specs=pl.BlockSpec((tm, d), row),
        out_shape=jax.ShapeDtypeStruct((rows, d), F32),
        scratch_shapes=[pltpu.VMEM((tm + SUBLANES, cw), F32)],
        compiler_params=_cparams(("arbitrary",)),
        name="mix_out_prompt",
    )(x, o, gb, u, u, gate, g_post, w_conv, w_o)


def _mix_out_sample_kernel(x_ref, o_ref_in, gb_ref, um2_ref, um1_ref, u0_ref, gt_ref, gpost_ref, wc_ref, wo_ref,
                           out_ref):
    out_ref[...] = _mix_out_tail(x_ref[...], o_ref_in[...], gb_ref[...], u0_ref[...], um1_ref[...], um2_ref[...],
                                 wc_ref, wo_ref, gt_ref[0], gpost_ref[...], ATTN_WIDTH)


def _mix_out_sample(x, o, gb, ext, gate, g_post, w_conv, w_o, tm):
    rows, d = x.shape
    cw = gb.shape[1]
    const = lambda i: (0, 0)
    row = lambda i: (i, 0)
    return pl.pallas_call(
        _mix_out_sample_kernel,
        grid=(rows // tm,),
        in_specs=[pl.BlockSpec((tm, d), row),
                  pl.BlockSpec((tm, ATTN_WIDTH), row),
                  pl.BlockSpec((tm, cw), row),
                  pl.BlockSpec((tm, cw), lambda i: (i, 0)),
                  pl.BlockSpec((tm, cw), lambda i: (i + 1, 0)),
                  pl.BlockSpec((tm, cw), lambda i: (i + 2, 0)),
                  pl.BlockSpec((1, tm, d), lambda i: (0, 0, 0)),
                  pl.BlockSpec((1, d), const),
                  pl.BlockSpec(w_conv.shape, const),
                  pl.BlockSpec(w_o.shape, const)],
        out_specs=pl.BlockSpec((tm, d), row),
        out_shape=jax.ShapeDtypeStruct((rows, d), F32),
        compiler_params=_cparams(("arbitrary",)),
        name="mix_out_sample",
    )(x, o, gb, ext, ext, ext, gate, g_post, w_conv, w_o)


def _prompt_layer(x, mod, lw, tm, tq):
    b, s, d = x.shape
    conv_ch = lw["w_conv"].shape[1]
    tps = s // tm
    md = lambda j, k: mod[:, j, k][:, None, :]
    xr = x.reshape(b * s, d)
    x1 = _ffn(xr, md(0, 0), md(0, 1), md(0, 2), lw["g_pre"][0:1], lw["g_post"][0:1], lw["ffn1_w13"], lw["ffn1_w2"],
              tm, tps)
    tabs = _rope_tables(jnp.arange(s))
    q, k, v, qi, kiwi, gb, u = _mix_in(x1, md(1, 0), md(1, 1), lw["g_pre"][1:2], lw["w_in_p"], tabs, tm, tps, tps,
                                       conv_ch)
    nb = s // tq
    kt = jnp.swapaxes(k.astype(BF16).reshape(b, nb, tq, KV_WIDTH), 2, 3)
    kit = jnp.swapaxes(kiwi[:, :IDX_DIM].astype(BF16).reshape(b, nb, tq, IDX_DIM), 2, 3)
    kit = jnp.concatenate([kit, kit], axis=2)
    topk = min(TOPK_MAX, s // 4)
    o = _prompt_attn(q, qi.reshape(b, s, -1), kiwi.reshape(b, s, -1), kt,
                     v.astype(BF16).reshape(b, s, KV_WIDTH), kit, tq, topk)
    x2 = _mix_out_prompt(x1, o.reshape(b * s, -1), gb, u, md(1, 2), lw["g_post"][1:2], lw["w_conv"], lw["w_o"],
                         tm, tps)
    x3 = _ffn(x2, md(2, 0), md(2, 1), md(2, 2), lw["g_pre"][2:3], lw["g_post"][2:3], lw["ffn2_w13"], lw["ffn2_w2"],
              tm, tps)
    state = (k.reshape(b, s, N_KV_HEADS, HEAD_DIM), v.reshape(b, s, N_KV_HEADS, HEAD_DIM),
             kiwi[:, :IDX_DIM].reshape(b, s, IDX_DIM), u.reshape(b, s, conv_ch)[:, s - (CONV_K - 1):])
    return x3.reshape(b, s, d), state


def _sample_layer(x_tm, mod, lw, cache_k, cache_v, cache_idx_k, state_conv, page_table):
    n = mod.shape[0]
    t_tok = x_tm.shape[0] // n
    d = x_tm.shape[1]
    conv_ch = lw["w_conv"].shape[1]
    n_pages = page_table.shape[1]
    page = cache_k.shape[1]
    past = n_pages * page
    md = lambda j, k: mod[:, j, k][None]
    x1 = _ffn(x_tm, md(0, 0), md(0, 1), md(0, 2), lw["g_pre"][0:1], lw["g_post"][0:1], lw["ffn1_w13"],
              lw["ffn1_w2"], n, t_tok)
    pos = jnp.repeat(past + jnp.arange(t_tok), n)
    tabs = _rope_tables(pos)
    q, k, v, qi, kiwi, gb, u = _mix_in(x1, md(1, 0), md(1, 1), lw["g_pre"][1:2], lw["w_in_p"], tabs, n, t_tok, t_tok,
                                       conv_ch)
    tpad = SUBLANES
    assert t_tok <= tpad

    def seq_major(a, heads, width):
        a = a.reshape(t_tok, n, heads, width).transpose(1, 2, 0, 3)
        a = jnp.pad(a, ((0, 0), (0, 0), (0, tpad - t_tok), (0, 0)))
        return a.reshape(n, heads * tpad, width)

    q_seq = q.reshape(N_HEADS, t_tok, n, LANES).transpose(2, 1, 0, 3).reshape(n, t_tok * N_HEADS, LANES)
    qi_nat = jnp.concatenate([qi[:, h * LANES + (h % 2) * IDX_DIM:h * LANES + (h % 2 + 1) * IDX_DIM]
                              for h in range(IDX_HEADS)], axis=1)
    qih = seq_major(qi_nat, IDX_HEADS, IDX_DIM)
    wib = jnp.broadcast_to(seq_major(kiwi[:, IDX_DIM:IDX_DIM + IDX_HEADS], IDX_HEADS, 1),
                           (n, IDX_HEADS * tpad, LANES))

    def new_keys_t(a):
        a = a.reshape(t_tok, n, -1).transpose(1, 2, 0).astype(BF16)
        return jnp.pad(a, ((0, 0), (0, 0), (0, page - t_tok)))

    ck_t = cache_k.transpose(0, 2, 3, 1).reshape(-1, KV_WIDTH, page)
    cv_t = cache_v.transpose(0, 2, 3, 1).reshape(-1, KV_WIDTH, page)
    cik_t = cache_idx_k.transpose(0, 2, 1)
    topk = min(TOPK_MAX, (past + t_tok) // 4)
    pps = max(p for p in (16, 8, 4, 2, 1) if n_pages % p == 0)
    o = _sample_attn(page_table, q_seq, qih, wib, new_keys_t(kiwi[:, :IDX_DIM]), new_keys_t(k), new_keys_t(v),
                     cik_t, ck_t, cv_t, t_tok, tpad, topk, pps)
    o = o.reshape(n, t_tok, N_KV_HEADS, HEADS_PER_KV, N_KV_HEADS, HEAD_DIM)
    o = jnp.stack([o[:, :, g, :, g] for g in range(N_KV_HEADS)], axis=2)
    o_tm = o.transpose(1, 0, 2, 3, 4).reshape(t_tok * n, ATTN_WIDTH)
    ext = jnp.concatenate([state_conv.transpose(1, 0, 2).reshape(-1, conv_ch), u], axis=0)
    x2 = _mix_out_sample(x1, o_tm, gb, ext, md(1, 2), lw["g_post"][1:2], lw["w_conv"], lw["w_o"], n)
    x3 = _ffn(x2, md(2, 0), md(2, 1), md(2, 2), lw["g_pre"][2:3], lw["g_post"][2:3], lw["ffn2_w13"],
              lw["ffn2_w2"], n, t_tok)
    tm_to_seq = lambda a: a.reshape((t_tok, n) + a.shape[1:]).swapaxes(0, 1)
    state = (tm_to_seq(k).reshape(n, t_tok, N_KV_HEADS, HEAD_DIM), tm_to_seq(v).reshape(n, t_tok, N_KV_HEADS, HEAD_DIM),
             tm_to_seq(kiwi[:, :IDX_DIM]), tm_to_seq(u)[:, t_tok - (CONV_K - 1):])
    return x3, state


def _row_tile(s, target):
    t = min(target, s)
    while s % t:
        t //= 2
    return t


def kernel(x_prompt, x_sample, cache_k, cache_v, cache_idx_k, state_conv, page_table, c_prompt, c_sample, w_ada, b_ada, g_pre, g_post, ffn1_w13, ffn1_w2, w_in, w_conv, w_o, ffn2_w13, ffn2_w2):
    depth = w_ada.shape[0]
    b, s, d = x_prompt.shape
    n, t_tok, _ = x_sample.shape
    conv_ch = w_conv.shape[2]
    tm = _row_tile(s, 512)
    tq = _row_tile(s, 256)

    yp = x_prompt
    ys = x_sample.swapaxes(0, 1).reshape(t_tok * n, d)
    c_all = jnp.concatenate([c_prompt, c_sample], axis=0)
    n_c = c_all.shape[0]
    c_all = jnp.pad(c_all, ((0, -n_c % SUBLANES), (0, 0)))
    outs = [[] for _ in range(8)]
    for l in range(depth):
        lw = dict(g_pre=g_pre[l], g_post=g_post[l], w_conv=w_conv[l],
                  ffn1_w13=ffn1_w13[l].astype(BF16), ffn1_w2=ffn1_w2[l].astype(BF16),
                  ffn2_w13=ffn2_w13[l].astype(BF16), ffn2_w2=ffn2_w2[l].astype(BF16),
                  w_in_p=_pack_w_in(w_in[l], conv_ch), w_o=w_o[l].astype(BF16))
        mod = _ada_mod(c_all, w_ada[l], b_ada[l])[:n_c].reshape(n_c, N_SUB, 3, d)
        yp, st_p = _prompt_layer(yp, mod[:b], lw, tm, tq)
        ys, st_s = _sample_layer(ys, mod[b:], lw, cache_k[l], cache_v[l], cache_idx_k[l], state_conv[l], page_table)
        for lst, a in zip(outs, st_p + st_s):
            lst.append(a)
    ys = ys.reshape(t_tok, n, d).swapaxes(0, 1)
    return (yp, ys) + tuple(jnp.stack(lst) for lst in outs)
```

```python
import functools

import numpy as np
import jax
import jax.numpy as jnp
from jax import lax
from jax.experimental import pallas as pl
from jax.experimental.pallas import tpu as pltpu

F32 = jnp.float32
BF16 = jnp.bfloat16
I32 = jnp.int32

N_HEADS = 8
HEAD_DIM = 64
N_KV_HEADS = 2
HEADS_PER_KV = N_HEADS // N_KV_HEADS
ATTN_WIDTH = N_HEADS * HEAD_DIM
KV_WIDTH = N_KV_HEADS * HEAD_DIM
ROT_DIV = 4
ROT = HEAD_DIM // ROT_DIV
ROT_HALF = ROT // 2
ROPE_THETA = 500000.0
IDX_HEADS = 4
IDX_DIM = 64
IDX_SCALE = (IDX_HEADS * IDX_DIM) ** -0.5
TOPK_MAX = 256
CONV_K = 3
N_SUB = 3
RMS_EPS = 1e-6
Q_SCALE = HEAD_DIM ** -0.5 * float(np.log2(np.e))

LANES = 128
SUBLANES = 8
VMEM_LIMIT = 56 * 1024 * 1024

NEG_BIG = -1e30
F32_MAX = float(np.finfo(np.float32).max)
INT_MIN = -(2 ** 31)
INT_MAX = 2 ** 31 - 1

NT_DIMS = (((1,), (1,)), ((), ()))


def _cparams(sem):
    return pltpu.CompilerParams(dimension_semantics=sem, vmem_limit_bytes=VMEM_LIMIT)


def _silu(x):
    return x * (1.0 / (1.0 + jnp.exp(-x)))


def _rms(x):
    return x * lax.rsqrt(jnp.mean(x * x, axis=-1, keepdims=True) + RMS_EPS)


def _tile_lanes(x, reps):
    return x if reps == 1 else jnp.concatenate([x] * reps, axis=1)


def _ada_kernel(c_ref, w_ref, b_ref, o_ref):
    a = _silu(c_ref[...]).astype(BF16)
    o_ref[...] = jnp.dot(a, w_ref[...].astype(BF16), preferred_element_type=F32) + b_ref[...]


def _ada_mod(c, w_ada, b_ada):
    n, d = c.shape
    nout = w_ada.shape[1]
    tn = 9 * LANES if nout % (9 * LANES) == 0 else LANES
    return pl.pallas_call(
        _ada_kernel,
        grid=(nout // tn,),
        in_specs=[pl.BlockSpec((n, d), lambda j: (0, 0)),
                  pl.BlockSpec((d, tn), lambda j: (0, j)),
                  pl.BlockSpec((1, tn), lambda j: (0, j))],
        out_specs=pl.BlockSpec((n, tn), lambda j: (0, j)),
        out_shape=jax.ShapeDtypeStruct((n, nout), F32),
        compiler_params=_cparams(("arbitrary",)),
        name="ada_mod",
    )(c, w_ada, b_ada.reshape(1, nout))


def _ffn_kernel(x_ref, sh_ref, sc_ref, gt_ref, gpre_ref, gpost_ref, w13_ref, w2_ref, o_ref, *, d_ff, chunk):
    x = x_ref[...]
    h = _rms(x) * gpre_ref[...] * (1.0 + sc_ref[0]) + sh_ref[0]
    hb = h.astype(BF16)
    y = jnp.zeros(x.shape, F32)
    for c in range(d_ff // chunk):
        a = jnp.dot(hb, w13_ref[:, c * chunk:(c + 1) * chunk], preferred_element_type=F32)
        b = jnp.dot(hb, w13_ref[:, d_ff + c * chunk:d_ff + (c + 1) * chunk], preferred_element_type=F32)
        g = (_silu(a) * b).astype(BF16)
        y = y + jnp.dot(g, w2_ref[c * chunk:(c + 1) * chunk, :], preferred_element_type=F32)
    o_ref[...] = x + 0.5 * gt_ref[0] * (_rms(y) * gpost_ref[...])


def _mod_spec(mod, tm, tiles_per_group):
    _, r, d = mod.shape
    return pl.BlockSpec((1, r, d), lambda i: (i // tiles_per_group, 0, 0))


def _ffn(x, shift, scale, gate, g_pre, g_post, w13, w2, tm, tiles_per_group):
    rows, d = x.shape
    d_ff = w2.shape[0]
    const = lambda i: (0, 0)
    return pl.pallas_call(
        functools.partial(_ffn_kernel, d_ff=d_ff, chunk=min(512, d_ff)),
        grid=(rows // tm,),
        in_specs=[pl.BlockSpec((tm, d), lambda i: (i, 0)),
                  _mod_spec(shift, tm, tiles_per_group),
                  _mod_spec(scale, tm, tiles_per_group),
                  _mod_spec(gate, tm, tiles_per_group),
                  pl.BlockSpec((1, d), const),
                  pl.BlockSpec((1, d), const),
                  pl.BlockSpec(w13.shape, const),
                  pl.BlockSpec(w2.shape, const)],
        out_specs=pl.BlockSpec((tm, d), lambda i: (i, 0)),
        out_shape=jax.ShapeDtypeStruct((rows, d), F32),
        compiler_params=_cparams(("arbitrary",)),
        name="ffn",
    )(x, shift, scale, gate, g_pre, g_post, w13, w2)


_OFF_Q = 0
_OFF_K = _OFF_Q + ATTN_WIDTH
_OFF_V = _OFF_K + KV_WIDTH
_OFF_QI = _OFF_V + KV_WIDTH
_OFF_KIWI = _OFF_QI + IDX_HEADS * IDX_DIM
_OFF_GB = _OFF_KIWI + LANES


def _pack_w_in(w_in, conv_ch):
    splits = [ATTN_WIDTH, KV_WIDTH, KV_WIDTH, IDX_HEADS * IDX_DIM, IDX_DIM, IDX_HEADS, conv_ch, conv_ch, conv_ch]
    offs = np.cumsum([0] + splits)
    d = w_in.shape[0]
    pad = jnp.zeros((d, LANES - IDX_DIM - IDX_HEADS), w_in.dtype)
    return jnp.concatenate([w_in[:, :offs[6]], pad, w_in[:, offs[6]:]], axis=1).astype(BF16)


def _rope_tables(pos):
    inv = jnp.power(ROPE_THETA, -jnp.arange(ROT_HALF, dtype=F32) * 2.0 / ROT)
    ang = pos.astype(F32)[:, None] * inv[None, :]
    cos, sin = jnp.cos(ang), jnp.sin(ang)
    n = pos.shape[0]
    one = jnp.ones((n, HEAD_DIM - ROT), F32)
    zero_r = jnp.zeros((n, HEAD_DIM - ROT), F32)
    zero_h = jnp.zeros((n, ROT_HALF), F32)
    c = jnp.concatenate([cos, cos, one], axis=1)
    s1 = jnp.concatenate([-sin, zero_h, zero_r], axis=1)
    s2 = jnp.concatenate([zero_h, sin, zero_r], axis=1)
    rep = LANES // HEAD_DIM
    return jnp.tile(c, (1, rep)), jnp.tile(s1, (1, rep)), jnp.tile(s2, (1, rep))


def _rope128(x, c, s1, s2):
    return x * c + pltpu.roll(x, LANES - ROT_HALF, 1) * s1 + pltpu.roll(x, ROT_HALF, 1) * s2


def _mix_in_kernel(x_ref, sh_ref, sc_ref, gpre_ref, w_ref, rc_ref, rs1_ref, rs2_ref,
                   q_ref, k_ref, v_ref, qi_ref, kiwi_ref, gb_ref, u_ref, *, conv_ch):
    x = x_ref[...]
    hb = (_rms(x) * gpre_ref[...] * (1.0 + sc_ref[0]) + sh_ref[0]).astype(BF16)
    c, s1, s2 = rc_ref[...], rs1_ref[...], rs2_ref[...]

    def proj(off, width):
        return jnp.dot(hb, w_ref[:, off:off + width], preferred_element_type=F32)

    def rope_cols(z):
        return [_rope128(z[:, g * LANES:(g + 1) * LANES], c, s1, s2) for g in range(z.shape[1] // LANES)]

    lane = lax.broadcasted_iota(I32, (x.shape[0], LANES), 1)
    low = lane < HEAD_DIM

    zq = proj(_OFF_Q, ATTN_WIDTH)
    for g, r in enumerate(rope_cols(zq)):
        r = r * Q_SCALE
        swapped = pltpu.roll(r, HEAD_DIM, 1)
        for e in range(2):
            h = 2 * g + e
            kv_group = h // HEADS_PER_KV
            src = r if e == kv_group else swapped
            keep = low if kv_group == 0 else jnp.logical_not(low)
            q_ref[h] = jnp.where(keep, src, 0.0).astype(BF16)
    zk = proj(_OFF_K, KV_WIDTH)
    for g, r in enumerate(rope_cols(zk)):
        k_ref[:, g * LANES:(g + 1) * LANES] = r
    v_ref[...] = proj(_OFF_V, KV_WIDTH)
    zqi = proj(_OFF_QI, IDX_HEADS * IDX_DIM)
    for g, r in enumerate(rope_cols(zqi)):
        qi_ref[:, (2 * g) * LANES:(2 * g + 1) * LANES] = jnp.where(low, r, 0.0).astype(BF16)
        qi_ref[:, (2 * g + 1) * LANES:(2 * g + 2) * LANES] = jnp.where(low, 0.0, r).astype(BF16)
    zkw = proj(_OFF_KIWI, LANES)
    kiwi_ref[...] = jnp.where(lane < IDX_DIM, _rope128(zkw, c, s1, s2), zkw * IDX_SCALE)
    gb_ref[...] = proj(_OFF_GB, conv_ch)
    u_ref[...] = proj(_OFF_GB + conv_ch, conv_ch) * proj(_OFF_GB + 2 * conv_ch, conv_ch)


def _mix_in(x, shift, scale, g_pre, w_in_p, rope_tabs, tm, tiles_per_group, rope_tiles, conv_ch):
    rows, d = x.shape
    const = lambda i: (0, 0)
    row = lambda i: (i, 0)
    rope_spec = pl.BlockSpec((tm, LANES), lambda i: (i % rope_tiles, 0))
    assert 2 * HEAD_DIM == LANES and KV_WIDTH == LANES and 2 * IDX_DIM == LANES
    widths = [(KV_WIDTH, F32), (KV_WIDTH, F32), (IDX_HEADS * LANES, BF16),
              (LANES, F32), (conv_ch, F32), (conv_ch, F32)]
    q_spec = pl.BlockSpec((N_HEADS, tm, LANES), lambda i: (0, i, 0))
    q_shape = jax.ShapeDtypeStruct((N_HEADS, rows, LANES), BF16)
    return pl.pallas_call(
        functools.partial(_mix_in_kernel, conv_ch=conv_ch),
        grid=(rows // tm,),
        in_specs=[pl.BlockSpec((tm, d), row),
                  _mod_spec(shift, tm, tiles_per_group),
                  _mod_spec(scale, tm, tiles_per_group),
                  pl.BlockSpec((1, d), const),
                  pl.BlockSpec(w_in_p.shape, const),
                  rope_spec, rope_spec, rope_spec],
        out_specs=[q_spec] + [pl.BlockSpec((tm, w), row) for w, _ in widths],
        out_shape=[q_shape] + [jax.ShapeDtypeStruct((rows, w), dt) for w, dt in widths],
        compiler_params=_cparams(("arbitrary",)),
        name="mix_in",
    )(x, shift, scale, g_pre, w_in_p, *rope_tabs)


def _key_to_float(u):
    o = u ^ INT_MIN
    bits = jnp.where(o >= 0, o, o ^ INT_MAX)
    return lax.bitcast_convert_type(bits, F32)


def _count(sc_ref, nkb, r0, rg, tk, pred):
    rows = slice(r0, r0 + rg)

    def body(kb, acc):
        for j in range(tk // LANES):
            x = sc_ref[kb, rows, j * LANES:(j + 1) * LANES]
            kpos = kb * tk + j * LANES + lax.broadcasted_iota(I32, (rg, LANES), 1)
            acc = acc + jnp.where(pred(x, kpos, rows), 1.0, 0.0)
        return acc

    group = 4

    def group_body(i, acc):
        for g in range(group):
            acc = body(group * i + g, acc)
        return acc

    acc = lax.fori_loop(0, nkb // group, group_body, jnp.zeros((rg, LANES), F32))
    return lax.fori_loop(group * (nkb // group), nkb, body, acc)


def _select_rows(cnt, rg, topk, idx_bits, row_ok, tie_fn=None):
    def bit_body(i, carry):
        key, c_ge = carry
        cand = key | jnp.left_shift(jnp.int32(1), 31 - i)
        t = _key_to_float(cand)
        c = cnt(lambda x, kp, rs: x >= t[rs])
        ok = c >= topk
        return jnp.where(ok, cand, key), jnp.where(ok, c, c_ge)

    key, c_ge = lax.fori_loop(0, 32, bit_body, (jnp.zeros((rg, LANES), I32), jnp.zeros((rg, LANES), F32)))
    thr = _key_to_float(key)
    enough = thr >= -F32_MAX
    thr = jnp.where(enough, thr, -F32_MAX)
    c_gt = cnt(lambda x, kp, rs: x > thr[rs])
    need = topk - c_gt
    excess = jnp.max(jnp.where((c_ge > topk) & enough & row_ok, 1.0, 0.0)) > 0.5

    def tie_search():
        if tie_fn is not None:
            return tie_fn(thr, need)

        def jbody(i, p):
            cand = p | jnp.left_shift(jnp.int32(1), idx_bits - 1 - i)
            c = cnt(lambda x, kp, rs: (x == thr[rs]) & (kp < cand[rs]))
            return jnp.where(c < need, cand, p)
        return lax.fori_loop(0, idx_bits, jbody, jnp.zeros((rg, LANES), I32))

    jmax = lax.cond(excess, tie_search, lambda: jnp.full((rg, LANES), INT_MAX, I32))
    return thr, jmax


def _tie_bound_blocked(sc_ref, cnt_ref, nkb, r0, rg, tk, thr, need):
    reps = tk // LANES
    thr_t = _tile_lanes(thr, reps)

    group = 4

    def count_body(i, carry):
        for g in range(group):
            kb = jnp.minimum(group * i + g, nkb - 1)
            eq = jnp.where(sc_ref[kb, r0:r0 + rg, :] == thr_t, 1.0, 0.0)
            c = sum(eq[:, j * LANES:(j + 1) * LANES] for j in range(reps))
            cnt_ref[kb] = jnp.broadcast_to(jnp.sum(c, axis=1, keepdims=True), (rg, LANES))
        return carry

    lax.fori_loop(0, (nkb + group - 1) // group, count_body, 0)

    def locate_body(kb, carry):
        run, kstar, before = carry
        new_run = run + cnt_ref[kb]
        hit = (run < need) & (new_run >= need)
        return new_run, jnp.where(hit, kb, kstar), jnp.where(hit, run, before)

    zeros = jnp.zeros((rg, LANES), F32)
    _, kstar, before = lax.fori_loop(0, nkb, locate_body, (zeros, jnp.full((rg, LANES), -1, I32), zeros))
    kstar_t = _tile_lanes(kstar, reps)

    def gather_body(kb, own):
        eq = jnp.where(sc_ref[kb, r0:r0 + rg, :] == thr_t, 1.0, 0.0)
        return jnp.where(kstar_t == kb, eq, own)

    own = lax.fori_loop(0, nkb, gather_body, jnp.zeros((rg, tk), F32))
    need_in = need - before
    col = lax.broadcasted_iota(I32, (rg, tk), 1)
    p = jnp.zeros((rg, LANES), I32)
    for b in reversed(range(int(tk - 1).bit_length())):
        cand = p | (1 << b)
        c = jnp.sum(jnp.where(col < _tile_lanes(cand, reps), own, 0.0), axis=1, keepdims=True)
        p = jnp.where(c < need_in, cand, p)
    return jnp.where(kstar >= 0, kstar * tk + p, INT_MAX)


def _select(sc_ref, cnt_ref, t_ref, j_ref, nkb, rows, rg, tk, topk, idx_bits):
    groups = range(0, rows, rg)

    def cnt(pred):
        acc = jnp.concatenate([_count(sc_ref, nkb, r0, rg, tk, pred) for r0 in groups], axis=0)
        return jnp.broadcast_to(jnp.sum(acc, axis=1, keepdims=True), (rows, LANES))

    def tie_fn(thr, need):
        return jnp.concatenate(
            [_tie_bound_blocked(sc_ref, cnt_ref, nkb, r0, rg, tk, thr[r0:r0 + rg], need[r0:r0 + rg]) for r0 in groups],
            axis=0)

    thr, jmax = _select_rows(cnt, rows, topk, idx_bits, True, tie_fn)
    t_ref[...] = thr
    j_ref[...] = jmax


def _sel_mask(scb, kpos, thr, jmax):
    return (scb > thr) | ((scb == thr) & (kpos <= jmax))


def _prompt_attn_kernel(q_ref, qi_ref, kiwi_ref, kt_ref, v_ref, kit_ref, o_ref,
                        sc_ref, cnt_ref, wib_ref, m_ref, l_ref, acc_ref, t_ref, j_ref,
                        *, tq, topk, idx_bits):
    tk = tq
    reps = tk // LANES
    i = pl.program_id(1)
    nkb = i + 1

    kiwi = kiwi_ref[...]
    for h in range(IDX_HEADS):
        wib_ref[h] = jnp.broadcast_to(kiwi[:, IDX_DIM + h:IDX_DIM + h + 1], (tq, LANES))

    def score_body(kb, carry):
        kit = kit_ref[kb]
        acc = None
        for h in range(IDX_HEADS):
            s = jnp.dot(qi_ref[:, h * LANES:(h + 1) * LANES], kit, preferred_element_type=F32)
            t = jnp.maximum(s, 0.0) * _tile_lanes(wib_ref[h], reps)
            acc = t if acc is None else acc + t
        sc_ref[kb] = acc
        return carry

    lax.fori_loop(0, nkb, score_body, 0)
    row = lax.broadcasted_iota(I32, (tq, tk), 0)
    col = lax.broadcasted_iota(I32, (tq, tk), 1)
    sc_ref[i] = jnp.where(col <= row, sc_ref[i], -jnp.inf)

    _select(sc_ref, cnt_ref, t_ref, j_ref, nkb, tq, min(128, tq), tk, topk, idx_bits)

    def bias_body(kb, carry):
        thr = _tile_lanes(t_ref[...], reps)
        jmax = _tile_lanes(j_ref[...], reps)
        sc_ref[kb] = jnp.where(_sel_mask(sc_ref[kb], kb * tk + col, thr, jmax), 0.0, NEG_BIG)
        return carry

    lax.fori_loop(0, nkb, bias_body, 0)

    m_ref[...] = jnp.full(m_ref.shape, NEG_BIG, F32)
    l_ref[...] = jnp.zeros(l_ref.shape, F32)
    acc_ref[...] = jnp.zeros(acc_ref.shape, F32)

    def attn_body(kb, carry):
        kt = kt_ref[kb]
        vb = v_ref[pl.ds(pl.multiple_of(kb * tk, tk), tk), :]
        bias = sc_ref[kb]
        for h in range(N_HEADS):
            s = jnp.dot(q_ref[h], kt, preferred_element_type=F32) + bias
            m_prev = m_ref[h]
            m_new = jnp.maximum(m_prev, jnp.max(s, axis=1, keepdims=True))
            alpha = jnp.exp2(m_prev - m_new)
            p = jnp.exp2(s - _tile_lanes(m_new, reps))
            l_ref[h] = alpha * l_ref[h] + sum(p[:, j * LANES:(j + 1) * LANES] for j in range(reps))
            acc_ref[h] = alpha * acc_ref[h] + jnp.dot(p.astype(BF16), vb, preferred_element_type=F32)
            m_ref[h] = m_new
        return carry

    lax.fori_loop(0, nkb, attn_body, 0)

    for h in range(N_HEADS):
        g = h // HEADS_PER_KV
        o = acc_ref[h] / jnp.sum(l_ref[h], axis=1, keepdims=True)
        o_ref[:, h * HEAD_DIM:(h + 1) * HEAD_DIM] = o[:, g * HEAD_DIM:(g + 1) * HEAD_DIM]


def _prompt_attn(q, qi, kiwi, kt, v, kit, tq, topk):
    b, s, _ = qi.shape
    nb = s // tq
    idx_bits = int(s).bit_length()
    blk = lambda w: pl.BlockSpec((None, tq, w), lambda bi, i: (bi, i, 0))
    full = lambda shp: pl.BlockSpec((None,) + shp, lambda bi, i: (bi,) + (0,) * len(shp),
                                    pipeline_mode=pl.Buffered(1))
    return pl.pallas_call(
        functools.partial(_prompt_attn_kernel, tq=tq, topk=topk, idx_bits=idx_bits),
        grid=(b, nb),
        in_specs=[pl.BlockSpec((N_HEADS, tq, LANES), lambda bi, i: (0, bi * nb + i, 0)),
                  blk(IDX_HEADS * LANES), blk(LANES),
                  full((nb, KV_WIDTH, tq)), full((s, KV_WIDTH)), full((nb, 2 * IDX_DIM, tq))],
        out_specs=blk(ATTN_WIDTH),
        out_shape=jax.ShapeDtypeStruct((b, s, ATTN_WIDTH), F32),
        scratch_shapes=[pltpu.VMEM((nb, tq, tq), F32),
                        pltpu.VMEM((nb, min(128, tq), LANES), F32),
                        pltpu.VMEM((IDX_HEADS, tq, LANES), F32),
                        pltpu.VMEM((N_HEADS, tq, LANES), F32),
                        pltpu.VMEM((N_HEADS, tq, LANES), F32),
                        pltpu.VMEM((N_HEADS, tq, KV_WIDTH), F32),
                        pltpu.VMEM((tq, LANES), F32),
                        pltpu.VMEM((tq, LANES), I32)],
        compiler_params=_cparams(("arbitrary", "arbitrary")),
        name="prompt_attn",
    )(q, qi, kiwi, kt, v, kit)


def _sample_attn_kernel(pt_ref, q_ref, qih_ref, wib_ref, kin_ref, kn_ref, vn_ref, *rest,
                        n_steps, pps, n_tok, tpad, page, topk, idx_bits):
    cik_refs, ck_refs, cv_refs = rest[:pps], rest[pps:2 * pps], rest[2 * pps:3 * pps]
    o_ref, sc_ref, sa_ref, vs_ref = rest[3 * pps:]
    step = pl.program_id(1)
    n_pages = n_steps * pps
    nkb = n_pages + 1
    qih = qih_ref[...]
    wib = wib_ref[...]
    q = q_ref[...]

    def idx_score(s):
        acc = None
        for h in range(IDX_HEADS):
            t = jnp.maximum(s[h * tpad:(h + 1) * tpad], 0.0) * wib[h * tpad:(h + 1) * tpad]
            acc = t if acc is None else acc + t
        return acc

    for j in range(pps):
        pg = step * pps + j
        sc_ref[pg] = idx_score(jnp.dot(qih, cik_refs[j][...].astype(BF16), preferred_element_type=F32))
        sa_ref[pg] = jnp.dot(q, ck_refs[j][...].astype(BF16), preferred_element_type=F32)
        vs_ref[pg] = cv_refs[j][...].astype(BF16)

    @pl.when(step == n_steps - 1)
    def _():
        s_new = idx_score(jnp.dot(qih, kin_ref[...], preferred_element_type=F32))
        tok = lax.broadcasted_iota(I32, (tpad, page), 0)
        col = lax.broadcasted_iota(I32, (tpad, page), 1)
        sc_ref[n_pages] = jnp.where((col <= tok) & (col < n_tok), s_new, -jnp.inf)
        sa_ref[n_pages] = jnp.dot(q, kn_ref[...], preferred_element_type=F32)
        vs_ref[n_pages] = vn_ref[...]

        kpos = (lax.broadcasted_iota(I32, (nkb, tpad, page), 0) * page
                + lax.broadcasted_iota(I32, (nkb, tpad, page), 2))

        def cnt(pred):
            w = jnp.where(pred(sc_ref[...], kpos, slice(None)), 1.0, 0.0)
            step = -(-nkb // SUBLANES)
            c = sum(jnp.sum(w[a:a + step], axis=0) for a in range(0, nkb, step))
            return jnp.broadcast_to(jnp.sum(c, axis=1, keepdims=True), c.shape)

        thr, jmax = _select_rows(cnt, tpad, topk, idx_bits, tok[:, :LANES] < n_tok)
        sc_ref[...] = jnp.where(_sel_mask(sc_ref[...], kpos, thr[None], jmax[None]), 0.0, NEG_BIG)

        heads = N_HEADS
        inv_l = []
        for t in range(n_tok):
            s = sa_ref[:, t * heads:(t + 1) * heads, :] + sc_ref[:, t:t + 1, :]
            m = jnp.max(jnp.max(s, axis=0), axis=1, keepdims=True)
            pr = jnp.exp2(s - m[None])
            sa_ref[:, t * heads:(t + 1) * heads, :] = pr
            l = jnp.sum(jnp.sum(pr, axis=0), axis=1, keepdims=True)
            inv_l.append(jnp.broadcast_to(1.0 / l, (heads, KV_WIDTH)))

        lanes_acc = next(g for g in (5, 4, 3, 2, 1) if nkb % g == 0)

        def pv_body(i, accs):
            return tuple(
                a + lax.dot_general(sa_ref[lanes_acc * i + g].astype(BF16), vs_ref[lanes_acc * i + g], NT_DIMS,
                                    preferred_element_type=F32)
                for g, a in enumerate(accs))

        zero = jnp.zeros((n_tok * heads, KV_WIDTH), F32)
        accs = lax.fori_loop(0, nkb // lanes_acc, pv_body, (zero,) * lanes_acc)
        o_ref[...] = sum(accs) * jnp.concatenate(inv_l, axis=0)


def _sample_attn(page_table, q, qih, wib, kin_t, kn_t, vn_t, cik_t, ck_t, cv_t, n_tok, tpad, topk, pps):
    n, n_pages = page_table.shape
    page = ck_t.shape[2]
    assert page == LANES and KV_WIDTH == LANES and n_pages % pps == 0
    n_steps = n_pages // pps
    idx_bits = int((n_pages + 1) * page).bit_length()
    rows = n_tok * N_HEADS
    seq = lambda r, w: pl.BlockSpec((None, r, w), lambda b, p, pt: (b, 0, 0))

    def paged(r, j):
        return pl.BlockSpec((None, r, page), lambda b, p, pt: (pt[b, p * pps + j], 0, 0))

    grid_spec = pltpu.PrefetchScalarGridSpec(
        num_scalar_prefetch=1,
        grid=(n, n_steps),
        in_specs=([seq(rows, KV_WIDTH), seq(IDX_HEADS * tpad, IDX_DIM), seq(IDX_HEADS * tpad, LANES),
                   seq(IDX_DIM, page), seq(KV_WIDTH, page), seq(KV_WIDTH, page)]
                  + [paged(IDX_DIM, j) for j in range(pps)]
                  + [paged(KV_WIDTH, j) for j in range(pps)]
                  + [paged(KV_WIDTH, j) for j in range(pps)]),
        out_specs=seq(rows, KV_WIDTH),
        scratch_shapes=[pltpu.VMEM((n_pages + 1, tpad, page), F32),
                        pltpu.VMEM((n_pages + 1, rows, page), F32),
                        pltpu.VMEM((n_pages + 1, KV_WIDTH, page), BF16)],
    )
    return pl.pallas_call(
        functools.partial(_sample_attn_kernel, n_steps=n_steps, pps=pps, n_tok=n_tok, tpad=tpad, page=page,
                          topk=topk, idx_bits=idx_bits),
        grid_spec=grid_spec,
        out_shape=jax.ShapeDtypeStruct((n, rows, KV_WIDTH), F32),
        compiler_params=_cparams(("arbitrary", "arbitrary")),
        name="sample_attn",
    )(page_table, q, qih, wib, kin_t, kn_t, vn_t, *([cik_t] * pps), *([ck_t] * pps), *([cv_t] * pps))


def _mix_out_tail(x, o, gb, u0, um1, um2, wc_ref, wo_ref, gt, gpost, aw):
    y = wc_ref[0:1, :] * um2 + wc_ref[1:2, :] * um1 + wc_ref[2:3, :] * u0
    mixed = (jnp.dot(o.astype(BF16), wo_ref[0:aw, :], preferred_element_type=F32)
             + jnp.dot((gb * y).astype(BF16), wo_ref[aw:, :], preferred_element_type=F32))
    return x + gt * (_rms(mixed) * gpost)


def _mix_out_prompt_kernel(x_ref, o_ref_in, gb_ref, u_ref, halo_ref, gt_ref, gpost_ref, wc_ref, wo_ref,
                           out_ref, ext_ref, *, tm, tiles_per_seq):
    i = pl.program_id(0)
    halo = halo_ref[...]
    ext_ref[0:SUBLANES, :] = jnp.where(i % tiles_per_seq == 0, jnp.zeros_like(halo), halo)
    ext_ref[SUBLANES:, :] = u_ref[...]
    um2 = ext_ref[SUBLANES - 2:SUBLANES - 2 + tm, :]
    um1 = ext_ref[SUBLANES - 1:SUBLANES - 1 + tm, :]
    out_ref[...] = _mix_out_tail(x_ref[...], o_ref_in[...], gb_ref[...], u_ref[...], um1, um2,
                                 wc_ref, wo_ref, gt_ref[0], gpost_ref[...], ATTN_WIDTH)


def _mix_out_prompt(x, o, gb, u, gate, g_post, w_conv, w_o, tm, tiles_per_seq):
    rows, d = x.shape
    cw = u.shape[1]
    const = lambda i: (0, 0)
    row = lambda i: (i, 0)
    hb = tm // SUBLANES
    return pl.pallas_call(
        functools.partial(_mix_out_prompt_kernel, tm=tm, tiles_per_seq=tiles_per_seq),
        grid=(rows // tm,),
        in_specs=[pl.BlockSpec((tm, d), row),
                  pl.BlockSpec((tm, ATTN_WIDTH), row),
                  pl.BlockSpec((tm, cw), row),
                  pl.BlockSpec((tm, cw), row),
                  pl.BlockSpec((SUBLANES, cw), lambda i: (jnp.maximum(i * hb - 1, 0), 0)),
                  _mod_spec(gate, tm, tiles_per_seq),
                  pl.BlockSpec((1, d), const),
                  pl.BlockSpec(w_conv.shape, const),
                  pl.BlockSpec(w_o.shape, const)],
        out_specs=pl.BlockSpec((tm, d), row),
        out_shape=jax.ShapeDtypeStruct((rows, d), F32),
        scratch_shapes=[pltpu.VMEM((tm + SUBLANES, cw), F32)],
        compiler_params=_cparams(("arbitrary",)),
        name="mix_out_prompt",
    )(x, o, gb, u, u, gate, g_post, w_conv, w_o)


def _mix_out_sample_kernel(x_ref, o_ref_in, gb_ref, um2_ref, um1_ref, u0_ref, gt_ref, gpost_ref, wc_ref, wo_ref,
                           out_ref):
    out_ref[...] = _mix_out_tail(x_ref[...], o_ref_in[...], gb_ref[...], u0_ref[...], um1_ref[...], um2_ref[...],
                                 wc_ref, wo_ref, gt_ref[0], gpost_ref[...], ATTN_WIDTH)


def _mix_out_sample(x, o, gb, ext, gate, g_post, w_conv, w_o, tm):
    rows, d = x.shape
    cw = gb.shape[1]
    const = lambda i: (0, 0)
    row = lambda i: (i, 0)
    return pl.pallas_call(
        _mix_out_sample_kernel,
        grid=(rows // tm,),
        in_specs=[pl.BlockSpec((tm, d), row),
                  pl.BlockSpec((tm, ATTN_WIDTH), row),
                  pl.BlockSpec((tm, cw), row),
                  pl.BlockSpec((tm, cw), lambda i: (i, 0)),
                  pl.BlockSpec((tm, cw), lambda i: (i + 1, 0)),
                  pl.BlockSpec((tm, cw), lambda i: (i + 2, 0)),
                  pl.BlockSpec((1, tm, d), lambda i: (0, 0, 0)),
                  pl.BlockSpec((1, d), const),
                  pl.BlockSpec(w_conv.shape, const),
                  pl.BlockSpec(w_o.shape, const)],
        out_specs=pl.BlockSpec((tm, d), row),
        out_shape=jax.ShapeDtypeStruct((rows, d), F32),
        compiler_params=_cparams(("arbitrary",)),
        name="mix_out_sample",
    )(x, o, gb, ext, ext, ext, gate, g_post, w_conv, w_o)


def _prompt_layer(x, mod, lw, tm, tq):
    b, s, d = x.shape
    conv_ch = lw["w_conv"].shape[1]
    tps = s // tm
    md = lambda j, k: mod[:, j, k][:, None, :]
    xr = x.reshape(b * s, d)
    x1 = _ffn(xr, md(0, 0), md(0, 1), md(0, 2), lw["g_pre"][0:1], lw["g_post"][0:1], lw["ffn1_w13"], lw["ffn1_w2"],
              tm, tps)
    tabs = _rope_tables(jnp.arange(s))
    q, k, v, qi, kiwi, gb, u = _mix_in(x1, md(1, 0), md(1, 1), lw["g_pre"][1:2], lw["w_in_p"], tabs, tm, tps, tps,
                                       conv_ch)
    nb = s // tq
    kt = jnp.swapaxes(k.astype(BF16).reshape(b, nb, tq, KV_WIDTH), 2, 3)
    kit = jnp.swapaxes(kiwi[:, :IDX_DIM].astype(BF16).reshape(b, nb, tq, IDX_DIM), 2, 3)
    kit = jnp.concatenate([kit, kit], axis=2)
    topk = min(TOPK_MAX, s // 4)
    o = _prompt_attn(q, qi.reshape(b, s, -1), kiwi.reshape(b, s, -1), kt,
                     v.astype(BF16).reshape(b, s, KV_WIDTH), kit, tq, topk)
    x2 = _mix_out_prompt(x1, o.reshape(b * s, -1), gb, u, md(1, 2), lw["g_post"][1:2], lw["w_conv"], lw["w_o"],
                         tm, tps)
    x3 = _ffn(x2, md(2, 0), md(2, 1), md(2, 2), lw["g_pre"][2:3], lw["g_post"][2:3], lw["ffn2_w13"], lw["ffn2_w2"],
              tm, tps)
    state = (k.reshape(b, s, N_KV_HEADS, HEAD_DIM), v.reshape(b, s, N_KV_HEADS, HEAD_DIM),
             kiwi[:, :IDX_DIM].reshape(b, s, IDX_DIM), u.reshape(b, s, conv_ch)[:, s - (CONV_K - 1):])
    return x3.reshape(b, s, d), state


def _sample_layer(x_tm, mod, lw, cache_k, cache_v, cache_idx_k, state_conv, page_table):
    n = mod.shape[0]
    t_tok = x_tm.shape[0] // n
    d = x_tm.shape[1]
    conv_ch = lw["w_conv"].shape[1]
    n_pages = page_table.shape[1]
    page = cache_k.shape[1]
    past = n_pages * page
    md = lambda j, k: mod[:, j, k][None]
    x1 = _ffn(x_tm, md(0, 0), md(0, 1), md(0, 2), lw["g_pre"][0:1], lw["g_post"][0:1], lw["ffn1_w13"],
              lw["ffn1_w2"], n, t_tok)
    pos = jnp.repeat(past + jnp.arange(t_tok), n)
    tabs = _rope_tables(pos)
    q, k, v, qi, kiwi, gb, u = _mix_in(x1, md(1, 0), md(1, 1), lw["g_pre"][1:2], lw["w_in_p"], tabs, n, t_tok, t_tok,
                                       conv_ch)
    tpad = SUBLANES
    assert t_tok <= tpad

    def seq_major(a, heads, width):
        a = a.reshape(t_tok, n, heads, width).transpose(1, 2, 0, 3)
        a = jnp.pad(a, ((0, 0), (0, 0), (0, tpad - t_tok), (0, 0)))
        return a.reshape(n, heads * tpad, width)

    q_seq = q.reshape(N_HEADS, t_tok, n, LANES).transpose(2, 1, 0, 3).reshape(n, t_tok * N_HEADS, LANES)
    qi_nat = jnp.concatenate([qi[:, h * LANES + (h % 2) * IDX_DIM:h * LANES + (h % 2 + 1) * IDX_DIM]
                              for h in range(IDX_HEADS)], axis=1)
    qih = seq_major(qi_nat, IDX_HEADS, IDX_DIM)
    wib = jnp.broadcast_to(seq_major(kiwi[:, IDX_DIM:IDX_DIM + IDX_HEADS], IDX_HEADS, 1),
                           (n, IDX_HEADS * tpad, LANES))

    def new_keys_t(a):
        a = a.reshape(t_tok, n, -1).transpose(1, 2, 0).astype(BF16)
        return jnp.pad(a, ((0, 0), (0, 0), (0, page - t_tok)))

    ck_t = cache_k.transpose(0, 2, 3, 1).reshape(-1, KV_WIDTH, page)
    cv_t = cache_v.transpose(0, 2, 3, 1).reshape(-1, KV_WIDTH, page)
    cik_t = cache_idx_k.transpose(0, 2, 1)
    topk = min(TOPK_MAX, (past + t_tok) // 4)
    pps = max(p for p in (16, 8, 4, 2, 1) if n_pages % p == 0)
    o = _sample_attn(page_table, q_seq, qih, wib, new_keys_t(kiwi[:, :IDX_DIM]), new_keys_t(k), new_keys_t(v),
                     cik_t, ck_t, cv_t, t_tok, tpad, topk, pps)
    o = o.reshape(n, t_tok, N_KV_HEADS, HEADS_PER_KV, N_KV_HEADS, HEAD_DIM)
    o = jnp.stack([o[:, :, g, :, g] for g in range(N_KV_HEADS)], axis=2)
    o_tm = o.transpose(1, 0, 2, 3, 4).reshape(t_tok * n, ATTN_WIDTH)
    ext = jnp.concatenate([state_conv.transpose(1, 0, 2).reshape(-1, conv_ch), u], axis=0)
    x2 = _mix_out_sample(x1, o_tm, gb, ext, md(1, 2), lw["g_post"][1:2], lw["w_conv"], lw["w_o"], n)
    x3 = _ffn(x2, md(2, 0), md(2, 1), md(2, 2), lw["g_pre"][2:3], lw["g_post"][2:3], lw["ffn2_w13"],
              lw["ffn2_w2"], n, t_tok)
    tm_to_seq = lambda a: a.reshape((t_tok, n) + a.shape[1:]).swapaxes(0, 1)
    state = (tm_to_seq(k).reshape(n, t_tok, N_KV_HEADS, HEAD_DIM), tm_to_seq(v).reshape(n, t_tok, N_KV_HEADS, HEAD_DIM),
             tm_to_seq(kiwi[:, :IDX_DIM]), tm_to_seq(u)[:, t_tok - (CONV_K - 1):])
    return x3, state


def _row_tile(s, target):
    t = min(target, s)
    while s % t:
        t //= 2
    return t


def kernel(x_prompt, x_sample, cache_k, cache_v, cache_idx_k, state_conv, page_table, c_prompt, c_sample, w_ada, b_ada, g_pre, g_post, ffn1_w13, ffn1_w2, w_in, w_conv, w_o, ffn2_w13, ffn2_w2):
    depth = w_ada.shape[0]
    b, s, d = x_prompt.shape
    n, t_tok, _ = x_sample.shape
    conv_ch = w_conv.shape[2]
    tm = _row_tile(s, 512)
    tq = _row_tile(s, 256)

    yp = x_prompt
    ys = x_sample.swapaxes(0, 1).reshape(t_tok * n, d)
    c_all = jnp.concatenate([c_prompt, c_sample], axis=0)
    n_c = c_all.shape[0]
    c_all = jnp.pad(c_all, ((0, -n_c % SUBLANES), (0, 0)))
    outs = [[] for _ in range(8)]
    for l in range(depth):
        lw = dict(g_pre=g_pre[l], g_post=g_post[l], w_conv=w_conv[l],
                  ffn1_w13=ffn1_w13[l].astype(BF16), ffn1_w2=ffn1_w2[l].astype(BF16),
                  ffn2_w13=ffn2_w13[l].astype(BF16), ffn2_w2=ffn2_w2[l].astype(BF16),
                  w_in_p=_pack_w_in(w_in[l], conv_ch), w_o=w_o[l].astype(BF16))
        mod = _ada_mod(c_all, w_ada[l], b_ada[l])[:n_c].reshape(n_c, N_SUB, 3, d)
        yp, st_p = _prompt_layer(yp, mod[:b], lw, tm, tq)
        ys, st_s = _sample_layer(ys, mod[b:], lw, cache_k[l], cache_v[l], cache_idx_k[l], state_conv[l], page_table)
        for lst, a in zip(outs, st_p + st_s):
            lst.append(a)
    ys = ys.reshape(t_tok, n, d).swapaxes(0, 1)
    return (yp, ys) + tuple(jnp.stack(lst) for lst in outs)
```

```python
import functools

import numpy as np
import jax
import jax.numpy as jnp
from jax import lax
from jax.experimental import pallas as pl
from jax.experimental.pallas import tpu as pltpu

F32 = jnp.float32
BF16 = jnp.bfloat16
I32 = jnp.int32

N_HEADS = 8
HEAD_DIM = 64
N_KV_HEADS = 2
HEADS_PER_KV = N_HEADS // N_KV_HEADS
ATTN_WIDTH = N_HEADS * HEAD_DIM
KV_WIDTH = N_KV_HEADS * HEAD_DIM
ROT_DIV = 4
ROT = HEAD_DIM // ROT_DIV
ROT_HALF = ROT // 2
ROPE_THETA = 500000.0
IDX_HEADS = 4
IDX_DIM = 64
IDX_SCALE = (IDX_HEADS * IDX_DIM) ** -0.5
TOPK_MAX = 256
CONV_K = 3
N_SUB = 3
RMS_EPS = 1e-6
Q_SCALE = HEAD_DIM ** -0.5 * float(np.log2(np.e))

LANES = 128
SUBLANES = 8
VMEM_LIMIT = 56 * 1024 * 1024

NEG_BIG = -1e30
F32_MAX = float(np.finfo(np.float32).max)
INT_MIN = -(2 ** 31)
INT_MAX = 2 ** 31 - 1

NT_DIMS = (((1,), (1,)), ((), ()))


def _cparams(sem):
    return pltpu.CompilerParams(dimension_semantics=sem, vmem_limit_bytes=VMEM_LIMIT)


def _silu(x):
    return x * (1.0 / (1.0 + jnp.exp(-x)))


def _rms(x):
    return x * lax.rsqrt(jnp.mean(x * x, axis=-1, keepdims=True) + RMS_EPS)


def _tile_lanes(x, reps):
    return x if reps == 1 else jnp.concatenate([x] * reps, axis=1)


def _ada_kernel(c_ref, w_ref, b_ref, o_ref):
    a = _silu(c_ref[...]).astype(BF16)
    o_ref[...] = jnp.dot(a, w_ref[...].astype(BF16), preferred_element_type=F32) + b_ref[...]


def _ada_mod(c, w_ada, b_ada):
    n, d = c.shape
    nout = w_ada.shape[1]
    tn = 9 * LANES if nout % (9 * LANES) == 0 else LANES
    return pl.pallas_call(
        _ada_kernel,
        grid=(nout // tn,),
        in_specs=[pl.BlockSpec((n, d), lambda j: (0, 0)),
                  pl.BlockSpec((d, tn), lambda j: (0, j)),
                  pl.BlockSpec((1, tn), lambda j: (0, j))],
        out_specs=pl.BlockSpec((n, tn), lambda j: (0, j)),
        out_shape=jax.ShapeDtypeStruct((n, nout), F32),
        compiler_params=_cparams(("arbitrary",)),
        name="ada_mod",
    )(c, w_ada, b_ada.reshape(1, nout))


def _ffn_kernel(x_ref, sh_ref, sc_ref, gt_ref, gpre_ref, gpost_ref, w13_ref, w2_ref, o_ref, *, d_ff, chunk):
    x = x_ref[...]
    h = _rms(x) * gpre_ref[...] * (1.0 + sc_ref[0]) + sh_ref[0]
    hb = h.astype(BF16)
    y = jnp.zeros(x.shape, F32)
    for c in range(d_ff // chunk):
        a = jnp.dot(hb, w13_ref[:, c * chunk:(c + 1) * chunk], preferred_element_type=F32)
        b = jnp.dot(hb, w13_ref[:, d_ff + c * chunk:d_ff + (c + 1) * chunk], preferred_element_type=F32)
        g = (_silu(a) * b).astype(BF16)
        y = y + jnp.dot(g, w2_ref[c * chunk:(c + 1) * chunk, :], preferred_element_type=F32)
    o_ref[...] = x + 0.5 * gt_ref[0] * (_rms(y) * gpost_ref[...])


def _mod_spec(mod, tm, tiles_per_group):
    _, r, d = mod.shape
    return pl.BlockSpec((1, r, d), lambda i: (i // tiles_per_group, 0, 0))


def _ffn(x, shift, scale, gate, g_pre, g_post, w13, w2, tm, tiles_per_group):
    rows, d = x.shape
    d_ff = w2.shape[0]
    const = lambda i: (0, 0)
    return pl.pallas_call(
        functools.partial(_ffn_kernel, d_ff=d_ff, chunk=min(512, d_ff)),
        grid=(rows // tm,),
        in_specs=[pl.BlockSpec((tm, d), lambda i: (i, 0)),
                  _mod_spec(shift, tm, tiles_per_group),
                  _mod_spec(scale, tm, tiles_per_group),
                  _mod_spec(gate, tm, tiles_per_group),
                  pl.BlockSpec((1, d), const),
                  pl.BlockSpec((1, d), const),
                  pl.BlockSpec(w13.shape, const),
                  pl.BlockSpec(w2.shape, const)],
        out_specs=pl.BlockSpec((tm, d), lambda i: (i, 0)),
        out_shape=jax.ShapeDtypeStruct((rows, d), F32),
        compiler_params=_cparams(("arbitrary",)),
        name="ffn",
    )(x, shift, scale, gate, g_pre, g_post, w13, w2)


_OFF_Q = 0
_OFF_K = _OFF_Q + ATTN_WIDTH
_OFF_V = _OFF_K + KV_WIDTH
_OFF_QI = _OFF_V + KV_WIDTH
_OFF_KIWI = _OFF_QI + IDX_HEADS * IDX_DIM
_OFF_GB = _OFF_KIWI + LANES


def _pack_w_in(w_in, conv_ch):
    splits = [ATTN_WIDTH, KV_WIDTH, KV_WIDTH, IDX_HEADS * IDX_DIM, IDX_DIM, IDX_HEADS, conv_ch, conv_ch, conv_ch]
    offs = np.cumsum([0] + splits)
    d = w_in.shape[0]
    pad = jnp.zeros((d, LANES - IDX_DIM - IDX_HEADS), w_in.dtype)
    return jnp.concatenate([w_in[:, :offs[6]], pad, w_in[:, offs[6]:]], axis=1).astype(BF16)


def _rope_tables(pos):
    inv = jnp.power(ROPE_THETA, -jnp.arange(ROT_HALF, dtype=F32) * 2.0 / ROT)
    ang = pos.astype(F32)[:, None] * inv[None, :]
    cos, sin = jnp.cos(ang), jnp.sin(ang)
    n = pos.shape[0]
    one = jnp.ones((n, HEAD_DIM - ROT), F32)
    zero_r = jnp.zeros((n, HEAD_DIM - ROT), F32)
    zero_h = jnp.zeros((n, ROT_HALF), F32)
    c = jnp.concatenate([cos, cos, one], axis=1)
    s1 = jnp.concatenate([-sin, zero_h, zero_r], axis=1)
    s2 = jnp.concatenate([zero_h, sin, zero_r], axis=1)
    rep = LANES // HEAD_DIM
    return jnp.tile(c, (1, rep)), jnp.tile(s1, (1, rep)), jnp.tile(s2, (1, rep))


def _rope128(x, c, s1, s2):
    return x * c + pltpu.roll(x, LANES - ROT_HALF, 1) * s1 + pltpu.roll(x, ROT_HALF, 1) * s2


def _mix_in_kernel(x_ref, sh_ref, sc_ref, gpre_ref, w_ref, rc_ref, rs1_ref, rs2_ref,
                   q_ref, k_ref, v_ref, qi_ref, kiwi_ref, gb_ref, u_ref, *, conv_ch):
    x = x_ref[...]
    hb = (_rms(x) * gpre_ref[...] * (1.0 + sc_ref[0]) + sh_ref[0]).astype(BF16)
    c, s1, s2 = rc_ref[...], rs1_ref[...], rs2_ref[...]

    def proj(off, width):
        return jnp.dot(hb, w_ref[:, off:off + width], preferred_element_type=F32)

    def rope_cols(z):
        return [_rope128(z[:, g * LANES:(g + 1) * LANES], c, s1, s2) for g in range(z.shape[1] // LANES)]

    lane = lax.broadcasted_iota(I32, (x.shape[0], LANES), 1)
    low = lane < HEAD_DIM

    zq = proj(_OFF_Q, ATTN_WIDTH)
    for g, r in enumerate(rope_cols(zq)):
        r = r * Q_SCALE
        swapped = pltpu.roll(r, HEAD_DIM, 1)
        for e in range(2):
            h = 2 * g + e
            kv_group = h // HEADS_PER_KV
            src = r if e == kv_group else swapped
            keep = low if kv_group == 0 else jnp.logical_not(low)
            q_ref[h] = jnp.where(keep, src, 0.0).astype(BF16)
    zk = proj(_OFF_K, KV_WIDTH)
    for g, r in enumerate(rope_cols(zk)):
        k_ref[:, g * LANES:(g + 1) * LANES] = r
    v_ref[...] = proj(_OFF_V, KV_WIDTH)
    zqi = proj(_OFF_QI, IDX_HEADS * IDX_DIM)
    for g, r in enumerate(rope_cols(zqi)):
        qi_ref[:, (2 * g) * LANES:(2 * g + 1) * LANES] = jnp.where(low, r, 0.0).astype(BF16)
        qi_ref[:, (2 * g + 1) * LANES:(2 * g + 2) * LANES] = jnp.where(low, 0.0, r).astype(BF16)
    zkw = proj(_OFF_KIWI, LANES)
    kiwi_ref[...] = jnp.where(lane < IDX_DIM, _rope128(zkw, c, s1, s2), zkw * IDX_SCALE)
    gb_ref[...] = proj(_OFF_GB, conv_ch)
    u_ref[...] = proj(_OFF_GB + conv_ch, conv_ch) * proj(_OFF_GB + 2 * conv_ch, conv_ch)


def _mix_in(x, shift, scale, g_pre, w_in_p, rope_tabs, tm, tiles_per_group, rope_tiles, conv_ch):
    rows, d = x.shape
    const = lambda i: (0, 0)
    row = lambda i: (i, 0)
    rope_spec = pl.BlockSpec((tm, LANES), lambda i: (i % rope_tiles, 0))
    assert 2 * HEAD_DIM == LANES and KV_WIDTH == LANES and 2 * IDX_DIM == LANES
    widths = [(KV_WIDTH, F32), (KV_WIDTH, F32), (IDX_HEADS * LANES, BF16),
              (LANES, F32), (conv_ch, F32), (conv_ch, F32)]
    q_spec = pl.BlockSpec((N_HEADS, tm, LANES), lambda i: (0, i, 0))
    q_shape = jax.ShapeDtypeStruct((N_HEADS, rows, LANES), BF16)
    return pl.pallas_call(
        functools.partial(_mix_in_kernel, conv_ch=conv_ch),
        grid=(rows // tm,),
        in_specs=[pl.BlockSpec((tm, d), row),
                  _mod_spec(shift, tm, tiles_per_group),
                  _mod_spec(scale, tm, tiles_per_group),
                  pl.BlockSpec((1, d), const),
                  pl.BlockSpec(w_in_p.shape, const),
                  rope_spec, rope_spec, rope_spec],
        out_specs=[q_spec] + [pl.BlockSpec((tm, w), row) for w, _ in widths],
        out_shape=[q_shape] + [jax.ShapeDtypeStruct((rows, w), dt) for w, dt in widths],
        compiler_params=_cparams(("arbitrary",)),
        name="mix_in",
    )(x, shift, scale, g_pre, w_in_p, *rope_tabs)


def _key_to_float(u):
    o = u ^ INT_MIN
    bits = jnp.where(o >= 0, o, o ^ INT_MAX)
    return lax.bitcast_convert_type(bits, F32)


def _count(sc_ref, nkb, r0, rg, tk, pred):
    rows = slice(r0, r0 + rg)

    def body(kb, acc):
        for j in range(tk // LANES):
            x = sc_ref[kb, rows, j * LANES:(j + 1) * LANES]
            kpos = kb * tk + j * LANES + lax.broadcasted_iota(I32, (rg, LANES), 1)
            acc = acc + jnp.where(pred(x, kpos, rows), 1.0, 0.0)
        return acc

    group = 4

    def group_body(i, acc):
        for g in range(group):
            acc = body(group * i + g, acc)
        return acc

    acc = lax.fori_loop(0, nkb // group, group_body, jnp.zeros((rg, LANES), F32))
    return lax.fori_loop(group * (nkb // group), nkb, body, acc)


def _select_rows(cnt, rg, topk, idx_bits, row_ok, tie_fn=None, two_bits=False):
    def count_ge(cand):
        t = _key_to_float(cand)
        return cnt(lambda x, kp, rs: x >= t[rs])

    def bit_body(i, carry):
        key, c_ge = carry
        cand = key | jnp.left_shift(jnp.int32(1), 31 - i)
        c = count_ge(cand)
        ok = c >= topk
        return jnp.where(ok, cand, key), jnp.where(ok, c, c_ge)

    def two_bit_body(i, carry):
        key, c_ge = carry
        sh = 30 - 2 * i
        c1, c2, c3 = (key | jnp.left_shift(jnp.int32(v), sh) for v in (2, 1, 3))
        n1, n2, n3 = count_ge(c1), count_ge(c2), count_ge(c3)
        ok1, ok2, ok3 = n1 >= topk, n2 >= topk, n3 >= topk
        key = jnp.where(ok3, c3, jnp.where(ok1, c1, jnp.where(ok2, c2, key)))
        c_ge = jnp.where(ok3, n3, jnp.where(ok1, n1, jnp.where(ok2, n2, c_ge)))
        return key, c_ge

    init = (jnp.zeros((rg, LANES), I32), jnp.zeros((rg, LANES), F32))
    key, c_ge = lax.fori_loop(0, 16, two_bit_body, init) if two_bits else lax.fori_loop(0, 32, bit_body, init)
    thr = _key_to_float(key)
    enough = thr >= -F32_MAX
    thr = jnp.where(enough, thr, -F32_MAX)
    c_gt = cnt(lambda x, kp, rs: x > thr[rs])
    need = topk - c_gt
    excess = jnp.max(jnp.where((c_ge > topk) & enough & row_ok, 1.0, 0.0)) > 0.5

    def tie_search():
        if tie_fn is not None:
            return tie_fn(thr, need)

        def jbody(i, p):
            cand = p | jnp.left_shift(jnp.int32(1), idx_bits - 1 - i)
            c = cnt(lambda x, kp, rs: (x == thr[rs]) & (kp < cand[rs]))
            return jnp.where(c < need, cand, p)
        return lax.fori_loop(0, idx_bits, jbody, jnp.zeros((rg, LANES), I32))

    jmax = lax.cond(excess, tie_search, lambda: jnp.full((rg, LANES), INT_MAX, I32))
    return thr, jmax


def _tie_bound_blocked(sc_ref, cnt_ref, nkb, r0, rg, tk, thr, need):
    reps = tk // LANES
    thr_t = _tile_lanes(thr, reps)

    group = 8

    def count_body(i, carry):
        for g in range(group):
            kb = jnp.minimum(group * i + g, nkb - 1)
            eq = jnp.where(sc_ref[kb, r0:r0 + rg, :] == thr_t, 1.0, 0.0)
            c = sum(eq[:, j * LANES:(j + 1) * LANES] for j in range(reps))
            cnt_ref[kb] = jnp.broadcast_to(jnp.sum(c, axis=1, keepdims=True), (rg, LANES))
        return carry

    lax.fori_loop(0, (nkb + group - 1) // group, count_body, 0)

    def locate_body(kb, carry):
        run, kstar, before = carry
        new_run = run + cnt_ref[kb]
        hit = (run < need) & (new_run >= need)
        return new_run, jnp.where(hit, kb, kstar), jnp.where(hit, run, before)

    zeros = jnp.zeros((rg, LANES), F32)
    _, kstar, before = lax.fori_loop(0, nkb, locate_body, (zeros, jnp.full((rg, LANES), -1, I32), zeros))
    kstar_t = _tile_lanes(kstar, reps)

    def gather_body(kb, own):
        eq = jnp.where(sc_ref[kb, r0:r0 + rg, :] == thr_t, 1.0, 0.0)
        return jnp.where(kstar_t == kb, eq, own)

    own = lax.fori_loop(0, nkb, gather_body, jnp.zeros((rg, tk), F32))
    need_in = need - before
    col = lax.broadcasted_iota(I32, (rg, tk), 1)
    p = jnp.zeros((rg, LANES), I32)
    for b in reversed(range(int(tk - 1).bit_length())):
        cand = p | (1 << b)
        c = jnp.sum(jnp.where(col < _tile_lanes(cand, reps), own, 0.0), axis=1, keepdims=True)
        p = jnp.where(c < need_in, cand, p)
    return jnp.where(kstar >= 0, kstar * tk + p, INT_MAX)


def _select(sc_ref, cnt_ref, t_ref, j_ref, nkb, rows, rg, tk, topk, idx_bits):
    groups = range(0, rows, rg)

    def cnt(pred):
        acc = jnp.concatenate([_count(sc_ref, nkb, r0, rg, tk, pred) for r0 in groups], axis=0)
        return jnp.broadcast_to(jnp.sum(acc, axis=1, keepdims=True), (rows, LANES))

    def tie_fn(thr, need):
        return jnp.concatenate(
            [_tie_bound_blocked(sc_ref, cnt_ref, nkb, r0, rg, tk, thr[r0:r0 + rg], need[r0:r0 + rg]) for r0 in groups],
            axis=0)

    thr, jmax = _select_rows(cnt, rows, topk, idx_bits, True, tie_fn)
    t_ref[...] = thr
    j_ref[...] = jmax


def _sel_mask(scb, kpos, thr, jmax):
    return (scb > thr) | ((scb == thr) & (kpos <= jmax))


def _prompt_attn_kernel(q_ref, qi_ref, kiwi_ref, kt_ref, v_ref, kit_ref, o_ref,
                        sc_ref, cnt_ref, wib_ref, m_ref, l_ref, acc_ref, t_ref, j_ref,
                        *, tq, topk, idx_bits):
    tk = tq
    reps = tk // LANES
    i = pl.program_id(1)
    nkb = i + 1

    kiwi = kiwi_ref[...]
    for h in range(IDX_HEADS):
        wib_ref[h] = jnp.broadcast_to(kiwi[:, IDX_DIM + h:IDX_DIM + h + 1], (tq, LANES))

    def score_body(kb, carry):
        kit = kit_ref[kb]
        acc = None
        for h in range(IDX_HEADS):
            s = jnp.dot(qi_ref[:, h * LANES:(h + 1) * LANES], kit, preferred_element_type=F32)
            t = jnp.maximum(s, 0.0) * _tile_lanes(wib_ref[h], reps)
            acc = t if acc is None else acc + t
        sc_ref[kb] = acc
        return carry

    lax.fori_loop(0, nkb, score_body, 0)
    row = lax.broadcasted_iota(I32, (tq, tk), 0)
    col = lax.broadcasted_iota(I32, (tq, tk), 1)
    sc_ref[i] = jnp.where(col <= row, sc_ref[i], -jnp.inf)

    _select(sc_ref, cnt_ref, t_ref, j_ref, nkb, tq, min(128, tq), tk, topk, idx_bits)

    def bias_body(kb, carry):
        thr = _tile_lanes(t_ref[...], reps)
        jmax = _tile_lanes(j_ref[...], reps)
        sc_ref[kb] = jnp.where(_sel_mask(sc_ref[kb], kb * tk + col, thr, jmax), 0.0, NEG_BIG)
        return carry

    lax.fori_loop(0, nkb, bias_body, 0)

    m_ref[...] = jnp.full(m_ref.shape, NEG_BIG, F32)
    l_ref[...] = jnp.zeros(l_ref.shape, F32)
    acc_ref[...] = jnp.zeros(acc_ref.shape, F32)

    def attn_body(kb, carry):
        kt = kt_ref[kb]
        vb = v_ref[pl.ds(pl.multiple_of(kb * tk, tk), tk), :]
        bias = sc_ref[kb]
        for h in range(N_HEADS):
            s = jnp.dot(q_ref[h], kt, preferred_element_type=F32) + bias
            m_prev = m_ref[h]
            m_new = jnp.maximum(m_prev, jnp.max(s, axis=1, keepdims=True))
            alpha = jnp.exp2(m_prev - m_new)
            p = jnp.exp2(s - _tile_lanes(m_new, reps))
            l_ref[h] = alpha * l_ref[h] + sum(p[:, j * LANES:(j + 1) * LANES] for j in range(reps))
            acc_ref[h] = alpha * acc_ref[h] + jnp.dot(p.astype(BF16), vb, preferred_element_type=F32)
            m_ref[h] = m_new
        return carry

    lax.fori_loop(0, nkb, attn_body, 0)

    for h in range(N_HEADS):
        g = h // HEADS_PER_KV
        o = acc_ref[h] / jnp.sum(l_ref[h], axis=1, keepdims=True)
        o_ref[:, h * HEAD_DIM:(h + 1) * HEAD_DIM] = o[:, g * HEAD_DIM:(g + 1) * HEAD_DIM]


def _prompt_attn(q, qi, kiwi, kt, v, kit, tq, topk):
    b, s, _ = qi.shape
    nb = s // tq
    idx_bits = int(s).bit_length()
    blk = lambda w: pl.BlockSpec((None, tq, w), lambda bi, i: (bi, i, 0))
    full = lambda shp: pl.BlockSpec((None,) + shp, lambda bi, i: (bi,) + (0,) * len(shp),
                                    pipeline_mode=pl.Buffered(1))
    return pl.pallas_call(
        functools.partial(_prompt_attn_kernel, tq=tq, topk=topk, idx_bits=idx_bits),
        grid=(b, nb),
        in_specs=[pl.BlockSpec((N_HEADS, tq, LANES), lambda bi, i: (0, bi * nb + i, 0)),
                  blk(IDX_HEADS * LANES), blk(LANES),
                  full((nb, KV_WIDTH, tq)), full((s, KV_WIDTH)), full((nb, 2 * IDX_DIM, tq))],
        out_specs=blk(ATTN_WIDTH),
        out_shape=jax.ShapeDtypeStruct((b, s, ATTN_WIDTH), F32),
        scratch_shapes=[pltpu.VMEM((nb, tq, tq), F32),
                        pltpu.VMEM((nb, min(128, tq), LANES), F32),
                        pltpu.VMEM((IDX_HEADS, tq, LANES), F32),
                        pltpu.VMEM((N_HEADS, tq, LANES), F32),
                        pltpu.VMEM((N_HEADS, tq, LANES), F32),
                        pltpu.VMEM((N_HEADS, tq, KV_WIDTH), F32),
                        pltpu.VMEM((tq, LANES), F32),
                        pltpu.VMEM((tq, LANES), I32)],
        compiler_params=_cparams(("arbitrary", "arbitrary")),
        name="prompt_attn",
    )(q, qi, kiwi, kt, v, kit)


def _sample_attn_kernel(pt_ref, q_ref, qih_ref, wib_ref, kin_ref, kn_ref, vn_ref, *rest,
                        n_steps, pps, n_tok, tpad, page, topk, idx_bits):
    cik_refs, ck_refs, cv_refs = rest[:pps], rest[pps:2 * pps], rest[2 * pps:3 * pps]
    o_ref, sc_ref, sa_ref, vs_ref = rest[3 * pps:]
    step = pl.program_id(1)
    n_pages = n_steps * pps
    nkb = n_pages + 1
    qih = qih_ref[...]
    wib = wib_ref[...]
    q = q_ref[...]

    def idx_score(s):
        acc = None
        for h in range(IDX_HEADS):
            t = jnp.maximum(s[h * tpad:(h + 1) * tpad], 0.0) * wib[h * tpad:(h + 1) * tpad]
            acc = t if acc is None else acc + t
        return acc

    for j in range(pps):
        pg = step * pps + j
        sc_ref[pg] = idx_score(jnp.dot(qih, cik_refs[j][...].astype(BF16), preferred_element_type=F32))
        sa_ref[pg] = jnp.dot(q, ck_refs[j][...].astype(BF16), preferred_element_type=F32)
        vs_ref[pg] = cv_refs[j][...].astype(BF16)

    @pl.when(step == n_steps - 1)
    def _():
        s_new = idx_score(jnp.dot(qih, kin_ref[...], preferred_element_type=F32))
        tok = lax.broadcasted_iota(I32, (tpad, page), 0)
        col = lax.broadcasted_iota(I32, (tpad, page), 1)
        sc_ref[n_pages] = jnp.where((col <= tok) & (col < n_tok), s_new, -jnp.inf)
        sa_ref[n_pages] = jnp.dot(q, kn_ref[...], preferred_element_type=F32)
        vs_ref[n_pages] = vn_ref[...]

        kpos = (lax.broadcasted_iota(I32, (nkb, tpad, page), 0) * page
                + lax.broadcasted_iota(I32, (nkb, tpad, page), 2))

        def cnt(pred):
            w = jnp.where(pred(sc_ref[...], kpos, slice(None)), 1.0, 0.0)
            chunk = -(-nkb // SUBLANES)
            c = sum(jnp.sum(w[a:a + chunk], axis=0) for a in range(0, nkb, chunk))
            return jnp.broadcast_to(jnp.sum(c, axis=1, keepdims=True), c.shape)

        thr, jmax = _select_rows(cnt, tpad, topk, idx_bits, tok[:, :LANES] < n_tok, two_bits=True)
        sc_ref[...] = jnp.where(_sel_mask(sc_ref[...], kpos, thr[None], jmax[None]), 0.0, NEG_BIG)

        heads = N_HEADS
        inv_l = []
        for t in range(n_tok):
            s = sa_ref[:, t * heads:(t + 1) * heads, :] + sc_ref[:, t:t + 1, :]
            m = jnp.max(jnp.max(s, axis=0), axis=1, keepdims=True)
            pr = jnp.exp2(s - m[None])
            sa_ref[:, t * heads:(t + 1) * heads, :] = pr
            l = jnp.sum(jnp.sum(pr, axis=0), axis=1, keepdims=True)
            inv_l.append(jnp.broadcast_to(1.0 / l, (heads, KV_WIDTH)))

        p_all = jnp.concatenate([sa_ref[kb].astype(BF16) for kb in range(nkb)], axis=1)
        vt_all = jnp.concatenate([vs_ref[kb] for kb in range(nkb)], axis=1)
        acc = lax.dot_general(p_all, vt_all, NT_DIMS, preferred_element_type=F32)
        o_ref[...] = acc * jnp.concatenate(inv_l, axis=0)


def _sample_attn(page_table, q, qih, wib, kin_t, kn_t, vn_t, cik_t, ck_t, cv_t, n_tok, tpad, topk, pps):
    n, n_pages = page_table.shape
    page = ck_t.shape[2]
    assert page == LANES and KV_WIDTH == LANES and n_pages % pps == 0
    n_steps = n_pages // pps
    idx_bits = int((n_pages + 1) * page).bit_length()
    rows = n_tok * N_HEADS
    seq = lambda r, w: pl.BlockSpec((None, r, w), lambda b, p, pt: (b, 0, 0))

    def paged(r, j):
        return pl.BlockSpec((None, r, page), lambda b, p, pt: (pt[b, p * pps + j], 0, 0))

    grid_spec = pltpu.PrefetchScalarGridSpec(
        num_scalar_prefetch=1,
        grid=(n, n_steps),
        in_specs=([seq(rows, KV_WIDTH), seq(IDX_HEADS * tpad, IDX_DIM), seq(IDX_HEADS * tpad, LANES),
                   seq(IDX_DIM, page), seq(KV_WIDTH, page), seq(KV_WIDTH, page)]
                  + [paged(IDX_DIM, j) for j in range(pps)]
                  + [paged(KV_WIDTH, j) for j in range(pps)]
                  + [paged(KV_WIDTH, j) for j in range(pps)]),
        out_specs=seq(rows, KV_WIDTH),
        scratch_shapes=[pltpu.VMEM((n_pages + 1, tpad, page), F32),
                        pltpu.VMEM((n_pages + 1, rows, page), F32),
                        pltpu.VMEM((n_pages + 1, KV_WIDTH, page), BF16)],
    )
    return pl.pallas_call(
        functools.partial(_sample_attn_kernel, n_steps=n_steps, pps=pps, n_tok=n_tok, tpad=tpad, page=page,
                          topk=topk, idx_bits=idx_bits),
        grid_spec=grid_spec,
        out_shape=jax.ShapeDtypeStruct((n, rows, KV_WIDTH), F32),
        compiler_params=_cparams(("arbitrary", "arbitrary")),
        name="sample_attn",
    )(page_table, q, qih, wib, kin_t, kn_t, vn_t, *([cik_t] * pps), *([ck_t] * pps), *([cv_t] * pps))


def _mix_out_tail(x, o, gb, u0, um1, um2, wc_ref, wo_ref, gt, gpost, aw):
    y = wc_ref[0:1, :] * um2 + wc_ref[1:2, :] * um1 + wc_ref[2:3, :] * u0
    mixed = (jnp.dot(o.astype(BF16), wo_ref[0:aw, :], preferred_element_type=F32)
             + jnp.dot((gb * y).astype(BF16), wo_ref[aw:, :], preferred_element_type=F32))
    return x + gt * (_rms(mixed) * gpost)


def _mix_out_prompt_kernel(x_ref, o_ref_in, gb_ref, u_ref, halo_ref, gt_ref, gpost_ref, wc_ref, wo_ref,
                           out_ref, ext_ref, *, tm, tiles_per_seq):
    i = pl.program_id(0)
    halo = halo_ref[...]
    ext_ref[0:SUBLANES, :] = jnp.where(i % tiles_per_seq == 0, jnp.zeros_like(halo), halo)
    ext_ref[SUBLANES:, :] = u_ref[...]
    um2 = ext_ref[SUBLANES - 2:SUBLANES - 2 + tm, :]
    um1 = ext_ref[SUBLANES - 1:SUBLANES - 1 + tm, :]
    out_ref[...] = _mix_out_tail(x_ref[...], o_ref_in[...], gb_ref[...], u_ref[...], um1, um2,
                                 wc_ref, wo_ref, gt_ref[0], gpost_ref[...], ATTN_WIDTH)


def _mix_out_prompt(x, o, gb, u, gate, g_post, w_conv, w_o, tm, tiles_per_seq):
    rows, d = x.shape
    cw = u.shape[1]
    const = lambda i: (0, 0)
    row = lambda i: (i, 0)
    hb = tm // SUBLANES
    return pl.pallas_call(
        functools.partial(_mix_out_prompt_kernel, tm=tm, tiles_per_seq=tiles_per_seq),
        grid=(rows // tm,),
        in_specs=[pl.BlockSpec((tm, d), row),
                  pl.BlockSpec((tm, ATTN_WIDTH), row),
                  pl.BlockSpec((tm, cw), row),
                  pl.BlockSpec((tm, cw), row),
                  pl.BlockSpec((SUBLANES, cw), lambda i: (jnp.maximum(i * hb - 1, 0), 0)),
                  _mod_spec(gate, tm, tiles_per_seq),
                  pl.BlockSpec((1, d), const),
                  pl.BlockSpec(w_conv.shape, const),
                  pl.BlockSpec(w_o.shape, const)],
        out_specs=pl.BlockSpec((tm, d), row),
        out_shape=jax.ShapeDtypeStruct((rows, d), F32),
        scratch_shapes=[pltpu.VMEM((tm + SUBLANES, cw), F32)],
        compiler_params=_cparams(("arbitrary",)),
        name="mix_out_prompt",
    )(x, o, gb, u, u, gate, g_post, w_conv, w_o)


def _mix_out_sample_kernel(x_ref, o_ref_in, gb_ref, um2_ref, um1_ref, u0_ref, gt_ref, gpost_ref, wc_ref, wo_ref,
                           out_ref):
    out_ref[...] = _mix_out_tail(x_ref[...], o_ref_in[...], gb_ref[...], u0_ref[...], um1_ref[...], um2_ref[...],
                                 wc_ref, wo_ref, gt_ref[0], gpost_ref[...], ATTN_WIDTH)


def _mix_out_sample(x, o, gb, ext, gate, g_post, w_conv, w_o, tm):
    rows, d = x.shape
    cw = gb.shape[1]
    const = lambda i: (0, 0)
    row = lambda i: (i, 0)
    return pl.pallas_call(
        _mix_out_sample_kernel,
        grid=(rows // tm,),
        in_specs=[pl.BlockSpec((tm, d), row),
                  pl.BlockSpec((tm, ATTN_WIDTH), row),
                  pl.BlockSpec((tm, cw), row),
                  pl.BlockSpec((tm, cw), lambda i: (i, 0)),
                  pl.BlockSpec((tm, cw), lambda i: (i + 1, 0)),
                  pl.BlockSpec((tm, cw), lambda i: (i + 2, 0)),
                  pl.BlockSpec((1, tm, d), lambda i: (0, 0, 0)),
                  pl.BlockSpec((1, d), const),
                  pl.BlockSpec(w_conv.shape, const),
                  pl.BlockSpec(w_o.shape, const)],
        out_specs=pl.BlockSpec((tm, d), row),
        out_shape=jax.ShapeDtypeStruct((rows, d), F32),
        compiler_params=_cparams(("arbitrary",)),
        name="mix_out_sample",
    )(x, o, gb, ext, ext, ext, gate, g_post, w_conv, w_o)


def _prompt_layer(x, mod, lw, tm, tq):
    b, s, d = x.shape
    conv_ch = lw["w_conv"].shape[1]
    tps = s // tm
    md = lambda j, k: mod[:, j, k][:, None, :]
    xr = x.reshape(b * s, d)
    x1 = _ffn(xr, md(0, 0), md(0, 1), md(0, 2), lw["g_pre"][0:1], lw["g_post"][0:1], lw["ffn1_w13"], lw["ffn1_w2"],
              tm, tps)
    tabs = _rope_tables(jnp.arange(s))
    q, k, v, qi, kiwi, gb, u = _mix_in(x1, md(1, 0), md(1, 1), lw["g_pre"][1:2], lw["w_in_p"], tabs, tm, tps, tps,
                                       conv_ch)
    nb = s // tq
    kt = jnp.swapaxes(k.astype(BF16).reshape(b, nb, tq, KV_WIDTH), 2, 3)
    kit = jnp.swapaxes(kiwi[:, :IDX_DIM].astype(BF16).reshape(b, nb, tq, IDX_DIM), 2, 3)
    kit = jnp.concatenate([kit, kit], axis=2)
    topk = min(TOPK_MAX, s // 4)
    o = _prompt_attn(q, qi.reshape(b, s, -1), kiwi.reshape(b, s, -1), kt,
                     v.astype(BF16).reshape(b, s, KV_WIDTH), kit, tq, topk)
    x2 = _mix_out_prompt(x1, o.reshape(b * s, -1), gb, u, md(1, 2), lw["g_post"][1:2], lw["w_conv"], lw["w_o"],
                         tm, tps)
    x3 = _ffn(x2, md(2, 0), md(2, 1), md(2, 2), lw["g_pre"][2:3], lw["g_post"][2:3], lw["ffn2_w13"], lw["ffn2_w2"],
              tm, tps)
    state = (k.reshape(b, s, N_KV_HEADS, HEAD_DIM), v.reshape(b, s, N_KV_HEADS, HEAD_DIM),
             kiwi[:, :IDX_DIM].reshape(b, s, IDX_DIM), u.reshape(b, s, conv_ch)[:, s - (CONV_K - 1):])
    return x3.reshape(b, s, d), state


def _sample_layer(x_tm, mod, lw, cache_k, cache_v, cache_idx_k, state_conv, page_table):
    n = mod.shape[0]
    t_tok = x_tm.shape[0] // n
    d = x_tm.shape[1]
    conv_ch = lw["w_conv"].shape[1]
    n_pages = page_table.shape[1]
    page = cache_k.shape[1]
    past = n_pages * page
    md = lambda j, k: mod[:, j, k][None]
    x1 = _ffn(x_tm, md(0, 0), md(0, 1), md(0, 2), lw["g_pre"][0:1], lw["g_post"][0:1], lw["ffn1_w13"],
              lw["ffn1_w2"], n, t_tok)
    pos = jnp.repeat(past + jnp.arange(t_tok), n)
    tabs = _rope_tables(pos)
    q, k, v, qi, kiwi, gb, u = _mix_in(x1, md(1, 0), md(1, 1), lw["g_pre"][1:2], lw["w_in_p"], tabs, n, t_tok, t_tok,
                                       conv_ch)
    tpad = SUBLANES
    assert t_tok <= tpad

    def seq_major(a, heads, width):
        a = a.reshape(t_tok, n, heads, width).transpose(1, 2, 0, 3)
        a = jnp.pad(a, ((0, 0), (0, 0), (0, tpad - t_tok), (0, 0)))
        return a.reshape(n, heads * tpad, width)

    q_seq = q.reshape(N_HEADS, t_tok, n, LANES).transpose(2, 1, 0, 3).reshape(n, t_tok * N_HEADS, LANES)
    qi_nat = jnp.concatenate([qi[:, h * LANES + (h % 2) * IDX_DIM:h * LANES + (h % 2 + 1) * IDX_DIM]
                              for h in range(IDX_HEADS)], axis=1)
    qih = seq_major(qi_nat, IDX_HEADS, IDX_DIM)
    wib = jnp.broadcast_to(seq_major(kiwi[:, IDX_DIM:IDX_DIM + IDX_HEADS], IDX_HEADS, 1),
                           (n, IDX_HEADS * tpad, LANES))

    def new_keys_t(a):
        a = a.reshape(t_tok, n, -1).transpose(1, 2, 0).astype(BF16)
        return jnp.pad(a, ((0, 0), (0, 0), (0, page - t_tok)))

    ck_t = cache_k.transpose(0, 2, 3, 1).reshape(-1, KV_WIDTH, page)
    cv_t = cache_v.transpose(0, 2, 3, 1).reshape(-1, KV_WIDTH, page)
    cik_t = cache_idx_k.transpose(0, 2, 1)
    topk = min(TOPK_MAX, (past + t_tok) // 4)
    pps = max(p for p in (16, 8, 4, 2, 1) if n_pages % p == 0)
    o = _sample_attn(page_table, q_seq, qih, wib, new_keys_t(kiwi[:, :IDX_DIM]), new_keys_t(k), new_keys_t(v),
                     cik_t, ck_t, cv_t, t_tok, tpad, topk, pps)
    o = o.reshape(n, t_tok, N_KV_HEADS, HEADS_PER_KV, N_KV_HEADS, HEAD_DIM)
    o = jnp.stack([o[:, :, g, :, g] for g in range(N_KV_HEADS)], axis=2)
    o_tm = o.transpose(1, 0, 2, 3, 4).reshape(t_tok * n, ATTN_WIDTH)
    ext = jnp.concatenate([state_conv.transpose(1, 0, 2).reshape(-1, conv_ch), u], axis=0)
    x2 = _mix_out_sample(x1, o_tm, gb, ext, md(1, 2), lw["g_post"][1:2], lw["w_conv"], lw["w_o"], n)
    x3 = _ffn(x2, md(2, 0), md(2, 1), md(2, 2), lw["g_pre"][2:3], lw["g_post"][2:3], lw["ffn2_w13"],
              lw["ffn2_w2"], n, t_tok)
    tm_to_seq = lambda a: a.reshape((t_tok, n) + a.shape[1:]).swapaxes(0, 1)
    state = (tm_to_seq(k).reshape(n, t_tok, N_KV_HEADS, HEAD_DIM), tm_to_seq(v).reshape(n, t_tok, N_KV_HEADS, HEAD_DIM),
             tm_to_seq(kiwi[:, :IDX_DIM]), tm_to_seq(u)[:, t_tok - (CONV_K - 1):])
    return x3, state


def _row_tile(s, target):
    t = min(target, s)
    while s % t:
        t //= 2
    return t


def kernel(x_prompt, x_sample, cache_k, cache_v, cache_idx_k, state_conv, page_table, c_prompt, c_sample, w_ada, b_ada, g_pre, g_post, ffn1_w13, ffn1_w2, w_in, w_conv, w_o, ffn2_w13, ffn2_w2):
    depth = w_ada.shape[0]
    b, s, d = x_prompt.shape
    n, t_tok, _ = x_sample.shape
    conv_ch = w_conv.shape[2]
    tm = _row_tile(s, 512)
    tq = _row_tile(s, 256)

    yp = x_prompt
    ys = x_sample.swapaxes(0, 1).reshape(t_tok * n, d)
    c_all = jnp.concatenate([c_prompt, c_sample], axis=0)
    n_c = c_all.shape[0]
    c_all = jnp.pad(c_all, ((0, -n_c % SUBLANES), (0, 0)))
    outs = [[] for _ in range(8)]
    for l in range(depth):
        lw = dict(g_pre=g_pre[l], g_post=g_post[l], w_conv=w_conv[l],
                  ffn1_w13=ffn1_w13[l].astype(BF16), ffn1_w2=ffn1_w2[l].astype(BF16),
                  ffn2_w13=ffn2_w13[l].astype(BF16), ffn2_w2=ffn2_w2[l].astype(BF16),
                  w_in_p=_pack_w_in(w_in[l], conv_ch), w_o=w_o[l].astype(BF16))
        mod = _ada_mod(c_all, w_ada[l], b_ada[l])[:n_c].reshape(n_c, N_SUB, 3, d)
        yp, st_p = _prompt_layer(yp, mod[:b], lw, tm, tq)
        ys, st_s = _sample_layer(ys, mod[b:], lw, cache_k[l], cache_v[l], cache_idx_k[l], state_conv[l], page_table)
        for lst, a in zip(outs, st_p + st_s):
            lst.append(a)
    ys = ys.reshape(t_tok, n, d).swapaxes(0, 1)
    return (yp, ys) + tuple(jnp.stack(lst) for lst in outs)
```

```python
import functools

import numpy as np
import jax
import jax.numpy as jnp
from jax import lax
from jax.experimental import pallas as pl
from jax.experimental.pallas import tpu as pltpu

F32 = jnp.float32
BF16 = jnp.bfloat16
I32 = jnp.int32

N_HEADS = 8
HEAD_DIM = 64
N_KV_HEADS = 2
HEADS_PER_KV = N_HEADS // N_KV_HEADS
ATTN_WIDTH = N_HEADS * HEAD_DIM
KV_WIDTH = N_KV_HEADS * HEAD_DIM
ROT_DIV = 4
ROT = HEAD_DIM // ROT_DIV
ROT_HALF = ROT // 2
ROPE_THETA = 500000.0
IDX_HEADS = 4
IDX_DIM = 64
IDX_SCALE = (IDX_HEADS * IDX_DIM) ** -0.5
TOPK_MAX = 256
CONV_K = 3
N_SUB = 3
RMS_EPS = 1e-6
Q_SCALE = HEAD_DIM ** -0.5 * float(np.log2(np.e))

LANES = 128
SUBLANES = 8
VMEM_LIMIT = 56 * 1024 * 1024

NEG_BIG = -1e30
F32_MAX = float(np.finfo(np.float32).max)
INT_MIN = -(2 ** 31)
INT_MAX = 2 ** 31 - 1

NT_DIMS = (((1,), (1,)), ((), ()))


def _cparams(sem):
    return pltpu.CompilerParams(dimension_semantics=sem, vmem_limit_bytes=VMEM_LIMIT)


def _silu(x):
    return x * (1.0 / (1.0 + jnp.exp(-x)))


def _rms(x):
    return x * lax.rsqrt(jnp.mean(x * x, axis=-1, keepdims=True) + RMS_EPS)


def _tile_lanes(x, reps):
    return x if reps == 1 else jnp.concatenate([x] * reps, axis=1)


def _ada_kernel(c_ref, w_ref, b_ref, o_ref):
    a = _silu(c_ref[...]).astype(BF16)
    o_ref[...] = jnp.dot(a, w_ref[...].astype(BF16), preferred_element_type=F32) + b_ref[...]


def _ada_mod(c, w_ada, b_ada):
    n, d = c.shape
    nout = w_ada.shape[1]
    tn = 9 * LANES if nout % (9 * LANES) == 0 else LANES
    return pl.pallas_call(
        _ada_kernel,
        grid=(nout // tn,),
        in_specs=[pl.BlockSpec((n, d), lambda j: (0, 0)),
                  pl.BlockSpec((d, tn), lambda j: (0, j)),
                  pl.BlockSpec((1, tn), lambda j: (0, j))],
        out_specs=pl.BlockSpec((n, tn), lambda j: (0, j)),
        out_shape=jax.ShapeDtypeStruct((n, nout), F32),
        compiler_params=_cparams(("arbitrary",)),
        name="ada_mod",
    )(c, w_ada, b_ada.reshape(1, nout))


def _ffn_kernel(x_ref, sh_ref, sc_ref, gt_ref, gpre_ref, gpost_ref, w13_ref, w2_ref, o_ref, *, d_ff, chunk):
    x = x_ref[...]
    h = _rms(x) * gpre_ref[...] * (1.0 + sc_ref[0]) + sh_ref[0]
    hb = h.astype(BF16)
    y = jnp.zeros(x.shape, F32)
    for c in range(d_ff // chunk):
        a = jnp.dot(hb, w13_ref[:, c * chunk:(c + 1) * chunk], preferred_element_type=F32)
        b = jnp.dot(hb, w13_ref[:, d_ff + c * chunk:d_ff + (c + 1) * chunk], preferred_element_type=F32)
        g = (_silu(a) * b).astype(BF16)
        y = y + jnp.dot(g, w2_ref[c * chunk:(c + 1) * chunk, :], preferred_element_type=F32)
    o_ref[...] = x + 0.5 * gt_ref[0] * (_rms(y) * gpost_ref[...])


def _mod_spec(mod, tm, tiles_per_group):
    _, r, d = mod.shape
    return pl.BlockSpec((1, r, d), lambda i: (i // tiles_per_group, 0, 0))


def _ffn(x, shift, scale, gate, g_pre, g_post, w13, w2, tm, tiles_per_group):
    rows, d = x.shape
    d_ff = w2.shape[0]
    const = lambda i: (0, 0)
    return pl.pallas_call(
        functools.partial(_ffn_kernel, d_ff=d_ff, chunk=min(512, d_ff)),
        grid=(rows // tm,),
        in_specs=[pl.BlockSpec((tm, d), lambda i: (i, 0)),
                  _mod_spec(shift, tm, tiles_per_group),
                  _mod_spec(scale, tm, tiles_per_group),
                  _mod_spec(gate, tm, tiles_per_group),
                  pl.BlockSpec((1, d), const),
                  pl.BlockSpec((1, d), const),
                  pl.BlockSpec(w13.shape, const),
                  pl.BlockSpec(w2.shape, const)],
        out_specs=pl.BlockSpec((tm, d), lambda i: (i, 0)),
        out_shape=jax.ShapeDtypeStruct((rows, d), F32),
        compiler_params=_cparams(("arbitrary",)),
        name="ffn",
    )(x, shift, scale, gate, g_pre, g_post, w13, w2)


_OFF_Q = 0
_OFF_K = _OFF_Q + ATTN_WIDTH
_OFF_V = _OFF_K + KV_WIDTH
_OFF_QI = _OFF_V + KV_WIDTH
_OFF_KIWI = _OFF_QI + IDX_HEADS * IDX_DIM
_OFF_GB = _OFF_KIWI + LANES


def _pack_w_in(w_in, conv_ch):
    splits = [ATTN_WIDTH, KV_WIDTH, KV_WIDTH, IDX_HEADS * IDX_DIM, IDX_DIM, IDX_HEADS, conv_ch, conv_ch, conv_ch]
    offs = np.cumsum([0] + splits)
    d = w_in.shape[0]
    pad = jnp.zeros((d, LANES - IDX_DIM - IDX_HEADS), w_in.dtype)
    return jnp.concatenate([w_in[:, :offs[6]], pad, w_in[:, offs[6]:]], axis=1).astype(BF16)


def _rope_tables(pos):
    inv = jnp.power(ROPE_THETA, -jnp.arange(ROT_HALF, dtype=F32) * 2.0 / ROT)
    ang = pos.astype(F32)[:, None] * inv[None, :]
    cos, sin = jnp.cos(ang), jnp.sin(ang)
    n = pos.shape[0]
    one = jnp.ones((n, HEAD_DIM - ROT), F32)
    zero_r = jnp.zeros((n, HEAD_DIM - ROT), F32)
    zero_h = jnp.zeros((n, ROT_HALF), F32)
    c = jnp.concatenate([cos, cos, one], axis=1)
    s1 = jnp.concatenate([-sin, zero_h, zero_r], axis=1)
    s2 = jnp.concatenate([zero_h, sin, zero_r], axis=1)
    rep = LANES // HEAD_DIM
    return jnp.tile(c, (1, rep)), jnp.tile(s1, (1, rep)), jnp.tile(s2, (1, rep))


def _rope128(x, c, s1, s2):
    return x * c + pltpu.roll(x, LANES - ROT_HALF, 1) * s1 + pltpu.roll(x, ROT_HALF, 1) * s2


def _mix_in_kernel(x_ref, sh_ref, sc_ref, gpre_ref, w_ref, rc_ref, rs1_ref, rs2_ref,
                   q_ref, k_ref, v_ref, qi_ref, kiwi_ref, gb_ref, u_ref, *, conv_ch):
    x = x_ref[...]
    hb = (_rms(x) * gpre_ref[...] * (1.0 + sc_ref[0]) + sh_ref[0]).astype(BF16)
    c, s1, s2 = rc_ref[...], rs1_ref[...], rs2_ref[...]

    def proj(off, width):
        return jnp.dot(hb, w_ref[:, off:off + width], preferred_element_type=F32)

    def rope_cols(z):
        return [_rope128(z[:, g * LANES:(g + 1) * LANES], c, s1, s2) for g in range(z.shape[1] // LANES)]

    lane = lax.broadcasted_iota(I32, (x.shape[0], LANES), 1)
    low = lane < HEAD_DIM

    zq = proj(_OFF_Q, ATTN_WIDTH)
    for g, r in enumerate(rope_cols(zq)):
        r = r * Q_SCALE
        swapped = pltpu.roll(r, HEAD_DIM, 1)
        for e in range(2):
            h = 2 * g + e
            kv_group = h // HEADS_PER_KV
            src = r if e == kv_group else swapped
            keep = low if kv_group == 0 else jnp.logical_not(low)
            q_ref[h] = jnp.where(keep, src, 0.0).astype(BF16)
    zk = proj(_OFF_K, KV_WIDTH)
    for g, r in enumerate(rope_cols(zk)):
        k_ref[:, g * LANES:(g + 1) * LANES] = r
    v_ref[...] = proj(_OFF_V, KV_WIDTH)
    zqi = proj(_OFF_QI, IDX_HEADS * IDX_DIM)
    for g, r in enumerate(rope_cols(zqi)):
        qi_ref[:, (2 * g) * LANES:(2 * g + 1) * LANES] = jnp.where(low, r, 0.0).astype(BF16)
        qi_ref[:, (2 * g + 1) * LANES:(2 * g + 2) * LANES] = jnp.where(low, 0.0, r).astype(BF16)
    zkw = proj(_OFF_KIWI, LANES)
    kiwi_ref[...] = jnp.where(lane < IDX_DIM, _rope128(zkw, c, s1, s2), zkw * IDX_SCALE)
    gb_ref[...] = proj(_OFF_GB, conv_ch)
    u_ref[...] = proj(_OFF_GB + conv_ch, conv_ch) * proj(_OFF_GB + 2 * conv_ch, conv_ch)


def _mix_in(x, shift, scale, g_pre, w_in_p, rope_tabs, tm, tiles_per_group, rope_tiles, conv_ch):
    rows, d = x.shape
    const = lambda i: (0, 0)
    row = lambda i: (i, 0)
    rope_spec = pl.BlockSpec((tm, LANES), lambda i: (i % rope_tiles, 0))
    assert 2 * HEAD_DIM == LANES and KV_WIDTH == LANES and 2 * IDX_DIM == LANES
    widths = [(KV_WIDTH, F32), (KV_WIDTH, F32), (IDX_HEADS * LANES, BF16),
              (LANES, F32), (conv_ch, F32), (conv_ch, F32)]
    q_spec = pl.BlockSpec((N_HEADS, tm, LANES), lambda i: (0, i, 0))
    q_shape = jax.ShapeDtypeStruct((N_HEADS, rows, LANES), BF16)
    return pl.pallas_call(
        functools.partial(_mix_in_kernel, conv_ch=conv_ch),
        grid=(rows // tm,),
        in_specs=[pl.BlockSpec((tm, d), row),
                  _mod_spec(shift, tm, tiles_per_group),
                  _mod_spec(scale, tm, tiles_per_group),
                  pl.BlockSpec((1, d), const),
                  pl.BlockSpec(w_in_p.shape, const),
                  rope_spec, rope_spec, rope_spec],
        out_specs=[q_spec] + [pl.BlockSpec((tm, w), row) for w, _ in widths],
        out_shape=[q_shape] + [jax.ShapeDtypeStruct((rows, w), dt) for w, dt in widths],
        compiler_params=_cparams(("arbitrary",)),
        name="mix_in",
    )(x, shift, scale, g_pre, w_in_p, *rope_tabs)


def _key_to_float(u):
    o = u ^ INT_MIN
    bits = jnp.where(o >= 0, o, o ^ INT_MAX)
    return lax.bitcast_convert_type(bits, F32)


def _count(sc_ref, nkb, r0, rg, tk, pred):
    rows = slice(r0, r0 + rg)

    def body(kb, acc):
        for j in range(tk // LANES):
            x = sc_ref[kb, rows, j * LANES:(j + 1) * LANES]
            kpos = kb * tk + j * LANES + lax.broadcasted_iota(I32, (rg, LANES), 1)
            acc = acc + jnp.where(pred(x, kpos, rows), 1.0, 0.0)
        return acc

    group = 4

    def group_body(i, acc):
        for g in range(group):
            acc = body(group * i + g, acc)
        return acc

    acc = lax.fori_loop(0, nkb // group, group_body, jnp.zeros((rg, LANES), F32))
    return lax.fori_loop(group * (nkb // group), nkb, body, acc)


def _select_rows(cnt, rg, topk, idx_bits, row_ok, tie_fn=None, two_bits=False):
    def count_ge(cand):
        t = _key_to_float(cand)
        return cnt(lambda x, kp, rs: x >= t[rs])

    def bit_body(i, carry):
        key, c_ge = carry
        cand = key | jnp.left_shift(jnp.int32(1), 31 - i)
        c = count_ge(cand)
        ok = c >= topk
        return jnp.where(ok, cand, key), jnp.where(ok, c, c_ge)

    def two_bit_body(i, carry):
        key, c_ge = carry
        sh = 30 - 2 * i
        c1, c2, c3 = (key | jnp.left_shift(jnp.int32(v), sh) for v in (2, 1, 3))
        n1, n2, n3 = count_ge(c1), count_ge(c2), count_ge(c3)
        ok1, ok2, ok3 = n1 >= topk, n2 >= topk, n3 >= topk
        key = jnp.where(ok3, c3, jnp.where(ok1, c1, jnp.where(ok2, c2, key)))
        c_ge = jnp.where(ok3, n3, jnp.where(ok1, n1, jnp.where(ok2, n2, c_ge)))
        return key, c_ge

    init = (jnp.zeros((rg, LANES), I32), jnp.zeros((rg, LANES), F32))
    key, c_ge = lax.fori_loop(0, 16, two_bit_body, init) if two_bits else lax.fori_loop(0, 32, bit_body, init)
    thr = _key_to_float(key)
    enough = thr >= -F32_MAX
    thr = jnp.where(enough, thr, -F32_MAX)
    excess = jnp.max(jnp.where((c_ge > topk) & enough & row_ok, 1.0, 0.0)) > 0.5

    def tie_search():
        if tie_fn is not None:
            return tie_fn(thr, c_ge)
        need = topk - cnt(lambda x, kp, rs: x > thr[rs])

        def jbody(i, p):
            cand = p | jnp.left_shift(jnp.int32(1), idx_bits - 1 - i)
            c = cnt(lambda x, kp, rs: (x == thr[rs]) & (kp < cand[rs]))
            return jnp.where(c < need, cand, p)
        return lax.fori_loop(0, idx_bits, jbody, jnp.zeros((rg, LANES), I32))

    jmax = lax.cond(excess, tie_search, lambda: jnp.full((rg, LANES), INT_MAX, I32))
    return thr, jmax


def _tie_bound_blocked(sc_ref, cnt_ref, nkb, r0, rg, tk, topk, thr, c_ge):
    reps = tk // LANES
    thr_t = _tile_lanes(thr, reps)

    group = 8

    def count_body(i, carry):
        for g in range(group):
            kb = jnp.minimum(group * i + g, nkb - 1)
            eq = jnp.where(sc_ref[kb, r0:r0 + rg, :] == thr_t, 1.0, 0.0)
            c = sum(eq[:, j * LANES:(j + 1) * LANES] for j in range(reps))
            cnt_ref[kb] = jnp.broadcast_to(jnp.sum(c, axis=1, keepdims=True), (rg, LANES))
        return carry

    lax.fori_loop(0, (nkb + group - 1) // group, count_body, 0)
    ties = lax.fori_loop(0, nkb, lambda kb, tot: tot + cnt_ref[kb], jnp.zeros((rg, LANES), F32))
    need = topk - (c_ge - ties)

    def locate_body(kb, carry):
        run, kstar, before = carry
        new_run = run + cnt_ref[kb]
        hit = (run < need) & (new_run >= need)
        return new_run, jnp.where(hit, kb, kstar), jnp.where(hit, run, before)

    zeros = jnp.zeros((rg, LANES), F32)
    _, kstar, before = lax.fori_loop(0, nkb, locate_body, (zeros, jnp.full((rg, LANES), -1, I32), zeros))
    kstar_t = _tile_lanes(kstar, reps)

    def gather_body(kb, own):
        eq = jnp.where(sc_ref[kb, r0:r0 + rg, :] == thr_t, 1.0, 0.0)
        return jnp.where(kstar_t == kb, eq, own)

    own = lax.fori_loop(0, nkb, gather_body, jnp.zeros((rg, tk), F32))
    need_in = need - before
    upper = jnp.where(lax.broadcasted_iota(I32, (tk, tk), 0) <= lax.broadcasted_iota(I32, (tk, tk), 1), 1.0, 0.0)
    running = jnp.dot(own.astype(BF16), upper.astype(BF16), preferred_element_type=F32)
    p = jnp.sum(jnp.where(running < _tile_lanes(need_in, reps), 1.0, 0.0), axis=1, keepdims=True).astype(I32)
    return jnp.where(kstar >= 0, kstar * tk + p, INT_MAX)


def _select(sc_ref, cnt_ref, t_ref, j_ref, nkb, rows, rg, tk, topk, idx_bits):
    groups = range(0, rows, rg)

    def cnt(pred):
        acc = jnp.concatenate([_count(sc_ref, nkb, r0, rg, tk, pred) for r0 in groups], axis=0)
        return jnp.broadcast_to(jnp.sum(acc, axis=1, keepdims=True), (rows, LANES))

    def tie_fn(thr, c_ge):
        return jnp.concatenate(
            [_tie_bound_blocked(sc_ref, cnt_ref, nkb, r0, rg, tk, topk, thr[r0:r0 + rg], c_ge[r0:r0 + rg])
             for r0 in groups], axis=0)

    thr, jmax = _select_rows(cnt, rows, topk, idx_bits, True, tie_fn)
    t_ref[...] = thr
    j_ref[...] = jmax


def _sel_mask(scb, kpos, thr, jmax):
    return (scb > thr) | ((scb == thr) & (kpos <= jmax))


def _prompt_attn_kernel(q_ref, qi_ref, kiwi_ref, kt_ref, v_ref, kit_ref, o_ref,
                        sc_ref, cnt_ref, wib_ref, m_ref, l_ref, acc_ref, t_ref, j_ref,
                        *, tq, topk, idx_bits):
    tk = tq
    reps = tk // LANES
    i = pl.program_id(1)
    nkb = i + 1

    kiwi = kiwi_ref[...]
    for h in range(IDX_HEADS):
        wib_ref[h] = jnp.broadcast_to(kiwi[:, IDX_DIM + h:IDX_DIM + h + 1], (tq, LANES))

    def score_body(kb, carry):
        kit = kit_ref[kb]
        acc = None
        for h in range(IDX_HEADS):
            s = jnp.dot(qi_ref[:, h * LANES:(h + 1) * LANES], kit, preferred_element_type=F32)
            t = jnp.maximum(s, 0.0) * _tile_lanes(wib_ref[h], reps)
            acc = t if acc is None else acc + t
        sc_ref[kb] = acc
        return carry

    lax.fori_loop(0, nkb, score_body, 0)
    row = lax.broadcasted_iota(I32, (tq, tk), 0)
    col = lax.broadcasted_iota(I32, (tq, tk), 1)
    sc_ref[i] = jnp.where(col <= row, sc_ref[i], -jnp.inf)

    _select(sc_ref, cnt_ref, t_ref, j_ref, nkb, tq, min(128, tq), tk, topk, idx_bits)

    def bias_body(kb, carry):
        thr = _tile_lanes(t_ref[...], reps)
        jmax = _tile_lanes(j_ref[...], reps)
        sc_ref[kb] = jnp.where(_sel_mask(sc_ref[kb], kb * tk + col, thr, jmax), 0.0, NEG_BIG)
        return carry

    lax.fori_loop(0, nkb, bias_body, 0)

    m_ref[...] = jnp.full(m_ref.shape, NEG_BIG, F32)
    l_ref[...] = jnp.zeros(l_ref.shape, F32)
    acc_ref[...] = jnp.zeros(acc_ref.shape, F32)

    def attn_body(kb, carry):
        kt = kt_ref[kb]
        vb = v_ref[pl.ds(pl.multiple_of(kb * tk, tk), tk), :]
        bias = sc_ref[kb]
        for h in range(N_HEADS):
            s = jnp.dot(q_ref[h], kt, preferred_element_type=F32) + bias
            m_prev = m_ref[h]
            m_new = jnp.maximum(m_prev, jnp.max(s, axis=1, keepdims=True))
            alpha = jnp.exp2(m_prev - m_new)
            p = jnp.exp2(s - _tile_lanes(m_new, reps))
            l_ref[h] = alpha * l_ref[h] + sum(p[:, j * LANES:(j + 1) * LANES] for j in range(reps))
            acc_ref[h] = alpha * acc_ref[h] + jnp.dot(p.astype(BF16), vb, preferred_element_type=F32)
            m_ref[h] = m_new
        return carry

    lax.fori_loop(0, nkb, attn_body, 0)

    for h in range(N_HEADS):
        g = h // HEADS_PER_KV
        o = acc_ref[h] / jnp.sum(l_ref[h], axis=1, keepdims=True)
        o_ref[:, h * HEAD_DIM:(h + 1) * HEAD_DIM] = o[:, g * HEAD_DIM:(g + 1) * HEAD_DIM]


def _prompt_attn(q, qi, kiwi, kt, v, kit, tq, topk):
    b, s, _ = qi.shape
    nb = s // tq
    idx_bits = int(s).bit_length()
    blk = lambda w: pl.BlockSpec((None, tq, w), lambda bi, i: (bi, i, 0))
    full = lambda shp: pl.BlockSpec((None,) + shp, lambda bi, i: (bi,) + (0,) * len(shp),
                                    pipeline_mode=pl.Buffered(1))
    return pl.pallas_call(
        functools.partial(_prompt_attn_kernel, tq=tq, topk=topk, idx_bits=idx_bits),
        grid=(b, nb),
        in_specs=[pl.BlockSpec((N_HEADS, tq, LANES), lambda bi, i: (0, bi * nb + i, 0)),
                  blk(IDX_HEADS * LANES), blk(LANES),
                  full((nb, KV_WIDTH, tq)), full((s, KV_WIDTH)), full((nb, 2 * IDX_DIM, tq))],
        out_specs=blk(ATTN_WIDTH),
        out_shape=jax.ShapeDtypeStruct((b, s, ATTN_WIDTH), F32),
        scratch_shapes=[pltpu.VMEM((nb, tq, tq), F32),
                        pltpu.VMEM((nb, min(128, tq), LANES), F32),
                        pltpu.VMEM((IDX_HEADS, tq, LANES), F32),
                        pltpu.VMEM((N_HEADS, tq, LANES), F32),
                        pltpu.VMEM((N_HEADS, tq, LANES), F32),
                        pltpu.VMEM((N_HEADS, tq, KV_WIDTH), F32),
                        pltpu.VMEM((tq, LANES), F32),
                        pltpu.VMEM((tq, LANES), I32)],
        compiler_params=_cparams(("arbitrary", "arbitrary")),
        name="prompt_attn",
    )(q, qi, kiwi, kt, v, kit)


def _sample_attn_kernel(pt_ref, q_ref, qih_ref, wib_ref, kin_ref, kn_ref, vn_ref, *rest,
                        n_steps, pps, n_tok, tpad, page, topk, idx_bits):
    cik_refs, ck_refs, cv_refs = rest[:pps], rest[pps:2 * pps], rest[2 * pps:3 * pps]
    o_ref, sc_ref, sa_ref, vs_ref = rest[3 * pps:]
    step = pl.program_id(1)
    n_pages = n_steps * pps
    nkb = n_pages + 1
    qih = qih_ref[...]
    wib = wib_ref[...]
    q = q_ref[...]

    def idx_score(s):
        acc = None
        for h in range(IDX_HEADS):
            t = jnp.maximum(s[h * tpad:(h + 1) * tpad], 0.0) * wib[h * tpad:(h + 1) * tpad]
            acc = t if acc is None else acc + t
        return acc

    for j in range(pps):
        pg = step * pps + j
        sc_ref[pg] = idx_score(jnp.dot(qih, cik_refs[j][...].astype(BF16), preferred_element_type=F32))
        sa_ref[pg] = jnp.dot(q, ck_refs[j][...].astype(BF16), preferred_element_type=F32)
        vs_ref[pg] = cv_refs[j][...].astype(BF16)

    @pl.when(step == n_steps - 1)
    def _():
        s_new = idx_score(jnp.dot(qih, kin_ref[...], preferred_element_type=F32))
        tok = lax.broadcasted_iota(I32, (tpad, page), 0)
        col = lax.broadcasted_iota(I32, (tpad, page), 1)
        sc_ref[n_pages] = jnp.where((col <= tok) & (col < n_tok), s_new, -jnp.inf)
        sa_ref[n_pages] = jnp.dot(q, kn_ref[...], preferred_element_type=F32)
        vs_ref[n_pages] = vn_ref[...]

        kpos = (lax.broadcasted_iota(I32, (nkb, tpad, page), 0) * page
                + lax.broadcasted_iota(I32, (nkb, tpad, page), 2))

        def cnt(pred):
            w = jnp.where(pred(sc_ref[...], kpos, slice(None)), 1.0, 0.0)
            chunk = -(-nkb // SUBLANES)
            c = sum(jnp.sum(w[a:a + chunk], axis=0) for a in range(0, nkb, chunk))
            return jnp.broadcast_to(jnp.sum(c, axis=1, keepdims=True), c.shape)

        thr, jmax = _select_rows(cnt, tpad, topk, idx_bits, tok[:, :LANES] < n_tok, two_bits=True)
        sc_ref[...] = jnp.where(_sel_mask(sc_ref[...], kpos, thr[None], jmax[None]), 0.0, NEG_BIG)

        heads = N_HEADS
        inv_l = []
        for t in range(n_tok):
            s = sa_ref[:, t * heads:(t + 1) * heads, :] + sc_ref[:, t:t + 1, :]
            m = jnp.max(jnp.max(s, axis=0), axis=1, keepdims=True)
            pr = jnp.exp2(s - m[None])
            sa_ref[:, t * heads:(t + 1) * heads, :] = pr
            l = jnp.sum(jnp.sum(pr, axis=0), axis=1, keepdims=True)
            inv_l.append(jnp.broadcast_to(1.0 / l, (heads, KV_WIDTH)))

        p_all = jnp.concatenate([sa_ref[kb].astype(BF16) for kb in range(nkb)], axis=1)
        vt_all = jnp.concatenate([vs_ref[kb] for kb in range(nkb)], axis=1)
        acc = lax.dot_general(p_all, vt_all, NT_DIMS, preferred_element_type=F32)
        o_ref[...] = acc * jnp.concatenate(inv_l, axis=0)


def _sample_attn(page_table, q, qih, wib, kin_t, kn_t, vn_t, cik_t, ck_t, cv_t, n_tok, tpad, topk, pps):
    n, n_pages = page_table.shape
    page = ck_t.shape[2]
    assert page == LANES and KV_WIDTH == LANES and n_pages % pps == 0
    n_steps = n_pages // pps
    idx_bits = int((n_pages + 1) * page).bit_length()
    rows = n_tok * N_HEADS
    seq = lambda r, w: pl.BlockSpec((None, r, w), lambda b, p, pt: (b, 0, 0))

    def paged(r, j):
        return pl.BlockSpec((None, r, page), lambda b, p, pt: (pt[b, p * pps + j], 0, 0))

    grid_spec = pltpu.PrefetchScalarGridSpec(
        num_scalar_prefetch=1,
        grid=(n, n_steps),
        in_specs=([seq(rows, KV_WIDTH), seq(IDX_HEADS * tpad, IDX_DIM), seq(IDX_HEADS * tpad, LANES),
                   seq(IDX_DIM, page), seq(KV_WIDTH, page), seq(KV_WIDTH, page)]
                  + [paged(IDX_DIM, j) for j in range(pps)]
                  + [paged(KV_WIDTH, j) for j in range(pps)]
                  + [paged(KV_WIDTH, j) for j in range(pps)]),
        out_specs=seq(rows, KV_WIDTH),
        scratch_shapes=[pltpu.VMEM((n_pages + 1, tpad, page), F32),
                        pltpu.VMEM((n_pages + 1, rows, page), F32),
                        pltpu.VMEM((n_pages + 1, KV_WIDTH, page), BF16)],
    )
    return pl.pallas_call(
        functools.partial(_sample_attn_kernel, n_steps=n_steps, pps=pps, n_tok=n_tok, tpad=tpad, page=page,
                          topk=topk, idx_bits=idx_bits),
        grid_spec=grid_spec,
        out_shape=jax.ShapeDtypeStruct((n, rows, KV_WIDTH), F32),
        compiler_params=_cparams(("arbitrary", "arbitrary")),
        name="sample_attn",
    )(page_table, q, qih, wib, kin_t, kn_t, vn_t, *([cik_t] * pps), *([ck_t] * pps), *([cv_t] * pps))


def _mix_out_tail(x, o, gb, u0, um1, um2, wc_ref, wo_ref, gt, gpost, aw):
    y = wc_ref[0:1, :] * um2 + wc_ref[1:2, :] * um1 + wc_ref[2:3, :] * u0
    mixed = (jnp.dot(o.astype(BF16), wo_ref[0:aw, :], preferred_element_type=F32)
             + jnp.dot((gb * y).astype(BF16), wo_ref[aw:, :], preferred_element_type=F32))
    return x + gt * (_rms(mixed) * gpost)


def _mix_out_prompt_kernel(x_ref, o_ref_in, gb_ref, u_ref, halo_ref, gt_ref, gpost_ref, wc_ref, wo_ref,
                           out_ref, ext_ref, *, tm, tiles_per_seq):
    i = pl.program_id(0)
    halo = halo_ref[...]
    ext_ref[0:SUBLANES, :] = jnp.where(i % tiles_per_seq == 0, jnp.zeros_like(halo), halo)
    ext_ref[SUBLANES:, :] = u_ref[...]
    um2 = ext_ref[SUBLANES - 2:SUBLANES - 2 + tm, :]
    um1 = ext_ref[SUBLANES - 1:SUBLANES - 1 + tm, :]
    out_ref[...] = _mix_out_tail(x_ref[...], o_ref_in[...], gb_ref[...], u_ref[...], um1, um2,
                                 wc_ref, wo_ref, gt_ref[0], gpost_ref[...], ATTN_WIDTH)


def _mix_out_prompt(x, o, gb, u, gate, g_post, w_conv, w_o, tm, tiles_per_seq):
    rows, d = x.shape
    cw = u.shape[1]
    const = lambda i: (0, 0)
    row = lambda i: (i, 0)
    hb = tm // SUBLANES
    return pl.pallas_call(
        functools.partial(_mix_out_prompt_kernel, tm=tm, tiles_per_seq=tiles_per_seq),
        grid=(rows // tm,),
        in_specs=[pl.BlockSpec((tm, d), row),
                  pl.BlockSpec((tm, ATTN_WIDTH), row),
                  pl.BlockSpec((tm, cw), row),
                  pl.BlockSpec((tm, cw), row),
                  pl.BlockSpec((SUBLANES, cw), lambda i: (jnp.maximum(i * hb - 1, 0), 0)),
                  _mod_spec(gate, tm, tiles_per_seq),
                  pl.BlockSpec((1, d), const),
                  pl.BlockSpec(w_conv.shape, const),
                  pl.BlockSpec(w_o.shape, const)],
        out_specs=pl.BlockSpec((tm, d), row),
        out_shape=jax.ShapeDtypeStruct((rows, d), F32),
        scratch_shapes=[pltpu.VMEM((tm + SUBLANES, cw), F32)],
        compiler_params=_cparams(("arbitrary",)),
        name="mix_out_prompt",
    )(x, o, gb, u, u, gate, g_post, w_conv, w_o)


def _mix_out_sample_kernel(x_ref, o_ref_in, gb_ref, um2_ref, um1_ref, u0_ref, gt_ref, gpost_ref, wc_ref, wo_ref,
                           out_ref):
    out_ref[...] = _mix_out_tail(x_ref[...], o_ref_in[...], gb_ref[...], u0_ref[...], um1_ref[...], um2_ref[...],
                                 wc_ref, wo_ref, gt_ref[0], gpost_ref[...], ATTN_WIDTH)


def _mix_out_sample(x, o, gb, ext, gate, g_post, w_conv, w_o, tm):
    rows, d = x.shape
    cw = gb.shape[1]
    const = lambda i: (0, 0)
    row = lambda i: (i, 0)
    return pl.pallas_call(
        _mix_out_sample_kernel,
        grid=(rows // tm,),
        in_specs=[pl.BlockSpec((tm, d), row),
                  pl.BlockSpec((tm, ATTN_WIDTH), row),
                  pl.BlockSpec((tm, cw), row),
                  pl.BlockSpec((tm, cw), lambda i: (i, 0)),
                  pl.BlockSpec((tm, cw), lambda i: (i + 1, 0)),
                  pl.BlockSpec((tm, cw), lambda i: (i + 2, 0)),
                  pl.BlockSpec((1, tm, d), lambda i: (0, 0, 0)),
                  pl.BlockSpec((1, d), const),
                  pl.BlockSpec(w_conv.shape, const),
                  pl.BlockSpec(w_o.shape, const)],
        out_specs=pl.BlockSpec((tm, d), row),
        out_shape=jax.ShapeDtypeStruct((rows, d), F32),
        compiler_params=_cparams(("arbitrary",)),
        name="mix_out_sample",
    )(x, o, gb, ext, ext, ext, gate, g_post, w_conv, w_o)


def _prompt_layer(x, mod, lw, tm, tq):
    b, s, d = x.shape
    conv_ch = lw["w_conv"].shape[1]
    tps = s // tm
    md = lambda j, k: mod[:, j, k][:, None, :]
    xr = x.reshape(b * s, d)
    x1 = _ffn(xr, md(0, 0), md(0, 1), md(0, 2), lw["g_pre"][0:1], lw["g_post"][0:1], lw["ffn1_w13"], lw["ffn1_w2"],
              tm, tps)
    tabs = _rope_tables(jnp.arange(s))
    q, k, v, qi, kiwi, gb, u = _mix_in(x1, md(1, 0), md(1, 1), lw["g_pre"][1:2], lw["w_in_p"], tabs, tm, tps, tps,
                                       conv_ch)
    nb = s // tq
    kt = jnp.swapaxes(k.astype(BF16).reshape(b, nb, tq, KV_WIDTH), 2, 3)
    kit = jnp.swapaxes(kiwi[:, :IDX_DIM].astype(BF16).reshape(b, nb, tq, IDX_DIM), 2, 3)
    kit = jnp.concatenate([kit, kit], axis=2)
    topk = min(TOPK_MAX, s // 4)
    o = _prompt_attn(q, qi.reshape(b, s, -1), kiwi.reshape(b, s, -1), kt,
                     v.astype(BF16).reshape(b, s, KV_WIDTH), kit, tq, topk)
    x2 = _mix_out_prompt(x1, o.reshape(b * s, -1), gb, u, md(1, 2), lw["g_post"][1:2], lw["w_conv"], lw["w_o"],
                         tm, tps)
    x3 = _ffn(x2, md(2, 0), md(2, 1), md(2, 2), lw["g_pre"][2:3], lw["g_post"][2:3], lw["ffn2_w13"], lw["ffn2_w2"],
              tm, tps)
    state = (k.reshape(b, s, N_KV_HEADS, HEAD_DIM), v.reshape(b, s, N_KV_HEADS, HEAD_DIM),
             kiwi[:, :IDX_DIM].reshape(b, s, IDX_DIM), u.reshape(b, s, conv_ch)[:, s - (CONV_K - 1):])
    return x3.reshape(b, s, d), state


def _sample_layer(x_tm, mod, lw, cache_k, cache_v, cache_idx_k, state_conv, page_table):
    n = mod.shape[0]
    t_tok = x_tm.shape[0] // n
    d = x_tm.shape[1]
    conv_ch = lw["w_conv"].shape[1]
    n_pages = page_table.shape[1]
    page = cache_k.shape[1]
    past = n_pages * page
    md = lambda j, k: mod[:, j, k][None]
    x1 = _ffn(x_tm, md(0, 0), md(0, 1), md(0, 2), lw["g_pre"][0:1], lw["g_post"][0:1], lw["ffn1_w13"],
              lw["ffn1_w2"], n, t_tok)
    pos = jnp.repeat(past + jnp.arange(t_tok), n)
    tabs = _rope_tables(pos)
    q, k, v, qi, kiwi, gb, u = _mix_in(x1, md(1, 0), md(1, 1), lw["g_pre"][1:2], lw["w_in_p"], tabs, n, t_tok, t_tok,
                                       conv_ch)
    tpad = SUBLANES
    assert t_tok <= tpad

    def seq_major(a, heads, width):
        a = a.reshape(t_tok, n, heads, width).transpose(1, 2, 0, 3)
        a = jnp.pad(a, ((0, 0), (0, 0), (0, tpad - t_tok), (0, 0)))
        return a.reshape(n, heads * tpad, width)

    q_seq = q.reshape(N_HEADS, t_tok, n, LANES).transpose(2, 1, 0, 3).reshape(n, t_tok * N_HEADS, LANES)
    qi_nat = jnp.concatenate([qi[:, h * LANES + (h % 2) * IDX_DIM:h * LANES + (h % 2 + 1) * IDX_DIM]
                              for h in range(IDX_HEADS)], axis=1)
    qih = seq_major(qi_nat, IDX_HEADS, IDX_DIM)
    wib = jnp.broadcast_to(seq_major(kiwi[:, IDX_DIM:IDX_DIM + IDX_HEADS], IDX_HEADS, 1),
                           (n, IDX_HEADS * tpad, LANES))

    def new_keys_t(a):
        a = a.reshape(t_tok, n, -1).transpose(1, 2, 0).astype(BF16)
        return jnp.pad(a, ((0, 0), (0, 0), (0, page - t_tok)))

    ck_t = cache_k.transpose(0, 2, 3, 1).reshape(-1, KV_WIDTH, page)
    cv_t = cache_v.transpose(0, 2, 3, 1).reshape(-1, KV_WIDTH, page)
    cik_t = cache_idx_k.transpose(0, 2, 1)
    topk = min(TOPK_MAX, (past + t_tok) // 4)
    pps = max(p for p in (16, 8, 4, 2, 1) if n_pages % p == 0)
    o = _sample_attn(page_table, q_seq, qih, wib, new_keys_t(kiwi[:, :IDX_DIM]), new_keys_t(k), new_keys_t(v),
                     cik_t, ck_t, cv_t, t_tok, tpad, topk, pps)
    o = o.reshape(n, t_tok, N_KV_HEADS, HEADS_PER_KV, N_KV_HEADS, HEAD_DIM)
    o = jnp.stack([o[:, :, g, :, g] for g in range(N_KV_HEADS)], axis=2)
    o_tm = o.transpose(1, 0, 2, 3, 4).reshape(t_tok * n, ATTN_WIDTH)
    ext = jnp.concatenate([state_conv.transpose(1, 0, 2).reshape(-1, conv_ch), u], axis=0)
    x2 = _mix_out_sample(x1, o_tm, gb, ext, md(1, 2), lw["g_post"][1:2], lw["w_conv"], lw["w_o"], n)
    x3 = _ffn(x2, md(2, 0), md(2, 1), md(2, 2), lw["g_pre"][2:3], lw["g_post"][2:3], lw["ffn2_w13"],
              lw["ffn2_w2"], n, t_tok)
    tm_to_seq = lambda a: a.reshape((t_tok, n) + a.shape[1:]).swapaxes(0, 1)
    state = (tm_to_seq(k).reshape(n, t_tok, N_KV_HEADS, HEAD_DIM), tm_to_seq(v).reshape(n, t_tok, N_KV_HEADS, HEAD_DIM),
             tm_to_seq(kiwi[:, :IDX_DIM]), tm_to_seq(u)[:, t_tok - (CONV_K - 1):])
    return x3, state


def _row_tile(s, target):
    t = min(target, s)
    while s % t:
        t //= 2
    return t


def kernel(x_prompt, x_sample, cache_k, cache_v, cache_idx_k, state_conv, page_table, c_prompt, c_sample, w_ada, b_ada, g_pre, g_post, ffn1_w13, ffn1_w2, w_in, w_conv, w_o, ffn2_w13, ffn2_w2):
    depth = w_ada.shape[0]
    b, s, d = x_prompt.shape
    n, t_tok, _ = x_sample.shape
    conv_ch = w_conv.shape[2]
    tm = _row_tile(s, 512)
    tq = _row_tile(s, 256)

    yp = x_prompt
    ys = x_sample.swapaxes(0, 1).reshape(t_tok * n, d)
    c_all = jnp.concatenate([c_prompt, c_sample], axis=0)
    n_c = c_all.shape[0]
    c_all = jnp.pad(c_all, ((0, -n_c % SUBLANES), (0, 0)))
    outs = [[] for _ in range(8)]
    for l in range(depth):
        lw = dict(g_pre=g_pre[l], g_post=g_post[l], w_conv=w_conv[l],
                  ffn1_w13=ffn1_w13[l].astype(BF16), ffn1_w2=ffn1_w2[l].astype(BF16),
                  ffn2_w13=ffn2_w13[l].astype(BF16), ffn2_w2=ffn2_w2[l].astype(BF16),
                  w_in_p=_pack_w_in(w_in[l], conv_ch), w_o=w_o[l].astype(BF16))
        mod = _ada_mod(c_all, w_ada[l], b_ada[l])[:n_c].reshape(n_c, N_SUB, 3, d)
        yp, st_p = _prompt_layer(yp, mod[:b], lw, tm, tq)
        ys, st_s = _sample_layer(ys, mod[b:], lw, cache_k[l], cache_v[l], cache_idx_k[l], state_conv[l], page_table)
        for lst, a in zip(outs, st_p + st_s):
            lst.append(a)
    ys = ys.reshape(t_tok, n, d).swapaxes(0, 1)
    return (yp, ys) + tuple(jnp.stack(lst) for lst in outs)
```

```python
import functools

import numpy as np
import jax
import jax.numpy as jnp
from jax import lax
from jax.experimental import pallas as pl
from jax.experimental.pallas import tpu as pltpu

F32 = jnp.float32
BF16 = jnp.bfloat16
I32 = jnp.int32

N_HEADS = 8
HEAD_DIM = 64
N_KV_HEADS = 2
HEADS_PER_KV = N_HEADS // N_KV_HEADS
ATTN_WIDTH = N_HEADS * HEAD_DIM
KV_WIDTH = N_KV_HEADS * HEAD_DIM
ROT_DIV = 4
ROT = HEAD_DIM // ROT_DIV
ROT_HALF = ROT // 2
ROPE_THETA = 500000.0
IDX_HEADS = 4
IDX_DIM = 64
IDX_SCALE = (IDX_HEADS * IDX_DIM) ** -0.5
TOPK_MAX = 256
CONV_K = 3
N_SUB = 3
RMS_EPS = 1e-6
Q_SCALE = HEAD_DIM ** -0.5 * float(np.log2(np.e))

LANES = 128
SUBLANES = 8
VMEM_LIMIT = 56 * 1024 * 1024

NEG_BIG = -1e30
F32_MAX = float(np.finfo(np.float32).max)
INT_MIN = -(2 ** 31)
INT_MAX = 2 ** 31 - 1

NT_DIMS = (((1,), (1,)), ((), ()))


def _cparams(sem):
    return pltpu.CompilerParams(dimension_semantics=sem, vmem_limit_bytes=VMEM_LIMIT)


def _silu(x):
    return x * (1.0 / (1.0 + jnp.exp(-x)))


def _rms(x):
    return x * lax.rsqrt(jnp.mean(x * x, axis=-1, keepdims=True) + RMS_EPS)


def _tile_lanes(x, reps):
    return x if reps == 1 else jnp.concatenate([x] * reps, axis=1)


def _ada_kernel(c_ref, w_ref, b_ref, o_ref):
    a = _silu(c_ref[...]).astype(BF16)
    o_ref[...] = jnp.dot(a, w_ref[...].astype(BF16), preferred_element_type=F32) + b_ref[...]


def _ada_mod(c, w_ada, b_ada):
    n, d = c.shape
    nout = w_ada.shape[1]
    tn = 9 * LANES if nout % (9 * LANES) == 0 else LANES
    return pl.pallas_call(
        _ada_kernel,
        grid=(nout // tn,),
        in_specs=[pl.BlockSpec((n, d), lambda j: (0, 0)),
                  pl.BlockSpec((d, tn), lambda j: (0, j)),
                  pl.BlockSpec((1, tn), lambda j: (0, j))],
        out_specs=pl.BlockSpec((n, tn), lambda j: (0, j)),
        out_shape=jax.ShapeDtypeStruct((n, nout), F32),
        compiler_params=_cparams(("arbitrary",)),
        name="ada_mod",
    )(c, w_ada, b_ada.reshape(1, nout))


def _ffn_kernel(x_ref, sh_ref, sc_ref, gt_ref, gpre_ref, gpost_ref, w13_ref, w2_ref, o_ref, *, d_ff, chunk):
    x = x_ref[...]
    h = _rms(x) * gpre_ref[...] * (1.0 + sc_ref[0]) + sh_ref[0]
    hb = h.astype(BF16)
    y = jnp.zeros(x.shape, F32)
    for c in range(d_ff // chunk):
        a = jnp.dot(hb, w13_ref[:, c * chunk:(c + 1) * chunk], preferred_element_type=F32)
        b = jnp.dot(hb, w13_ref[:, d_ff + c * chunk:d_ff + (c + 1) * chunk], preferred_element_type=F32)
        g = (_silu(a) * b).astype(BF16)
        y = y + jnp.dot(g, w2_ref[c * chunk:(c + 1) * chunk, :], preferred_element_type=F32)
    o_ref[...] = x + 0.5 * gt_ref[0] * (_rms(y) * gpost_ref[...])


def _mod_spec(mod, tm, tiles_per_group):
    _, r, d = mod.shape
    return pl.BlockSpec((1, r, d), lambda i: (i // tiles_per_group, 0, 0))


def _ffn(x, shift, scale, gate, g_pre, g_post, w13, w2, tm, tiles_per_group):
    rows, d = x.shape
    d_ff = w2.shape[0]
    const = lambda i: (0, 0)
    return pl.pallas_call(
        functools.partial(_ffn_kernel, d_ff=d_ff, chunk=min(512, d_ff)),
        grid=(rows // tm,),
        in_specs=[pl.BlockSpec((tm, d), lambda i: (i, 0)),
                  _mod_spec(shift, tm, tiles_per_group),
                  _mod_spec(scale, tm, tiles_per_group),
                  _mod_spec(gate, tm, tiles_per_group),
                  pl.BlockSpec((1, d), const),
                  pl.BlockSpec((1, d), const),
                  pl.BlockSpec(w13.shape, const),
                  pl.BlockSpec(w2.shape, const)],
        out_specs=pl.BlockSpec((tm, d), lambda i: (i, 0)),
        out_shape=jax.ShapeDtypeStruct((rows, d), F32),
        compiler_params=_cparams(("arbitrary",)),
        name="ffn",
    )(x, shift, scale, gate, g_pre, g_post, w13, w2)


_OFF_Q = 0
_OFF_K = _OFF_Q + ATTN_WIDTH
_OFF_V = _OFF_K + KV_WIDTH
_OFF_QI = _OFF_V + KV_WIDTH
_OFF_KIWI = _OFF_QI + IDX_HEADS * IDX_DIM
_OFF_GB = _OFF_KIWI + LANES


def _pack_w_in(w_in, conv_ch):
    splits = [ATTN_WIDTH, KV_WIDTH, KV_WIDTH, IDX_HEADS * IDX_DIM, IDX_DIM, IDX_HEADS, conv_ch, conv_ch, conv_ch]
    offs = np.cumsum([0] + splits)
    d = w_in.shape[0]
    pad = jnp.zeros((d, LANES - IDX_DIM - IDX_HEADS), w_in.dtype)
    return jnp.concatenate([w_in[:, :offs[6]], pad, w_in[:, offs[6]:]], axis=1).astype(BF16)


def _rope_tables(pos):
    inv = jnp.power(ROPE_THETA, -jnp.arange(ROT_HALF, dtype=F32) * 2.0 / ROT)
    ang = pos.astype(F32)[:, None] * inv[None, :]
    cos, sin = jnp.cos(ang), jnp.sin(ang)
    n = pos.shape[0]
    one = jnp.ones((n, HEAD_DIM - ROT), F32)
    zero_r = jnp.zeros((n, HEAD_DIM - ROT), F32)
    zero_h = jnp.zeros((n, ROT_HALF), F32)
    c = jnp.concatenate([cos, cos, one], axis=1)
    s1 = jnp.concatenate([-sin, zero_h, zero_r], axis=1)
    s2 = jnp.concatenate([zero_h, sin, zero_r], axis=1)
    rep = LANES // HEAD_DIM
    return jnp.tile(c, (1, rep)), jnp.tile(s1, (1, rep)), jnp.tile(s2, (1, rep))


def _rope128(x, c, s1, s2):
    return x * c + pltpu.roll(x, LANES - ROT_HALF, 1) * s1 + pltpu.roll(x, ROT_HALF, 1) * s2


def _mix_in_kernel(x_ref, sh_ref, sc_ref, gpre_ref, w_ref, rc_ref, rs1_ref, rs2_ref,
                   q_ref, kt_ref, vt_ref, vb_ref, qi_ref, kiwi_ref, kiwit_ref, gb_ref, u_ref, *, conv_ch):
    x = x_ref[...]
    hb = (_rms(x) * gpre_ref[...] * (1.0 + sc_ref[0]) + sh_ref[0]).astype(BF16)
    c, s1, s2 = rc_ref[...], rs1_ref[...], rs2_ref[...]

    def proj(off, width):
        return jnp.dot(hb, w_ref[:, off:off + width], preferred_element_type=F32)

    def rope_cols(z):
        return [_rope128(z[:, g * LANES:(g + 1) * LANES], c, s1, s2) for g in range(z.shape[1] // LANES)]

    lane = lax.broadcasted_iota(I32, (x.shape[0], LANES), 1)
    low = lane < HEAD_DIM

    zq = proj(_OFF_Q, ATTN_WIDTH)
    for g, r in enumerate(rope_cols(zq)):
        r = r * Q_SCALE
        swapped = pltpu.roll(r, HEAD_DIM, 1)
        for e in range(2):
            h = 2 * g + e
            kv_group = h // HEADS_PER_KV
            src = r if e == kv_group else swapped
            keep = low if kv_group == 0 else jnp.logical_not(low)
            q_ref[h] = jnp.where(keep, src, 0.0).astype(BF16)
    (kr,) = rope_cols(proj(_OFF_K, KV_WIDTH))
    kt_ref[...] = kr.T
    pv = proj(_OFF_V, KV_WIDTH)
    vt_ref[...] = pv.T
    vb_ref[...] = pv.astype(BF16)
    zqi = proj(_OFF_QI, IDX_HEADS * IDX_DIM)
    for g, r in enumerate(rope_cols(zqi)):
        qi_ref[:, (2 * g) * LANES:(2 * g + 1) * LANES] = jnp.where(low, r, 0.0).astype(BF16)
        qi_ref[:, (2 * g + 1) * LANES:(2 * g + 2) * LANES] = jnp.where(low, 0.0, r).astype(BF16)
    zkw = proj(_OFF_KIWI, LANES)
    kiwi = jnp.where(lane < IDX_DIM, _rope128(zkw, c, s1, s2), zkw * IDX_SCALE)
    kiwi_ref[...] = kiwi
    kiwit_ref[...] = kiwi.T
    gb_ref[...] = proj(_OFF_GB, conv_ch)
    u_ref[...] = proj(_OFF_GB + conv_ch, conv_ch) * proj(_OFF_GB + 2 * conv_ch, conv_ch)


def _mix_in(x, shift, scale, g_pre, w_in_p, rope_tabs, tm, tiles_per_group, rope_tiles, conv_ch):
    rows, d = x.shape
    const = lambda i: (0, 0)
    row = lambda i: (i, 0)
    rope_spec = pl.BlockSpec((tm, LANES), lambda i: (i % rope_tiles, 0))
    assert 2 * HEAD_DIM == LANES and KV_WIDTH == LANES and 2 * IDX_DIM == LANES
    groups = rows // (tm * tiles_per_group)
    flat = lambda w, dt: (pl.BlockSpec((tm, w), row), jax.ShapeDtypeStruct((rows, w), dt))
    transposed = lambda w: (pl.BlockSpec((None, w, tm), lambda i: (i // tiles_per_group, 0, i % tiles_per_group)),
                            jax.ShapeDtypeStruct((groups, w, tm * tiles_per_group), F32))
    q_out = (pl.BlockSpec((N_HEADS, tm, LANES), lambda i: (0, i, 0)),
             jax.ShapeDtypeStruct((N_HEADS, rows, LANES), BF16))
    outs = [q_out, transposed(KV_WIDTH), transposed(KV_WIDTH), flat(KV_WIDTH, BF16), flat(IDX_HEADS * LANES, BF16),
            flat(LANES, F32), transposed(LANES), flat(conv_ch, F32), flat(conv_ch, F32)]
    return pl.pallas_call(
        functools.partial(_mix_in_kernel, conv_ch=conv_ch),
        grid=(rows // tm,),
        in_specs=[pl.BlockSpec((tm, d), row),
                  _mod_spec(shift, tm, tiles_per_group),
                  _mod_spec(scale, tm, tiles_per_group),
                  pl.BlockSpec((1, d), const),
                  pl.BlockSpec(w_in_p.shape, const),
                  rope_spec, rope_spec, rope_spec],
        out_specs=[spec for spec, _ in outs],
        out_shape=[shape for _, shape in outs],
        compiler_params=_cparams(("arbitrary",)),
        name="mix_in",
    )(x, shift, scale, g_pre, w_in_p, *rope_tabs)


def _key_to_float(u):
    o = u ^ INT_MIN
    bits = jnp.where(o >= 0, o, o ^ INT_MAX)
    return lax.bitcast_convert_type(bits, F32)


def _count(sc_ref, nkb, r0, rg, tk, pred):
    rows = slice(r0, r0 + rg)

    def body(kb, acc):
        for j in range(tk // LANES):
            x = sc_ref[kb, rows, j * LANES:(j + 1) * LANES]
            kpos = kb * tk + j * LANES + lax.broadcasted_iota(I32, (rg, LANES), 1)
            acc = acc + jnp.where(pred(x, kpos, rows), 1.0, 0.0)
        return acc

    group = 4

    def group_body(i, acc):
        for g in range(group):
            acc = body(group * i + g, acc)
        return acc

    acc = lax.fori_loop(0, nkb // group, group_body, jnp.zeros((rg, LANES), F32))
    return lax.fori_loop(group * (nkb // group), nkb, body, acc)


def _select_rows(cnt, rg, topk, idx_bits, row_ok, tie_fn=None, two_bits=False):
    def count_ge(cand):
        t = _key_to_float(cand)
        return cnt(lambda x, kp, rs: x >= t[rs])

    def bit_body(i, carry):
        key, c_ge = carry
        cand = key | jnp.left_shift(jnp.int32(1), 31 - i)
        c = count_ge(cand)
        ok = c >= topk
        return jnp.where(ok, cand, key), jnp.where(ok, c, c_ge)

    def two_bit_body(i, carry):
        key, c_ge = carry
        sh = 30 - 2 * i
        c1, c2, c3 = (key | jnp.left_shift(jnp.int32(v), sh) for v in (2, 1, 3))
        n1, n2, n3 = count_ge(c1), count_ge(c2), count_ge(c3)
        ok1, ok2, ok3 = n1 >= topk, n2 >= topk, n3 >= topk
        key = jnp.where(ok3, c3, jnp.where(ok1, c1, jnp.where(ok2, c2, key)))
        c_ge = jnp.where(ok3, n3, jnp.where(ok1, n1, jnp.where(ok2, n2, c_ge)))
        return key, c_ge

    init = (jnp.zeros((rg, LANES), I32), jnp.zeros((rg, LANES), F32))
    key, c_ge = lax.fori_loop(0, 16, two_bit_body, init) if two_bits else lax.fori_loop(0, 32, bit_body, init)
    thr = _key_to_float(key)
    enough = thr >= -F32_MAX
    thr = jnp.where(enough, thr, -F32_MAX)
    excess = jnp.max(jnp.where((c_ge > topk) & enough & row_ok, 1.0, 0.0)) > 0.5

    def tie_search():
        if tie_fn is not None:
            return tie_fn(thr, c_ge)
        need = topk - cnt(lambda x, kp, rs: x > thr[rs])

        def jbody(i, p):
            cand = p | jnp.left_shift(jnp.int32(1), idx_bits - 1 - i)
            c = cnt(lambda x, kp, rs: (x == thr[rs]) & (kp < cand[rs]))
            return jnp.where(c < need, cand, p)
        return lax.fori_loop(0, idx_bits, jbody, jnp.zeros((rg, LANES), I32))

    jmax = lax.cond(excess, tie_search, lambda: jnp.full((rg, LANES), INT_MAX, I32))
    return thr, jmax


def _tie_bound_blocked(sc_ref, cnt_ref, nkb, r0, rg, tk, topk, thr, c_ge):
    reps = tk // LANES
    thr_t = _tile_lanes(thr, reps)

    group = 8

    def count_body(i, carry):
        for g in range(group):
            kb = jnp.minimum(group * i + g, nkb - 1)
            eq = jnp.where(sc_ref[kb, r0:r0 + rg, :] == thr_t, 1.0, 0.0)
            c = sum(eq[:, j * LANES:(j + 1) * LANES] for j in range(reps))
            cnt_ref[kb] = jnp.broadcast_to(jnp.sum(c, axis=1, keepdims=True), (rg, LANES))
        return carry

    lax.fori_loop(0, (nkb + group - 1) // group, count_body, 0)
    ties = lax.fori_loop(0, nkb, lambda kb, tot: tot + cnt_ref[kb], jnp.zeros((rg, LANES), F32))
    need = topk - (c_ge - ties)

    def locate_body(kb, carry):
        run, kstar, before = carry
        new_run = run + cnt_ref[kb]
        hit = (run < need) & (new_run >= need)
        return new_run, jnp.where(hit, kb, kstar), jnp.where(hit, run, before)

    zeros = jnp.zeros((rg, LANES), F32)
    _, kstar, before = lax.fori_loop(0, nkb, locate_body, (zeros, jnp.full((rg, LANES), -1, I32), zeros))
    kstar_t = _tile_lanes(kstar, reps)

    def gather_body(kb, own):
        eq = jnp.where(sc_ref[kb, r0:r0 + rg, :] == thr_t, 1.0, 0.0)
        return jnp.where(kstar_t == kb, eq, own)

    own = lax.fori_loop(0, nkb, gather_body, jnp.zeros((rg, tk), F32))
    need_in = need - before
    upper = jnp.where(lax.broadcasted_iota(I32, (tk, tk), 0) <= lax.broadcasted_iota(I32, (tk, tk), 1), 1.0, 0.0)
    running = jnp.dot(own.astype(BF16), upper.astype(BF16), preferred_element_type=F32)
    p = jnp.sum(jnp.where(running < _tile_lanes(need_in, reps), 1.0, 0.0), axis=1, keepdims=True).astype(I32)
    return jnp.where(kstar >= 0, kstar * tk + p, INT_MAX)


def _select(sc_ref, cnt_ref, t_ref, j_ref, nkb, rows, rg, tk, topk, idx_bits):
    groups = range(0, rows, rg)

    def cnt(pred):
        acc = jnp.concatenate([_count(sc_ref, nkb, r0, rg, tk, pred) for r0 in groups], axis=0)
        return jnp.broadcast_to(jnp.sum(acc, axis=1, keepdims=True), (rows, LANES))

    def tie_fn(thr, c_ge):
        return jnp.concatenate(
            [_tie_bound_blocked(sc_ref, cnt_ref, nkb, r0, rg, tk, topk, thr[r0:r0 + rg], c_ge[r0:r0 + rg])
             for r0 in groups], axis=0)

    thr, jmax = _select_rows(cnt, rows, topk, idx_bits, True, tie_fn)
    t_ref[...] = thr
    j_ref[...] = jmax


def _sel_mask(scb, kpos, thr, jmax):
    return (scb > thr) | ((scb == thr) & (kpos <= jmax))


def _prompt_attn_kernel(q_ref, qi_ref, kiwi_ref, kt_ref, v_ref, kit_ref, o_ref,
                        sc_ref, cnt_ref, wib_ref, m_ref, l_ref, acc_ref, t_ref, j_ref,
                        *, tq, topk, idx_bits):
    tk = tq
    reps = tk // LANES
    i = pl.program_id(1)
    nkb = i + 1

    kiwi = kiwi_ref[...]
    for h in range(IDX_HEADS):
        wib_ref[h] = jnp.broadcast_to(kiwi[:, IDX_DIM + h:IDX_DIM + h + 1], (tq, LANES))

    def score_body(kb, carry):
        kit = kit_ref[kb]
        acc = None
        for h in range(IDX_HEADS):
            s = jnp.dot(qi_ref[:, h * LANES:(h + 1) * LANES], kit, preferred_element_type=F32)
            t = jnp.maximum(s, 0.0) * _tile_lanes(wib_ref[h], reps)
            acc = t if acc is None else acc + t
        sc_ref[kb] = acc
        return carry

    lax.fori_loop(0, nkb, score_body, 0)
    row = lax.broadcasted_iota(I32, (tq, tk), 0)
    col = lax.broadcasted_iota(I32, (tq, tk), 1)
    sc_ref[i] = jnp.where(col <= row, sc_ref[i], -jnp.inf)

    _select(sc_ref, cnt_ref, t_ref, j_ref, nkb, tq, min(128, tq), tk, topk, idx_bits)

    def bias_body(kb, carry):
        thr = _tile_lanes(t_ref[...], reps)
        jmax = _tile_lanes(j_ref[...], reps)
        sc_ref[kb] = jnp.where(_sel_mask(sc_ref[kb], kb * tk + col, thr, jmax), 0.0, NEG_BIG)
        return carry

    lax.fori_loop(0, nkb, bias_body, 0)

    m_ref[...] = jnp.full(m_ref.shape, NEG_BIG, F32)
    l_ref[...] = jnp.zeros(l_ref.shape, F32)
    acc_ref[...] = jnp.zeros(acc_ref.shape, F32)

    def attn_body(kb, carry):
        kt = kt_ref[kb]
        vb = v_ref[pl.ds(pl.multiple_of(kb * tk, tk), tk), :]
        bias = sc_ref[kb]
        for h in range(N_HEADS):
            s = jnp.dot(q_ref[h], kt, preferred_element_type=F32) + bias
            m_prev = m_ref[h]
            m_new = jnp.maximum(m_prev, jnp.max(s, axis=1, keepdims=True))
            alpha = jnp.exp2(m_prev - m_new)
            p = jnp.exp2(s - _tile_lanes(m_new, reps))
            l_ref[h] = alpha * l_ref[h] + sum(p[:, j * LANES:(j + 1) * LANES] for j in range(reps))
            acc_ref[h] = alpha * acc_ref[h] + jnp.dot(p.astype(BF16), vb, preferred_element_type=F32)
            m_ref[h] = m_new
        return carry

    lax.fori_loop(0, nkb, attn_body, 0)

    for h in range(N_HEADS):
        g = h // HEADS_PER_KV
        o = acc_ref[h] / jnp.sum(l_ref[h], axis=1, keepdims=True)
        o_ref[:, h * HEAD_DIM:(h + 1) * HEAD_DIM] = o[:, g * HEAD_DIM:(g + 1) * HEAD_DIM]


def _prompt_attn(q, qi, kiwi, kt, v, kit, tq, topk):
    b, s, _ = qi.shape
    nb = s // tq
    idx_bits = int(s).bit_length()
    blk = lambda w: pl.BlockSpec((None, tq, w), lambda bi, i: (bi, i, 0))
    full = lambda shp: pl.BlockSpec((None,) + shp, lambda bi, i: (bi,) + (0,) * len(shp),
                                    pipeline_mode=pl.Buffered(1))
    return pl.pallas_call(
        functools.partial(_prompt_attn_kernel, tq=tq, topk=topk, idx_bits=idx_bits),
        grid=(b, nb),
        in_specs=[pl.BlockSpec((N_HEADS, tq, LANES), lambda bi, i: (0, bi * nb + i, 0)),
                  blk(IDX_HEADS * LANES), blk(LANES),
                  full((nb, KV_WIDTH, tq)), full((s, KV_WIDTH)), full((nb, 2 * IDX_DIM, tq))],
        out_specs=blk(ATTN_WIDTH),
        out_shape=jax.ShapeDtypeStruct((b, s, ATTN_WIDTH), F32),
        scratch_shapes=[pltpu.VMEM((nb, tq, tq), F32),
                        pltpu.VMEM((nb, min(128, tq), LANES), F32),
                        pltpu.VMEM((IDX_HEADS, tq, LANES), F32),
                        pltpu.VMEM((N_HEADS, tq, LANES), F32),
                        pltpu.VMEM((N_HEADS, tq, LANES), F32),
                        pltpu.VMEM((N_HEADS, tq, KV_WIDTH), F32),
                        pltpu.VMEM((tq, LANES), F32),
                        pltpu.VMEM((tq, LANES), I32)],
        compiler_params=_cparams(("arbitrary", "arbitrary")),
        name="prompt_attn",
    )(q, qi, kiwi, kt, v, kit)


def _sample_attn_kernel(pt_ref, q_ref, qih_ref, wib_ref, kin_ref, kn_ref, vn_ref, *rest,
                        n_steps, pps, n_tok, tpad, page, topk, idx_bits):
    cik_refs, ck_refs, cv_refs = rest[:pps], rest[pps:2 * pps], rest[2 * pps:3 * pps]
    o_ref, sc_ref, sa_ref, vs_ref = rest[3 * pps:]
    step = pl.program_id(1)
    n_pages = n_steps * pps
    nkb = n_pages + 1
    qih = qih_ref[...]
    wib = wib_ref[...]
    q = q_ref[...]

    def idx_score(s):
        acc = None
        for h in range(IDX_HEADS):
            t = jnp.maximum(s[h * tpad:(h + 1) * tpad], 0.0) * wib[h * tpad:(h + 1) * tpad]
            acc = t if acc is None else acc + t
        return acc

    for j in range(pps):
        pg = step * pps + j
        sc_ref[pg] = idx_score(jnp.dot(qih, cik_refs[j][...].astype(BF16), preferred_element_type=F32))
        sa_ref[pg] = jnp.dot(q, ck_refs[j][...].astype(BF16), preferred_element_type=F32)
        vs_ref[pg] = cv_refs[j][...].astype(BF16)

    @pl.when(step == n_steps - 1)
    def _():
        s_new = idx_score(jnp.dot(qih, kin_ref[...], preferred_element_type=F32))
        tok = lax.broadcasted_iota(I32, (tpad, page), 0)
        col = lax.broadcasted_iota(I32, (tpad, page), 1)
        sc_ref[n_pages] = jnp.where((col <= tok) & (col < n_tok), s_new, -jnp.inf)
        sa_ref[n_pages] = jnp.dot(q, kn_ref[...], preferred_element_type=F32)
        vs_ref[n_pages] = vn_ref[...]

        kpos = (lax.broadcasted_iota(I32, (nkb, tpad, page), 0) * page
                + lax.broadcasted_iota(I32, (nkb, tpad, page), 2))

        def cnt(pred):
            w = jnp.where(pred(sc_ref[...], kpos, slice(None)), 1.0, 0.0)
            chunk = -(-nkb // SUBLANES)
            c = sum(jnp.sum(w[a:a + chunk], axis=0) for a in range(0, nkb, chunk))
            return jnp.broadcast_to(jnp.sum(c, axis=1, keepdims=True), c.shape)

        thr, jmax = _select_rows(cnt, tpad, topk, idx_bits, tok[:, :LANES] < n_tok, two_bits=True)
        sc_ref[...] = jnp.where(_sel_mask(sc_ref[...], kpos, thr[None], jmax[None]), 0.0, NEG_BIG)

        heads = N_HEADS
        inv_l = []
        for t in range(n_tok):
            s = sa_ref[:, t * heads:(t + 1) * heads, :] + sc_ref[:, t:t + 1, :]
            m = jnp.max(jnp.max(s, axis=0), axis=1, keepdims=True)
            pr = jnp.exp2(s - m[None])
            sa_ref[:, t * heads:(t + 1) * heads, :] = pr
            l = jnp.sum(jnp.sum(pr, axis=0), axis=1, keepdims=True)
            inv_l.append(jnp.broadcast_to(1.0 / l, (heads, KV_WIDTH)))

        p_all = jnp.concatenate([sa_ref[kb].astype(BF16) for kb in range(nkb)], axis=1)
        vt_all = jnp.concatenate([vs_ref[kb] for kb in range(nkb)], axis=1)
        acc = lax.dot_general(p_all, vt_all, NT_DIMS, preferred_element_type=F32)
        o_ref[...] = acc * jnp.concatenate(inv_l, axis=0)


def _sample_attn(page_table, q, qih, wib, kin_t, kn_t, vn_t, cik_t, ck_t, cv_t, n_tok, tpad, topk, pps):
    n, n_pages = page_table.shape
    page = ck_t.shape[2]
    assert page == LANES and KV_WIDTH == LANES and n_pages % pps == 0
    n_steps = n_pages // pps
    idx_bits = int((n_pages + 1) * page).bit_length()
    rows = n_tok * N_HEADS
    seq = lambda r, w: pl.BlockSpec((None, r, w), lambda b, p, pt: (b, 0, 0))

    def paged(r, j):
        return pl.BlockSpec((None, r, page), lambda b, p, pt: (pt[b, p * pps + j], 0, 0))

    grid_spec = pltpu.PrefetchScalarGridSpec(
        num_scalar_prefetch=1,
        grid=(n, n_steps),
        in_specs=([seq(rows, KV_WIDTH), seq(IDX_HEADS * tpad, IDX_DIM), seq(IDX_HEADS * tpad, LANES),
                   seq(IDX_DIM, page), seq(KV_WIDTH, page), seq(KV_WIDTH, page)]
                  + [paged(IDX_DIM, j) for j in range(pps)]
                  + [paged(KV_WIDTH, j) for j in range(pps)]
                  + [paged(KV_WIDTH, j) for j in range(pps)]),
        out_specs=seq(rows, KV_WIDTH),
        scratch_shapes=[pltpu.VMEM((n_pages + 1, tpad, page), F32),
                        pltpu.VMEM((n_pages + 1, rows, page), F32),
                        pltpu.VMEM((n_pages + 1, KV_WIDTH, page), BF16)],
    )
    return pl.pallas_call(
        functools.partial(_sample_attn_kernel, n_steps=n_steps, pps=pps, n_tok=n_tok, tpad=tpad, page=page,
                          topk=topk, idx_bits=idx_bits),
        grid_spec=grid_spec,
        out_shape=jax.ShapeDtypeStruct((n, rows, KV_WIDTH), F32),
        compiler_params=_cparams(("arbitrary", "arbitrary")),
        name="sample_attn",
    )(page_table, q, qih, wib, kin_t, kn_t, vn_t, *([cik_t] * pps), *([ck_t] * pps), *([cv_t] * pps))


def _mix_out_tail(x, o, gb, u0, um1, um2, wc_ref, wo_ref, gt, gpost, aw):
    y = wc_ref[0:1, :] * um2 + wc_ref[1:2, :] * um1 + wc_ref[2:3, :] * u0
    mixed = (jnp.dot(o.astype(BF16), wo_ref[0:aw, :], preferred_element_type=F32)
             + jnp.dot((gb * y).astype(BF16), wo_ref[aw:, :], preferred_element_type=F32))
    return x + gt * (_rms(mixed) * gpost)


def _mix_out_prompt_kernel(x_ref, o_ref_in, gb_ref, u_ref, halo_ref, gt_ref, gpost_ref, wc_ref, wo_ref,
                           out_ref, ext_ref, *, tm, tiles_per_seq):
    i = pl.program_id(0)
    halo = halo_ref[...]
    ext_ref[0:SUBLANES, :] = jnp.where(i % tiles_per_seq == 0, jnp.zeros_like(halo), halo)
    ext_ref[SUBLANES:, :] = u_ref[...]
    um2 = ext_ref[SUBLANES - 2:SUBLANES - 2 + tm, :]
    um1 = ext_ref[SUBLANES - 1:SUBLANES - 1 + tm, :]
    out_ref[...] = _mix_out_tail(x_ref[...], o_ref_in[...], gb_ref[...], u_ref[...], um1, um2,
                                 wc_ref, wo_ref, gt_ref[0], gpost_ref[...], ATTN_WIDTH)


def _mix_out_prompt(x, o, gb, u, gate, g_post, w_conv, w_o, tm, tiles_per_seq):
    rows, d = x.shape
    cw = u.shape[1]
    const = lambda i: (0, 0)
    row = lambda i: (i, 0)
    hb = tm // SUBLANES
    return pl.pallas_call(
        functools.partial(_mix_out_prompt_kernel, tm=tm, tiles_per_seq=tiles_per_seq),
        grid=(rows // tm,),
        in_specs=[pl.BlockSpec((tm, d), row),
                  pl.BlockSpec((tm, ATTN_WIDTH), row),
                  pl.BlockSpec((tm, cw), row),
                  pl.BlockSpec((tm, cw), row),
                  pl.BlockSpec((SUBLANES, cw), lambda i: (jnp.maximum(i * hb - 1, 0), 0)),
                  _mod_spec(gate, tm, tiles_per_seq),
                  pl.BlockSpec((1, d), const),
                  pl.BlockSpec(w_conv.shape, const),
                  pl.BlockSpec(w_o.shape, const)],
        out_specs=pl.BlockSpec((tm, d), row),
        out_shape=jax.ShapeDtypeStruct((rows, d), F32),
        scratch_shapes=[pltpu.VMEM((tm + SUBLANES, cw), F32)],
        compiler_params=_cparams(("arbitrary",)),
        name="mix_out_prompt",
    )(x, o, gb, u, u, gate, g_post, w_conv, w_o)


def _mix_out_sample_kernel(x_ref, o_ref_in, gb_ref, um2_ref, um1_ref, u0_ref, gt_ref, gpost_ref, wc_ref, wo_ref,
                           out_ref):
    out_ref[...] = _mix_out_tail(x_ref[...], o_ref_in[...], gb_ref[...], u0_ref[...], um1_ref[...], um2_ref[...],
                                 wc_ref, wo_ref, gt_ref[0], gpost_ref[...], ATTN_WIDTH)


def _mix_out_sample(x, o, gb, ext, gate, g_post, w_conv, w_o, tm):
    rows, d = x.shape
    cw = gb.shape[1]
    const = lambda i: (0, 0)
    row = lambda i: (i, 0)
    return pl.pallas_call(
        _mix_out_sample_kernel,
        grid=(rows // tm,),
        in_specs=[pl.BlockSpec((tm, d), row),
                  pl.BlockSpec((tm, ATTN_WIDTH), row),
                  pl.BlockSpec((tm, cw), row),
                  pl.BlockSpec((tm, cw), lambda i: (i, 0)),
                  pl.BlockSpec((tm, cw), lambda i: (i + 1, 0)),
                  pl.BlockSpec((tm, cw), lambda i: (i + 2, 0)),
                  pl.BlockSpec((1, tm, d), lambda i: (0, 0, 0)),
                  pl.BlockSpec((1, d), const),
                  pl.BlockSpec(w_conv.shape, const),
                  pl.BlockSpec(w_o.shape, const)],
        out_specs=pl.BlockSpec((tm, d), row),
        out_shape=jax.ShapeDtypeStruct((rows, d), F32),
        compiler_params=_cparams(("arbitrary",)),
        name="mix_out_sample",
    )(x, o, gb, ext, ext, ext, gate, g_post, w_conv, w_o)


def _prompt_layer(x, mod, lw, tm, tq):
    b, s, d = x.shape
    conv_ch = lw["w_conv"].shape[1]
    tps = s // tm
    md = lambda j, k: mod[:, j, k][:, None, :]
    xr = x.reshape(b * s, d)
    x1 = _ffn(xr, md(0, 0), md(0, 1), md(0, 2), lw["g_pre"][0:1], lw["g_post"][0:1], lw["ffn1_w13"], lw["ffn1_w2"],
              tm, tps)
    tabs = _rope_tables(jnp.arange(s))
    q, k_t, v_t, v_bf, qi, kiwi, kiwi_t, gb, u = _mix_in(x1, md(1, 0), md(1, 1), lw["g_pre"][1:2], lw["w_in_p"], tabs,
                                                         tm, tps, tps, conv_ch)
    nb = s // tq
    ki_t = kiwi_t[:, :IDX_DIM]
    blocks = lambda a: a.astype(BF16).reshape(b, a.shape[1], nb, tq).swapaxes(1, 2)
    kt = blocks(k_t)
    kit = blocks(jnp.concatenate([ki_t, ki_t], axis=1))
    topk = min(TOPK_MAX, s // 4)
    o = _prompt_attn(q, qi.reshape(b, s, -1), kiwi.reshape(b, s, -1), kt, v_bf.reshape(b, s, KV_WIDTH), kit, tq, topk)
    x2 = _mix_out_prompt(x1, o.reshape(b * s, -1), gb, u, md(1, 2), lw["g_post"][1:2], lw["w_conv"], lw["w_o"],
                         tm, tps)
    x3 = _ffn(x2, md(2, 0), md(2, 1), md(2, 2), lw["g_pre"][2:3], lw["g_post"][2:3], lw["ffn2_w13"], lw["ffn2_w2"],
              tm, tps)
    heads_last = lambda a: a.reshape(b, N_KV_HEADS, HEAD_DIM, s).transpose(0, 3, 1, 2)
    state = (heads_last(k_t), heads_last(v_t), ki_t.transpose(0, 2, 1),
             u.reshape(b, s, conv_ch)[:, s - (CONV_K - 1):])
    return x3.reshape(b, s, d), state


def _sample_layer(x_tm, mod, lw, cache_k, cache_v, cache_idx_k, state_conv, page_table):
    n = mod.shape[0]
    t_tok = x_tm.shape[0] // n
    d = x_tm.shape[1]
    conv_ch = lw["w_conv"].shape[1]
    n_pages = page_table.shape[1]
    page = cache_k.shape[1]
    past = n_pages * page
    md = lambda j, k: mod[:, j, k][None]
    x1 = _ffn(x_tm, md(0, 0), md(0, 1), md(0, 2), lw["g_pre"][0:1], lw["g_post"][0:1], lw["ffn1_w13"],
              lw["ffn1_w2"], n, t_tok)
    pos = jnp.repeat(past + jnp.arange(t_tok), n)
    tabs = _rope_tables(pos)
    q, k_t, v_t, _, qi, kiwi, kiwi_t, gb, u = _mix_in(x1, md(1, 0), md(1, 1), lw["g_pre"][1:2], lw["w_in_p"], tabs,
                                                      n, t_tok, t_tok, conv_ch)
    per_seq_t = lambda a: a[0].reshape(a.shape[1], t_tok, n).transpose(2, 0, 1)
    k_s, v_s, ki_s = per_seq_t(k_t), per_seq_t(v_t), per_seq_t(kiwi_t[:, :IDX_DIM])
    tpad = SUBLANES
    assert t_tok <= tpad

    def seq_major(a, heads, width):
        a = a.reshape(t_tok, n, heads, width).transpose(1, 2, 0, 3)
        a = jnp.pad(a, ((0, 0), (0, 0), (0, tpad - t_tok), (0, 0)))
        return a.reshape(n, heads * tpad, width)

    q_seq = q.reshape(N_HEADS, t_tok, n, LANES).transpose(2, 1, 0, 3).reshape(n, t_tok * N_HEADS, LANES)
    qi_nat = jnp.concatenate([qi[:, h * LANES + (h % 2) * IDX_DIM:h * LANES + (h % 2 + 1) * IDX_DIM]
                              for h in range(IDX_HEADS)], axis=1)
    qih = seq_major(qi_nat, IDX_HEADS, IDX_DIM)
    wib = jnp.broadcast_to(seq_major(kiwi[:, IDX_DIM:IDX_DIM + IDX_HEADS], IDX_HEADS, 1),
                           (n, IDX_HEADS * tpad, LANES))

    def new_keys_t(a):
        return jnp.pad(a.astype(BF16), ((0, 0), (0, 0), (0, page - t_tok)))

    ck_t = cache_k.transpose(0, 2, 3, 1).reshape(-1, KV_WIDTH, page)
    cv_t = cache_v.transpose(0, 2, 3, 1).reshape(-1, KV_WIDTH, page)
    cik_t = cache_idx_k.transpose(0, 2, 1)
    topk = min(TOPK_MAX, (past + t_tok) // 4)
    pps = max(p for p in (16, 8, 4, 2, 1) if n_pages % p == 0)
    o = _sample_attn(page_table, q_seq, qih, wib, new_keys_t(ki_s), new_keys_t(k_s), new_keys_t(v_s),
                     cik_t, ck_t, cv_t, t_tok, tpad, topk, pps)
    o = o.reshape(n, t_tok, N_KV_HEADS, HEADS_PER_KV, N_KV_HEADS, HEAD_DIM)
    o = jnp.stack([o[:, :, g, :, g] for g in range(N_KV_HEADS)], axis=2)
    o_tm = o.transpose(1, 0, 2, 3, 4).reshape(t_tok * n, ATTN_WIDTH)
    ext = jnp.concatenate([state_conv.transpose(1, 0, 2).reshape(-1, conv_ch), u], axis=0)
    x2 = _mix_out_sample(x1, o_tm, gb, ext, md(1, 2), lw["g_post"][1:2], lw["w_conv"], lw["w_o"], n)
    x3 = _ffn(x2, md(2, 0), md(2, 1), md(2, 2), lw["g_pre"][2:3], lw["g_post"][2:3], lw["ffn2_w13"],
              lw["ffn2_w2"], n, t_tok)
    tm_to_seq = lambda a: a.reshape((t_tok, n) + a.shape[1:]).swapaxes(0, 1)
    heads_last = lambda a: a.transpose(0, 2, 1).reshape(n, t_tok, N_KV_HEADS, HEAD_DIM)
    state = (heads_last(k_s), heads_last(v_s), ki_s.transpose(0, 2, 1), tm_to_seq(u)[:, t_tok - (CONV_K - 1):])
    return x3, state


def _row_tile(s, target):
    t = min(target, s)
    while s % t:
        t //= 2
    return t


def kernel(x_prompt, x_sample, cache_k, cache_v, cache_idx_k, state_conv, page_table, c_prompt, c_sample, w_ada, b_ada, g_pre, g_post, ffn1_w13, ffn1_w2, w_in, w_conv, w_o, ffn2_w13, ffn2_w2):
    depth = w_ada.shape[0]
    b, s, d = x_prompt.shape
    n, t_tok, _ = x_sample.shape
    conv_ch = w_conv.shape[2]
    tm = _row_tile(s, 512)
    tq = _row_tile(s, 256)

    yp = x_prompt
    ys = x_sample.swapaxes(0, 1).reshape(t_tok * n, d)
    c_all = jnp.concatenate([c_prompt, c_sample], axis=0)
    n_c = c_all.shape[0]
    c_all = jnp.pad(c_all, ((0, -n_c % SUBLANES), (0, 0)))
    outs = [[] for _ in range(8)]
    for l in range(depth):
        lw = dict(g_pre=g_pre[l], g_post=g_post[l], w_conv=w_conv[l],
                  ffn1_w13=ffn1_w13[l].astype(BF16), ffn1_w2=ffn1_w2[l].astype(BF16),
                  ffn2_w13=ffn2_w13[l].astype(BF16), ffn2_w2=ffn2_w2[l].astype(BF16),
                  w_in_p=_pack_w_in(w_in[l], conv_ch), w_o=w_o[l].astype(BF16))
        mod = _ada_mod(c_all, w_ada[l], b_ada[l])[:n_c].reshape(n_c, N_SUB, 3, d)
        yp, st_p = _prompt_layer(yp, mod[:b], lw, tm, tq)
        ys, st_s = _sample_layer(ys, mod[b:], lw, cache_k[l], cache_v[l], cache_idx_k[l], state_conv[l], page_table)
        for lst, a in zip(outs, st_p + st_s):
            lst.append(a)
    ys = ys.reshape(t_tok, n, d).swapaxes(0, 1)
    return (yp, ys) + tuple(jnp.stack(lst) for lst in outs)
```

```python
import functools

import numpy as np
import jax
import jax.numpy as jnp
from jax import lax
from jax.experimental import pallas as pl
from jax.experimental.pallas import tpu as pltpu

F32 = jnp.float32
BF16 = jnp.bfloat16
I32 = jnp.int32

N_HEADS = 8
HEAD_DIM = 64
N_KV_HEADS = 2
HEADS_PER_KV = N_HEADS // N_KV_HEADS
ATTN_WIDTH = N_HEADS * HEAD_DIM
KV_WIDTH = N_KV_HEADS * HEAD_DIM
ROT_DIV = 4
ROT = HEAD_DIM // ROT_DIV
ROT_HALF = ROT // 2
ROPE_THETA = 500000.0
IDX_HEADS = 4
IDX_DIM = 64
IDX_SCALE = (IDX_HEADS * IDX_DIM) ** -0.5
TOPK_MAX = 256
CONV_K = 3
N_SUB = 3
RMS_EPS = 1e-6
Q_SCALE = HEAD_DIM ** -0.5 * float(np.log2(np.e))

LANES = 128
SUBLANES = 8
VMEM_LIMIT = 56 * 1024 * 1024

NEG_BIG = -1e30
F32_MAX = float(np.finfo(np.float32).max)
INT_MIN = -(2 ** 31)
INT_MAX = 2 ** 31 - 1

NT_DIMS = (((1,), (1,)), ((), ()))


def _cparams(sem):
    return pltpu.CompilerParams(dimension_semantics=sem, vmem_limit_bytes=VMEM_LIMIT)


def _silu(x):
    return x * (1.0 / (1.0 + jnp.exp(-x)))


def _rms(x):
    return x * lax.rsqrt(jnp.mean(x * x, axis=-1, keepdims=True) + RMS_EPS)


def _tile_lanes(x, reps):
    return x if reps == 1 else jnp.concatenate([x] * reps, axis=1)


def _ada_kernel(c_ref, w_ref, b_ref, o_ref):
    a = _silu(c_ref[...]).astype(BF16)
    o_ref[...] = jnp.dot(a, w_ref[...].astype(BF16), preferred_element_type=F32) + b_ref[...]


def _ada_mod(c, w_ada, b_ada):
    n, d = c.shape
    nout = w_ada.shape[1]
    tn = 9 * LANES if nout % (9 * LANES) == 0 else LANES
    return pl.pallas_call(
        _ada_kernel,
        grid=(nout // tn,),
        in_specs=[pl.BlockSpec((n, d), lambda j: (0, 0)),
                  pl.BlockSpec((d, tn), lambda j: (0, j)),
                  pl.BlockSpec((1, tn), lambda j: (0, j))],
        out_specs=pl.BlockSpec((n, tn), lambda j: (0, j)),
        out_shape=jax.ShapeDtypeStruct((n, nout), F32),
        compiler_params=_cparams(("arbitrary",)),
        name="ada_mod",
    )(c, w_ada, b_ada.reshape(1, nout))


def _ffn_kernel(x_ref, sh_ref, sc_ref, gt_ref, gpre_ref, gpost_ref, w13_ref, w2_ref, o_ref, *, d_ff, chunk):
    x = x_ref[...]
    h = _rms(x) * gpre_ref[...] * (1.0 + sc_ref[0]) + sh_ref[0]
    hb = h.astype(BF16)
    y = jnp.zeros(x.shape, F32)
    for c in range(d_ff // chunk):
        a = jnp.dot(hb, w13_ref[:, c * chunk:(c + 1) * chunk], preferred_element_type=F32)
        b = jnp.dot(hb, w13_ref[:, d_ff + c * chunk:d_ff + (c + 1) * chunk], preferred_element_type=F32)
        g = (_silu(a) * b).astype(BF16)
        y = y + jnp.dot(g, w2_ref[c * chunk:(c + 1) * chunk, :], preferred_element_type=F32)
    o_ref[...] = x + 0.5 * gt_ref[0] * (_rms(y) * gpost_ref[...])


def _mod_spec(mod, tm, tiles_per_group):
    _, r, d = mod.shape
    return pl.BlockSpec((1, r, d), lambda i: (i // tiles_per_group, 0, 0))


def _ffn(x, shift, scale, gate, g_pre, g_post, w13, w2, tm, tiles_per_group):
    rows, d = x.shape
    d_ff = w2.shape[0]
    const = lambda i: (0, 0)
    return pl.pallas_call(
        functools.partial(_ffn_kernel, d_ff=d_ff, chunk=min(512, d_ff)),
        grid=(rows // tm,),
        in_specs=[pl.BlockSpec((tm, d), lambda i: (i, 0)),
                  _mod_spec(shift, tm, tiles_per_group),
                  _mod_spec(scale, tm, tiles_per_group),
                  _mod_spec(gate, tm, tiles_per_group),
                  pl.BlockSpec((1, d), const),
                  pl.BlockSpec((1, d), const),
                  pl.BlockSpec(w13.shape, const),
                  pl.BlockSpec(w2.shape, const)],
        out_specs=pl.BlockSpec((tm, d), lambda i: (i, 0)),
        out_shape=jax.ShapeDtypeStruct((rows, d), F32),
        compiler_params=_cparams(("arbitrary",)),
        name="ffn",
    )(x, shift, scale, gate, g_pre, g_post, w13, w2)


_OFF_Q = 0
_OFF_K = _OFF_Q + ATTN_WIDTH
_OFF_V = _OFF_K + KV_WIDTH
_OFF_QI = _OFF_V + KV_WIDTH
_OFF_KIWI = _OFF_QI + IDX_HEADS * IDX_DIM
_OFF_GB = _OFF_KIWI + LANES


def _pack_w_in(w_in, conv_ch):
    splits = [ATTN_WIDTH, KV_WIDTH, KV_WIDTH, IDX_HEADS * IDX_DIM, IDX_DIM, IDX_HEADS, conv_ch, conv_ch, conv_ch]
    offs = np.cumsum([0] + splits)
    d = w_in.shape[0]
    pad = jnp.zeros((d, LANES - IDX_DIM - IDX_HEADS), w_in.dtype)
    return jnp.concatenate([w_in[:, :offs[6]], pad, w_in[:, offs[6]:]], axis=1).astype(BF16)


def _rope_tables(pos):
    inv = jnp.power(ROPE_THETA, -jnp.arange(ROT_HALF, dtype=F32) * 2.0 / ROT)
    ang = pos.astype(F32)[:, None] * inv[None, :]
    cos, sin = jnp.cos(ang), jnp.sin(ang)
    n = pos.shape[0]
    one = jnp.ones((n, HEAD_DIM - ROT), F32)
    zero_r = jnp.zeros((n, HEAD_DIM - ROT), F32)
    zero_h = jnp.zeros((n, ROT_HALF), F32)
    c = jnp.concatenate([cos, cos, one], axis=1)
    s1 = jnp.concatenate([-sin, zero_h, zero_r], axis=1)
    s2 = jnp.concatenate([zero_h, sin, zero_r], axis=1)
    rep = LANES // HEAD_DIM
    return jnp.tile(c, (1, rep)), jnp.tile(s1, (1, rep)), jnp.tile(s2, (1, rep))


def _rope128(x, c, s1, s2):
    return x * c + pltpu.roll(x, LANES - ROT_HALF, 1) * s1 + pltpu.roll(x, ROT_HALF, 1) * s2


def _mix_in_kernel(x_ref, sh_ref, sc_ref, gpre_ref, w_ref, rc_ref, rs1_ref, rs2_ref,
                   q_ref, kt_ref, vt_ref, vb_ref, qi_ref, kiwi_ref, kiwit_ref, gb_ref, u_ref, *, conv_ch):
    x = x_ref[...]
    hb = (_rms(x) * gpre_ref[...] * (1.0 + sc_ref[0]) + sh_ref[0]).astype(BF16)
    c, s1, s2 = rc_ref[...], rs1_ref[...], rs2_ref[...]

    def proj(off, width):
        return jnp.dot(hb, w_ref[:, off:off + width], preferred_element_type=F32)

    def rope_cols(z):
        return [_rope128(z[:, g * LANES:(g + 1) * LANES], c, s1, s2) for g in range(z.shape[1] // LANES)]

    lane = lax.broadcasted_iota(I32, (x.shape[0], LANES), 1)
    low = lane < HEAD_DIM

    zq = proj(_OFF_Q, ATTN_WIDTH)
    for g, r in enumerate(rope_cols(zq)):
        r = r * Q_SCALE
        swapped = pltpu.roll(r, HEAD_DIM, 1)
        for e in range(2):
            h = 2 * g + e
            kv_group = h // HEADS_PER_KV
            src = r if e == kv_group else swapped
            keep = low if kv_group == 0 else jnp.logical_not(low)
            q_ref[h] = jnp.where(keep, src, 0.0).astype(BF16)
    (kr,) = rope_cols(proj(_OFF_K, KV_WIDTH))
    kt_ref[...] = kr.T
    pv = proj(_OFF_V, KV_WIDTH)
    vt_ref[...] = pv.T
    vb_ref[...] = pv.astype(BF16)
    zqi = proj(_OFF_QI, IDX_HEADS * IDX_DIM)
    for g, r in enumerate(rope_cols(zqi)):
        qi_ref[:, (2 * g) * LANES:(2 * g + 1) * LANES] = jnp.where(low, r, 0.0).astype(BF16)
        qi_ref[:, (2 * g + 1) * LANES:(2 * g + 2) * LANES] = jnp.where(low, 0.0, r).astype(BF16)
    zkw = proj(_OFF_KIWI, LANES)
    kiwi = jnp.where(lane < IDX_DIM, _rope128(zkw, c, s1, s2), zkw * IDX_SCALE)
    kiwi_ref[...] = kiwi
    kiwit_ref[...] = kiwi.T
    gb_ref[...] = proj(_OFF_GB, conv_ch)
    u_ref[...] = proj(_OFF_GB + conv_ch, conv_ch) * proj(_OFF_GB + 2 * conv_ch, conv_ch)


def _mix_in(x, shift, scale, g_pre, w_in_p, rope_tabs, tm, tiles_per_group, rope_tiles, conv_ch):
    rows, d = x.shape
    const = lambda i: (0, 0)
    row = lambda i: (i, 0)
    rope_spec = pl.BlockSpec((tm, LANES), lambda i: (i % rope_tiles, 0))
    assert 2 * HEAD_DIM == LANES and KV_WIDTH == LANES and 2 * IDX_DIM == LANES
    groups = rows // (tm * tiles_per_group)
    flat = lambda w, dt: (pl.BlockSpec((tm, w), row), jax.ShapeDtypeStruct((rows, w), dt))
    transposed = lambda w: (pl.BlockSpec((None, w, tm), lambda i: (i // tiles_per_group, 0, i % tiles_per_group)),
                            jax.ShapeDtypeStruct((groups, w, tm * tiles_per_group), F32))
    q_out = (pl.BlockSpec((N_HEADS, tm, LANES), lambda i: (0, i, 0)),
             jax.ShapeDtypeStruct((N_HEADS, rows, LANES), BF16))
    outs = [q_out, transposed(KV_WIDTH), transposed(KV_WIDTH), flat(KV_WIDTH, BF16), flat(IDX_HEADS * LANES, BF16),
            flat(LANES, F32), transposed(LANES), flat(conv_ch, F32), flat(conv_ch, F32)]
    return pl.pallas_call(
        functools.partial(_mix_in_kernel, conv_ch=conv_ch),
        grid=(rows // tm,),
        in_specs=[pl.BlockSpec((tm, d), row),
                  _mod_spec(shift, tm, tiles_per_group),
                  _mod_spec(scale, tm, tiles_per_group),
                  pl.BlockSpec((1, d), const),
                  pl.BlockSpec(w_in_p.shape, const),
                  rope_spec, rope_spec, rope_spec],
        out_specs=[spec for spec, _ in outs],
        out_shape=[shape for _, shape in outs],
        compiler_params=_cparams(("arbitrary",)),
        name="mix_in",
    )(x, shift, scale, g_pre, w_in_p, *rope_tabs)


def _key_to_float(u):
    o = u ^ INT_MIN
    bits = jnp.where(o >= 0, o, o ^ INT_MAX)
    return lax.bitcast_convert_type(bits, F32)


def _count(sc_ref, nkb, r0, rg, tk, pred):
    rows = slice(r0, r0 + rg)

    def body(kb, acc):
        for j in range(tk // LANES):
            x = sc_ref[kb, rows, j * LANES:(j + 1) * LANES]
            kpos = kb * tk + j * LANES + lax.broadcasted_iota(I32, (rg, LANES), 1)
            acc = acc + jnp.where(pred(x, kpos, rows), 1.0, 0.0)
        return acc

    group = 4

    def group_body(i, acc):
        for g in range(group):
            acc = body(group * i + g, acc)
        return acc

    acc = lax.fori_loop(0, nkb // group, group_body, jnp.zeros((rg, LANES), F32))
    return lax.fori_loop(group * (nkb // group), nkb, body, acc)


def _select_rows(cnt, rg, topk, idx_bits, row_ok, tie_fn=None, two_bits=False):
    def count_ge(cand):
        t = _key_to_float(cand)
        return cnt(lambda x, kp, rs: x >= t[rs])

    def bit_body(i, carry):
        key, c_ge = carry
        cand = key | jnp.left_shift(jnp.int32(1), 31 - i)
        c = count_ge(cand)
        ok = c >= topk
        return jnp.where(ok, cand, key), jnp.where(ok, c, c_ge)

    def two_bit_body(i, carry):
        key, c_ge = carry
        sh = 30 - 2 * i
        c1, c2, c3 = (key | jnp.left_shift(jnp.int32(v), sh) for v in (2, 1, 3))
        n1, n2, n3 = count_ge(c1), count_ge(c2), count_ge(c3)
        ok1, ok2, ok3 = n1 >= topk, n2 >= topk, n3 >= topk
        key = jnp.where(ok3, c3, jnp.where(ok1, c1, jnp.where(ok2, c2, key)))
        c_ge = jnp.where(ok3, n3, jnp.where(ok1, n1, jnp.where(ok2, n2, c_ge)))
        return key, c_ge

    init = (jnp.zeros((rg, LANES), I32), jnp.zeros((rg, LANES), F32))
    key, c_ge = lax.fori_loop(0, 16, two_bit_body, init) if two_bits else lax.fori_loop(0, 32, bit_body, init)
    thr = _key_to_float(key)
    enough = thr >= -F32_MAX
    thr = jnp.where(enough, thr, -F32_MAX)
    excess = jnp.max(jnp.where((c_ge > topk) & enough & row_ok, 1.0, 0.0)) > 0.5

    def tie_search():
        if tie_fn is not None:
            return tie_fn(thr, c_ge)
        need = topk - cnt(lambda x, kp, rs: x > thr[rs])

        def jbody(i, p):
            cand = p | jnp.left_shift(jnp.int32(1), idx_bits - 1 - i)
            c = cnt(lambda x, kp, rs: (x == thr[rs]) & (kp < cand[rs]))
            return jnp.where(c < need, cand, p)
        return lax.fori_loop(0, idx_bits, jbody, jnp.zeros((rg, LANES), I32))

    jmax = lax.cond(excess, tie_search, lambda: jnp.full((rg, LANES), INT_MAX, I32))
    return thr, jmax


def _tie_bound_blocked(sc_ref, cnt_ref, nkb, r0, rg, tk, topk, thr, c_ge):
    reps = tk // LANES
    thr_t = _tile_lanes(thr, reps)

    group = 8

    def count_body(i, carry):
        for g in range(group):
            kb = jnp.minimum(group * i + g, nkb - 1)
            eq = jnp.where(sc_ref[kb, r0:r0 + rg, :] == thr_t, 1.0, 0.0)
            c = sum(eq[:, j * LANES:(j + 1) * LANES] for j in range(reps))
            cnt_ref[kb] = jnp.broadcast_to(jnp.sum(c, axis=1, keepdims=True), (rg, LANES))
        return carry

    lax.fori_loop(0, (nkb + group - 1) // group, count_body, 0)
    ties = lax.fori_loop(0, nkb, lambda kb, tot: tot + cnt_ref[kb], jnp.zeros((rg, LANES), F32))
    need = topk - (c_ge - ties)

    def locate_body(kb, carry):
        run, kstar, before = carry
        new_run = run + cnt_ref[kb]
        hit = (run < need) & (new_run >= need)
        return new_run, jnp.where(hit, kb, kstar), jnp.where(hit, run, before)

    zeros = jnp.zeros((rg, LANES), F32)
    _, kstar, before = lax.fori_loop(0, nkb, locate_body, (zeros, jnp.full((rg, LANES), -1, I32), zeros))
    kstar_t = _tile_lanes(kstar, reps)

    def gather_body(kb, own):
        eq = jnp.where(sc_ref[kb, r0:r0 + rg, :] == thr_t, 1.0, 0.0)
        return jnp.where(kstar_t == kb, eq, own)

    own = lax.fori_loop(0, nkb, gather_body, jnp.zeros((rg, tk), F32))
    need_in = need - before
    upper = jnp.where(lax.broadcasted_iota(I32, (tk, tk), 0) <= lax.broadcasted_iota(I32, (tk, tk), 1), 1.0, 0.0)
    running = jnp.dot(own.astype(BF16), upper.astype(BF16), preferred_element_type=F32)
    p = jnp.sum(jnp.where(running < _tile_lanes(need_in, reps), 1.0, 0.0), axis=1, keepdims=True).astype(I32)
    return jnp.where(kstar >= 0, kstar * tk + p, INT_MAX)


def _select(sc_ref, cnt_ref, t_ref, j_ref, nkb, rows, rg, tk, topk, idx_bits):
    groups = range(0, rows, rg)

    def cnt(pred):
        acc = jnp.concatenate([_count(sc_ref, nkb, r0, rg, tk, pred) for r0 in groups], axis=0)
        return jnp.broadcast_to(jnp.sum(acc, axis=1, keepdims=True), (rows, LANES))

    def tie_fn(thr, c_ge):
        return jnp.concatenate(
            [_tie_bound_blocked(sc_ref, cnt_ref, nkb, r0, rg, tk, topk, thr[r0:r0 + rg], c_ge[r0:r0 + rg])
             for r0 in groups], axis=0)

    thr, jmax = _select_rows(cnt, rows, topk, idx_bits, True, tie_fn)
    t_ref[...] = thr
    j_ref[...] = jmax


def _sel_mask(scb, kpos, thr, jmax):
    return (scb > thr) | ((scb == thr) & (kpos <= jmax))


def _prompt_attn_kernel(q_ref, qi_ref, kiwi_ref, kt_ref, v_ref, kit_ref, o_ref,
                        sc_ref, cnt_ref, wib_ref, m_ref, l_ref, acc_ref, t_ref, j_ref,
                        *, tq, topk, idx_bits):
    tk = tq
    reps = tk // LANES
    i = pl.program_id(1)
    nkb = i + 1

    kiwi = kiwi_ref[...]
    for h in range(IDX_HEADS):
        wib_ref[h] = jnp.broadcast_to(kiwi[:, IDX_DIM + h:IDX_DIM + h + 1], (tq, LANES))

    def score_body(kb, carry):
        kit = kit_ref[kb]
        acc = None
        for h in range(IDX_HEADS):
            s = jnp.dot(qi_ref[:, h * LANES:(h + 1) * LANES], kit, preferred_element_type=F32)
            t = jnp.maximum(s, 0.0) * _tile_lanes(wib_ref[h], reps)
            acc = t if acc is None else acc + t
        sc_ref[kb] = acc
        return carry

    def score_group(j, carry):
        for g in range(4):
            carry = score_body(4 * j + g, carry)
        return carry

    lax.fori_loop(0, nkb // 4, score_group, 0)
    lax.fori_loop(4 * (nkb // 4), nkb, score_body, 0)
    row = lax.broadcasted_iota(I32, (tq, tk), 0)
    col = lax.broadcasted_iota(I32, (tq, tk), 1)
    sc_ref[i] = jnp.where(col <= row, sc_ref[i], -jnp.inf)

    _select(sc_ref, cnt_ref, t_ref, j_ref, nkb, tq, min(128, tq), tk, topk, idx_bits)

    def bias_body(kb, carry):
        thr = _tile_lanes(t_ref[...], reps)
        jmax = _tile_lanes(j_ref[...], reps)
        sc_ref[kb] = jnp.where(_sel_mask(sc_ref[kb], kb * tk + col, thr, jmax), 0.0, NEG_BIG)
        return carry

    lax.fori_loop(0, nkb, bias_body, 0)

    m_ref[...] = jnp.full(m_ref.shape, NEG_BIG, F32)
    l_ref[...] = jnp.zeros(l_ref.shape, F32)
    acc_ref[...] = jnp.zeros(acc_ref.shape, F32)

    def attn_body(kb, carry):
        kt = kt_ref[kb]
        vb = v_ref[pl.ds(pl.multiple_of(kb * tk, tk), tk), :]
        bias = sc_ref[kb]
        for h in range(N_HEADS):
            s = jnp.dot(q_ref[h], kt, preferred_element_type=F32) + bias
            m_prev = m_ref[h]
            m_new = jnp.maximum(m_prev, jnp.max(s, axis=1, keepdims=True))
            alpha = jnp.exp2(m_prev - m_new)
            p = jnp.exp2(s - _tile_lanes(m_new, reps))
            l_ref[h] = alpha * l_ref[h] + sum(p[:, j * LANES:(j + 1) * LANES] for j in range(reps))
            acc_ref[h] = alpha * acc_ref[h] + jnp.dot(p.astype(BF16), vb, preferred_element_type=F32)
            m_ref[h] = m_new
        return carry

    lax.fori_loop(0, nkb, attn_body, 0)

    for h in range(N_HEADS):
        g = h // HEADS_PER_KV
        o = acc_ref[h] / jnp.sum(l_ref[h], axis=1, keepdims=True)
        o_ref[:, h * HEAD_DIM:(h + 1) * HEAD_DIM] = o[:, g * HEAD_DIM:(g + 1) * HEAD_DIM]


def _prompt_attn(q, qi, kiwi, kt, v, kit, tq, topk):
    b, s, _ = qi.shape
    nb = s // tq
    idx_bits = int(s).bit_length()
    blk = lambda w: pl.BlockSpec((None, tq, w), lambda bi, i: (bi, i, 0))
    full = lambda shp: pl.BlockSpec((None,) + shp, lambda bi, i: (bi,) + (0,) * len(shp),
                                    pipeline_mode=pl.Buffered(1))
    return pl.pallas_call(
        functools.partial(_prompt_attn_kernel, tq=tq, topk=topk, idx_bits=idx_bits),
        grid=(b, nb),
        in_specs=[pl.BlockSpec((N_HEADS, tq, LANES), lambda bi, i: (0, bi * nb + i, 0)),
                  blk(IDX_HEADS * LANES), blk(LANES),
                  full((nb, KV_WIDTH, tq)), full((s, KV_WIDTH)), full((nb, 2 * IDX_DIM, tq))],
        out_specs=blk(ATTN_WIDTH),
        out_shape=jax.ShapeDtypeStruct((b, s, ATTN_WIDTH), F32),
        scratch_shapes=[pltpu.VMEM((nb, tq, tq), F32),
                        pltpu.VMEM((nb, min(128, tq), LANES), F32),
                        pltpu.VMEM((IDX_HEADS, tq, LANES), F32),
                        pltpu.VMEM((N_HEADS, tq, LANES), F32),
                        pltpu.VMEM((N_HEADS, tq, LANES), F32),
                        pltpu.VMEM((N_HEADS, tq, KV_WIDTH), F32),
                        pltpu.VMEM((tq, LANES), F32),
                        pltpu.VMEM((tq, LANES), I32)],
        compiler_params=_cparams(("arbitrary", "arbitrary")),
        name="prompt_attn",
    )(q, qi, kiwi, kt, v, kit)


def _sample_attn_kernel(pt_ref, q_ref, qih_ref, wib_ref, kin_ref, kn_ref, vn_ref, *rest,
                        n_steps, pps, n_tok, tpad, page, topk, idx_bits):
    cik_refs, ck_refs, cv_refs = rest[:pps], rest[pps:2 * pps], rest[2 * pps:3 * pps]
    o_ref, sc_ref, sa_ref, vs_ref = rest[3 * pps:]
    step = pl.program_id(1)
    n_pages = n_steps * pps
    nkb = n_pages + 1
    qih = qih_ref[...]
    wib = wib_ref[...]
    q = q_ref[...]

    def idx_score(s):
        acc = None
        for h in range(IDX_HEADS):
            t = jnp.maximum(s[h * tpad:(h + 1) * tpad], 0.0) * wib[h * tpad:(h + 1) * tpad]
            acc = t if acc is None else acc + t
        return acc

    for j in range(pps):
        pg = step * pps + j
        sc_ref[pg] = idx_score(jnp.dot(qih, cik_refs[j][...].astype(BF16), preferred_element_type=F32))
        sa_ref[pg] = jnp.dot(q, ck_refs[j][...].astype(BF16), preferred_element_type=F32)
        vs_ref[pg] = cv_refs[j][...].astype(BF16)

    @pl.when(step == n_steps - 1)
    def _():
        s_new = idx_score(jnp.dot(qih, kin_ref[...], preferred_element_type=F32))
        tok = lax.broadcasted_iota(I32, (tpad, page), 0)
        col = lax.broadcasted_iota(I32, (tpad, page), 1)
        sc_ref[n_pages] = jnp.where((col <= tok) & (col < n_tok), s_new, -jnp.inf)
        sa_ref[n_pages] = jnp.dot(q, kn_ref[...], preferred_element_type=F32)
        vs_ref[n_pages] = vn_ref[...]

        kpos = (lax.broadcasted_iota(I32, (nkb, tpad, page), 0) * page
                + lax.broadcasted_iota(I32, (nkb, tpad, page), 2))

        def cnt(pred):
            w = jnp.where(pred(sc_ref[...], kpos, slice(None)), 1.0, 0.0)
            chunk = -(-nkb // SUBLANES)
            c = sum(jnp.sum(w[a:a + chunk], axis=0) for a in range(0, nkb, chunk))
            return jnp.broadcast_to(jnp.sum(c, axis=1, keepdims=True), c.shape)

        thr, jmax = _select_rows(cnt, tpad, topk, idx_bits, tok[:, :LANES] < n_tok, two_bits=True)
        sc_ref[...] = jnp.where(_sel_mask(sc_ref[...], kpos, thr[None], jmax[None]), 0.0, NEG_BIG)

        heads = N_HEADS
        inv_l = []
        for t in range(n_tok):
            s = sa_ref[:, t * heads:(t + 1) * heads, :] + sc_ref[:, t:t + 1, :]
            m = jnp.max(jnp.max(s, axis=0), axis=1, keepdims=True)
            pr = jnp.exp2(s - m[None])
            sa_ref[:, t * heads:(t + 1) * heads, :] = pr
            l = jnp.sum(jnp.sum(pr, axis=0), axis=1, keepdims=True)
            inv_l.append(jnp.broadcast_to(1.0 / l, (heads, KV_WIDTH)))

        p_all = jnp.concatenate([sa_ref[kb].astype(BF16) for kb in range(nkb)], axis=1)
        vt_all = jnp.concatenate([vs_ref[kb] for kb in range(nkb)], axis=1)
        acc = lax.dot_general(p_all, vt_all, NT_DIMS, preferred_element_type=F32)
        o_ref[...] = acc * jnp.concatenate(inv_l, axis=0)


def _sample_attn(page_table, q, qih, wib, kin_t, kn_t, vn_t, cik_t, ck_t, cv_t, n_tok, tpad, topk, pps):
    n, n_pages = page_table.shape
    page = ck_t.shape[2]
    assert page == LANES and KV_WIDTH == LANES and n_pages % pps == 0
    n_steps = n_pages // pps
    idx_bits = int((n_pages + 1) * page).bit_length()
    rows = n_tok * N_HEADS
    seq = lambda r, w: pl.BlockSpec((None, r, w), lambda b, p, pt: (b, 0, 0))

    def paged(r, j):
        return pl.BlockSpec((None, r, page), lambda b, p, pt: (pt[b, p * pps + j], 0, 0))

    grid_spec = pltpu.PrefetchScalarGridSpec(
        num_scalar_prefetch=1,
        grid=(n, n_steps),
        in_specs=([seq(rows, KV_WIDTH), seq(IDX_HEADS * tpad, IDX_DIM), seq(IDX_HEADS * tpad, LANES),
                   seq(IDX_DIM, page), seq(KV_WIDTH, page), seq(KV_WIDTH, page)]
                  + [paged(IDX_DIM, j) for j in range(pps)]
                  + [paged(KV_WIDTH, j) for j in range(pps)]
                  + [paged(KV_WIDTH, j) for j in range(pps)]),
        out_specs=seq(rows, KV_WIDTH),
        scratch_shapes=[pltpu.VMEM((n_pages + 1, tpad, page), F32),
                        pltpu.VMEM((n_pages + 1, rows, page), F32),
                        pltpu.VMEM((n_pages + 1, KV_WIDTH, page), BF16)],
    )
    return pl.pallas_call(
        functools.partial(_sample_attn_kernel, n_steps=n_steps, pps=pps, n_tok=n_tok, tpad=tpad, page=page,
                          topk=topk, idx_bits=idx_bits),
        grid_spec=grid_spec,
        out_shape=jax.ShapeDtypeStruct((n, rows, KV_WIDTH), F32),
        compiler_params=_cparams(("arbitrary", "arbitrary")),
        name="sample_attn",
    )(page_table, q, qih, wib, kin_t, kn_t, vn_t, *([cik_t] * pps), *([ck_t] * pps), *([cv_t] * pps))


def _mix_out_tail(x, o, gb, u0, um1, um2, wc_ref, wo_ref, gt, gpost, aw):
    y = wc_ref[0:1, :] * um2 + wc_ref[1:2, :] * um1 + wc_ref[2:3, :] * u0
    mixed = (jnp.dot(o.astype(BF16), wo_ref[0:aw, :], preferred_element_type=F32)
             + jnp.dot((gb * y).astype(BF16), wo_ref[aw:, :], preferred_element_type=F32))
    return x + gt * (_rms(mixed) * gpost)


def _mix_out_prompt_kernel(x_ref, o_ref_in, gb_ref, u_ref, halo_ref, gt_ref, gpost_ref, wc_ref, wo_ref,
                           out_ref, ext_ref, *, tm, tiles_per_seq):
    i = pl.program_id(0)
    halo = halo_ref[...]
    ext_ref[0:SUBLANES, :] = jnp.where(i % tiles_per_seq == 0, jnp.zeros_like(halo), halo)
    ext_ref[SUBLANES:, :] = u_ref[...]
    um2 = ext_ref[SUBLANES - 2:SUBLANES - 2 + tm, :]
    um1 = ext_ref[SUBLANES - 1:SUBLANES - 1 + tm, :]
    out_ref[...] = _mix_out_tail(x_ref[...], o_ref_in[...], gb_ref[...], u_ref[...], um1, um2,
                                 wc_ref, wo_ref, gt_ref[0], gpost_ref[...], ATTN_WIDTH)


def _mix_out_prompt(x, o, gb, u, gate, g_post, w_conv, w_o, tm, tiles_per_seq):
    rows, d = x.shape
    cw = u.shape[1]
    const = lambda i: (0, 0)
    row = lambda i: (i, 0)
    hb = tm // SUBLANES
    return pl.pallas_call(
        functools.partial(_mix_out_prompt_kernel, tm=tm, tiles_per_seq=tiles_per_seq),
        grid=(rows // tm,),
        in_specs=[pl.BlockSpec((tm, d), row),
                  pl.BlockSpec((tm, ATTN_WIDTH), row),
                  pl.BlockSpec((tm, cw), row),
                  pl.BlockSpec((tm, cw), row),
                  pl.BlockSpec((SUBLANES, cw), lambda i: (jnp.maximum(i * hb - 1, 0), 0)),
                  _mod_spec(gate, tm, tiles_per_seq),
                  pl.BlockSpec((1, d), const),
                  pl.BlockSpec(w_conv.shape, const),
                  pl.BlockSpec(w_o.shape, const)],
        out_specs=pl.BlockSpec((tm, d), row),
        out_shape=jax.ShapeDtypeStruct((rows, d), F32),
        scratch_shapes=[pltpu.VMEM((tm + SUBLANES, cw), F32)],
        compiler_params=_cparams(("arbitrary",)),
        name="mix_out_prompt",
    )(x, o, gb, u, u, gate, g_post, w_conv, w_o)


def _mix_out_sample_kernel(x_ref, o_ref_in, gb_ref, um2_ref, um1_ref, u0_ref, gt_ref, gpost_ref, wc_ref, wo_ref,
                           out_ref):
    out_ref[...] = _mix_out_tail(x_ref[...], o_ref_in[...], gb_ref[...], u0_ref[...], um1_ref[...], um2_ref[...],
                                 wc_ref, wo_ref, gt_ref[0], gpost_ref[...], ATTN_WIDTH)


def _mix_out_sample(x, o, gb, ext, gate, g_post, w_conv, w_o, tm):
    rows, d = x.shape
    cw = gb.shape[1]
    const = lambda i: (0, 0)
    row = lambda i: (i, 0)
    return pl.pallas_call(
        _mix_out_sample_kernel,
        grid=(rows // tm,),
        in_specs=[pl.BlockSpec((tm, d), row),
                  pl.BlockSpec((tm, ATTN_WIDTH), row),
                  pl.BlockSpec((tm, cw), row),
                  pl.BlockSpec((tm, cw), lambda i: (i, 0)),
                  pl.BlockSpec((tm, cw), lambda i: (i + 1, 0)),
                  pl.BlockSpec((tm, cw), lambda i: (i + 2, 0)),
                  pl.BlockSpec((1, tm, d), lambda i: (0, 0, 0)),
                  pl.BlockSpec((1, d), const),
                  pl.BlockSpec(w_conv.shape, const),
                  pl.BlockSpec(w_o.shape, const)],
        out_specs=pl.BlockSpec((tm, d), row),
        out_shape=jax.ShapeDtypeStruct((rows, d), F32),
        compiler_params=_cparams(("arbitrary",)),
        name="mix_out_sample",
    )(x, o, gb, ext, ext, ext, gate, g_post, w_conv, w_o)


def _prompt_layer(x, mod, lw, tm, tq):
    b, s, d = x.shape
    conv_ch = lw["w_conv"].shape[1]
    tps = s // tm
    md = lambda j, k: mod[:, j, k][:, None, :]
    xr = x.reshape(b * s, d)
    x1 = _ffn(xr, md(0, 0), md(0, 1), md(0, 2), lw["g_pre"][0:1], lw["g_post"][0:1], lw["ffn1_w13"], lw["ffn1_w2"],
              tm, tps)
    tabs = _rope_tables(jnp.arange(s))
    q, k_t, v_t, v_bf, qi, kiwi, kiwi_t, gb, u = _mix_in(x1, md(1, 0), md(1, 1), lw["g_pre"][1:2], lw["w_in_p"], tabs,
                                                         tm, tps, tps, conv_ch)
    nb = s // tq
    ki_t = kiwi_t[:, :IDX_DIM]
    blocks = lambda a: a.astype(BF16).reshape(b, a.shape[1], nb, tq).swapaxes(1, 2)
    kt = blocks(k_t)
    kit = blocks(jnp.concatenate([ki_t, ki_t], axis=1))
    topk = min(TOPK_MAX, s // 4)
    o = _prompt_attn(q, qi.reshape(b, s, -1), kiwi.reshape(b, s, -1), kt, v_bf.reshape(b, s, KV_WIDTH), kit, tq, topk)
    x2 = _mix_out_prompt(x1, o.reshape(b * s, -1), gb, u, md(1, 2), lw["g_post"][1:2], lw["w_conv"], lw["w_o"],
                         tm, tps)
    x3 = _ffn(x2, md(2, 0), md(2, 1), md(2, 2), lw["g_pre"][2:3], lw["g_post"][2:3], lw["ffn2_w13"], lw["ffn2_w2"],
              tm, tps)
    heads_last = lambda a: a.reshape(b, N_KV_HEADS, HEAD_DIM, s).transpose(0, 3, 1, 2)
    state = (heads_last(k_t), heads_last(v_t), ki_t.transpose(0, 2, 1),
             u.reshape(b, s, conv_ch)[:, s - (CONV_K - 1):])
    return x3.reshape(b, s, d), state


def _sample_layer(x_tm, mod, lw, cache_k, cache_v, cache_idx_k, state_conv, page_table):
    n = mod.shape[0]
    t_tok = x_tm.shape[0] // n
    d = x_tm.shape[1]
    conv_ch = lw["w_conv"].shape[1]
    n_pages = page_table.shape[1]
    page = cache_k.shape[1]
    past = n_pages * page
    md = lambda j, k: mod[:, j, k][None]
    x1 = _ffn(x_tm, md(0, 0), md(0, 1), md(0, 2), lw["g_pre"][0:1], lw["g_post"][0:1], lw["ffn1_w13"],
              lw["ffn1_w2"], n, t_tok)
    pos = jnp.repeat(past + jnp.arange(t_tok), n)
    tabs = _rope_tables(pos)
    q, k_t, v_t, _, qi, kiwi, kiwi_t, gb, u = _mix_in(x1, md(1, 0), md(1, 1), lw["g_pre"][1:2], lw["w_in_p"], tabs,
                                                      n, t_tok, t_tok, conv_ch)
    per_seq_t = lambda a: a[0].reshape(a.shape[1], t_tok, n).transpose(2, 0, 1)
    k_s, v_s, ki_s = per_seq_t(k_t), per_seq_t(v_t), per_seq_t(kiwi_t[:, :IDX_DIM])
    tpad = SUBLANES
    assert t_tok <= tpad

    def seq_major(a, heads, width):
        a = a.reshape(t_tok, n, heads, width).transpose(1, 2, 0, 3)
        a = jnp.pad(a, ((0, 0), (0, 0), (0, tpad - t_tok), (0, 0)))
        return a.reshape(n, heads * tpad, width)

    q_seq = q.reshape(N_HEADS, t_tok, n, LANES).transpose(2, 1, 0, 3).reshape(n, t_tok * N_HEADS, LANES)
    qi_nat = jnp.concatenate([qi[:, h * LANES + (h % 2) * IDX_DIM:h * LANES + (h % 2 + 1) * IDX_DIM]
                              for h in range(IDX_HEADS)], axis=1)
    qih = seq_major(qi_nat, IDX_HEADS, IDX_DIM)
    wib = jnp.broadcast_to(seq_major(kiwi[:, IDX_DIM:IDX_DIM + IDX_HEADS], IDX_HEADS, 1),
                           (n, IDX_HEADS * tpad, LANES))

    def new_keys_t(a):
        return jnp.pad(a.astype(BF16), ((0, 0), (0, 0), (0, page - t_tok)))

    ck_t = cache_k.transpose(0, 2, 3, 1).reshape(-1, KV_WIDTH, page)
    cv_t = cache_v.transpose(0, 2, 3, 1).reshape(-1, KV_WIDTH, page)
    cik_t = cache_idx_k.transpose(0, 2, 1)
    topk = min(TOPK_MAX, (past + t_tok) // 4)
    pps = max(p for p in (16, 8, 4, 2, 1) if n_pages % p == 0)
    o = _sample_attn(page_table, q_seq, qih, wib, new_keys_t(ki_s), new_keys_t(k_s), new_keys_t(v_s),
                     cik_t, ck_t, cv_t, t_tok, tpad, topk, pps)
    o = o.reshape(n, t_tok, N_KV_HEADS, HEADS_PER_KV, N_KV_HEADS, HEAD_DIM)
    o = jnp.stack([o[:, :, g, :, g] for g in range(N_KV_HEADS)], axis=2)
    o_tm = o.transpose(1, 0, 2, 3, 4).reshape(t_tok * n, ATTN_WIDTH)
    ext = jnp.concatenate([state_conv.transpose(1, 0, 2).reshape(-1, conv_ch), u], axis=0)
    x2 = _mix_out_sample(x1, o_tm, gb, ext, md(1, 2), lw["g_post"][1:2], lw["w_conv"], lw["w_o"], n)
    x3 = _ffn(x2, md(2, 0), md(2, 1), md(2, 2), lw["g_pre"][2:3], lw["g_post"][2:3], lw["ffn2_w13"],
              lw["ffn2_w2"], n, t_tok)
    tm_to_seq = lambda a: a.reshape((t_tok, n) + a.shape[1:]).swapaxes(0, 1)
    heads_last = lambda a: a.transpose(0, 2, 1).reshape(n, t_tok, N_KV_HEADS, HEAD_DIM)
    state = (heads_last(k_s), heads_last(v_s), ki_s.transpose(0, 2, 1), tm_to_seq(u)[:, t_tok - (CONV_K - 1):])
    return x3, state


def _row_tile(s, target):
    t = min(target, s)
    while s % t:
        t //= 2
    return t


def kernel(x_prompt, x_sample, cache_k, cache_v, cache_idx_k, state_conv, page_table, c_prompt, c_sample, w_ada, b_ada, g_pre, g_post, ffn1_w13, ffn1_w2, w_in, w_conv, w_o, ffn2_w13, ffn2_w2):
    depth = w_ada.shape[0]
    b, s, d = x_prompt.shape
    n, t_tok, _ = x_sample.shape
    conv_ch = w_conv.shape[2]
    tm = _row_tile(s, 512)
    tq = _row_tile(s, 256)

    yp = x_prompt
    ys = x_sample.swapaxes(0, 1).reshape(t_tok * n, d)
    c_all = jnp.concatenate([c_prompt, c_sample], axis=0)
    n_c = c_all.shape[0]
    c_all = jnp.pad(c_all, ((0, -n_c % SUBLANES), (0, 0)))
    outs = [[] for _ in range(8)]
    for l in range(depth):
        lw = dict(g_pre=g_pre[l], g_post=g_post[l], w_conv=w_conv[l],
                  ffn1_w13=ffn1_w13[l].astype(BF16), ffn1_w2=ffn1_w2[l].astype(BF16),
                  ffn2_w13=ffn2_w13[l].astype(BF16), ffn2_w2=ffn2_w2[l].astype(BF16),
                  w_in_p=_pack_w_in(w_in[l], conv_ch), w_o=w_o[l].astype(BF16))
        mod = _ada_mod(c_all, w_ada[l], b_ada[l])[:n_c].reshape(n_c, N_SUB, 3, d)
        yp, st_p = _prompt_layer(yp, mod[:b], lw, tm, tq)
        ys, st_s = _sample_layer(ys, mod[b:], lw, cache_k[l], cache_v[l], cache_idx_k[l], state_conv[l], page_table)
        for lst, a in zip(outs, st_p + st_s):
            lst.append(a)
    ys = ys.reshape(t_tok, n, d).swapaxes(0, 1)
    return (yp, ys) + tuple(jnp.stack(lst) for lst in outs)
```

```python
import functools

import numpy as np
import jax
import jax.numpy as jnp
from jax import lax
from jax.experimental import pallas as pl
from jax.experimental.pallas import tpu as pltpu

F32 = jnp.float32
BF16 = jnp.bfloat16
I32 = jnp.int32

N_HEADS = 8
HEAD_DIM = 64
N_KV_HEADS = 2
HEADS_PER_KV = N_HEADS // N_KV_HEADS
ATTN_WIDTH = N_HEADS * HEAD_DIM
KV_WIDTH = N_KV_HEADS * HEAD_DIM
ROT_DIV = 4
ROT = HEAD_DIM // ROT_DIV
ROT_HALF = ROT // 2
ROPE_THETA = 500000.0
IDX_HEADS = 4
IDX_DIM = 64
IDX_SCALE = (IDX_HEADS * IDX_DIM) ** -0.5
TOPK_MAX = 256
CONV_K = 3
N_SUB = 3
RMS_EPS = 1e-6
Q_SCALE = HEAD_DIM ** -0.5 * float(np.log2(np.e))

LANES = 128
SUBLANES = 8
VMEM_LIMIT = 56 * 1024 * 1024

NEG_BIG = -1e30
F32_MAX = float(np.finfo(np.float32).max)
INT_MIN = -(2 ** 31)
INT_MAX = 2 ** 31 - 1

NT_DIMS = (((1,), (1,)), ((), ()))


def _cparams(sem):
    return pltpu.CompilerParams(dimension_semantics=sem, vmem_limit_bytes=VMEM_LIMIT)


def _silu(x):
    return x * (1.0 / (1.0 + jnp.exp(-x)))


def _rms(x):
    return x * lax.rsqrt(jnp.mean(x * x, axis=-1, keepdims=True) + RMS_EPS)


def _tile_lanes(x, reps):
    return x if reps == 1 else jnp.concatenate([x] * reps, axis=1)


def _ada_kernel(c_ref, w_ref, b_ref, o_ref):
    a = _silu(c_ref[...]).astype(BF16)
    o_ref[...] = jnp.dot(a, w_ref[...].astype(BF16), preferred_element_type=F32) + b_ref[...]


def _ada_mod(c, w_ada, b_ada):
    n, d = c.shape
    nout = w_ada.shape[1]
    tn = 9 * LANES if nout % (9 * LANES) == 0 else LANES
    return pl.pallas_call(
        _ada_kernel,
        grid=(nout // tn,),
        in_specs=[pl.BlockSpec((n, d), lambda j: (0, 0)),
                  pl.BlockSpec((d, tn), lambda j: (0, j)),
                  pl.BlockSpec((1, tn), lambda j: (0, j))],
        out_specs=pl.BlockSpec((n, tn), lambda j: (0, j)),
        out_shape=jax.ShapeDtypeStruct((n, nout), F32),
        compiler_params=_cparams(("arbitrary",)),
        name="ada_mod",
    )(c, w_ada, b_ada.reshape(1, nout))


def _ffn_kernel(x_ref, sh_ref, sc_ref, gt_ref, gpre_ref, gpost_ref, w13_ref, w2_ref, o_ref, *, d_ff, chunk):
    x = x_ref[...]
    h = _rms(x) * gpre_ref[...] * (1.0 + sc_ref[0]) + sh_ref[0]
    hb = h.astype(BF16)
    y = jnp.zeros(x.shape, F32)
    for c in range(d_ff // chunk):
        a = jnp.dot(hb, w13_ref[:, c * chunk:(c + 1) * chunk], preferred_element_type=F32)
        b = jnp.dot(hb, w13_ref[:, d_ff + c * chunk:d_ff + (c + 1) * chunk], preferred_element_type=F32)
        g = (_silu(a) * b).astype(BF16)
        y = y + jnp.dot(g, w2_ref[c * chunk:(c + 1) * chunk, :], preferred_element_type=F32)
    o_ref[...] = x + 0.5 * gt_ref[0] * (_rms(y) * gpost_ref[...])


def _mod_spec(mod, tm, tiles_per_group):
    _, r, d = mod.shape
    return pl.BlockSpec((1, r, d), lambda i: (i // tiles_per_group, 0, 0))


def _ffn(x, shift, scale, gate, g_pre, g_post, w13, w2, tm, tiles_per_group):
    rows, d = x.shape
    d_ff = w2.shape[0]
    const = lambda i: (0, 0)
    return pl.pallas_call(
        functools.partial(_ffn_kernel, d_ff=d_ff, chunk=min(512, d_ff)),
        grid=(rows // tm,),
        in_specs=[pl.BlockSpec((tm, d), lambda i: (i, 0)),
                  _mod_spec(shift, tm, tiles_per_group),
                  _mod_spec(scale, tm, tiles_per_group),
                  _mod_spec(gate, tm, tiles_per_group),
                  pl.BlockSpec((1, d), const),
                  pl.BlockSpec((1, d), const),
                  pl.BlockSpec(w13.shape, const),
                  pl.BlockSpec(w2.shape, const)],
        out_specs=pl.BlockSpec((tm, d), lambda i: (i, 0)),
        out_shape=jax.ShapeDtypeStruct((rows, d), F32),
        compiler_params=_cparams(("arbitrary",)),
        name="ffn",
    )(x, shift, scale, gate, g_pre, g_post, w13, w2)


_OFF_Q = 0
_OFF_K = _OFF_Q + ATTN_WIDTH
_OFF_V = _OFF_K + KV_WIDTH
_OFF_QI = _OFF_V + KV_WIDTH
_OFF_KIWI = _OFF_QI + IDX_HEADS * IDX_DIM
_OFF_GB = _OFF_KIWI + LANES


def _pack_w_in(w_in, conv_ch):
    splits = [ATTN_WIDTH, KV_WIDTH, KV_WIDTH, IDX_HEADS * IDX_DIM, IDX_DIM, IDX_HEADS, conv_ch, conv_ch, conv_ch]
    offs = np.cumsum([0] + splits)
    d = w_in.shape[0]
    pad = jnp.zeros((d, LANES - IDX_DIM - IDX_HEADS), w_in.dtype)
    return jnp.concatenate([w_in[:, :offs[6]], pad, w_in[:, offs[6]:]], axis=1).astype(BF16)


def _rope_tables(pos):
    inv = jnp.power(ROPE_THETA, -jnp.arange(ROT_HALF, dtype=F32) * 2.0 / ROT)
    ang = pos.astype(F32)[:, None] * inv[None, :]
    cos, sin = jnp.cos(ang), jnp.sin(ang)
    n = pos.shape[0]
    one = jnp.ones((n, HEAD_DIM - ROT), F32)
    zero_r = jnp.zeros((n, HEAD_DIM - ROT), F32)
    zero_h = jnp.zeros((n, ROT_HALF), F32)
    c = jnp.concatenate([cos, cos, one], axis=1)
    s1 = jnp.concatenate([-sin, zero_h, zero_r], axis=1)
    s2 = jnp.concatenate([zero_h, sin, zero_r], axis=1)
    rep = LANES // HEAD_DIM
    return jnp.tile(c, (1, rep)), jnp.tile(s1, (1, rep)), jnp.tile(s2, (1, rep))


def _rope128(x, c, s1, s2):
    return x * c + pltpu.roll(x, LANES - ROT_HALF, 1) * s1 + pltpu.roll(x, ROT_HALF, 1) * s2


def _mix_in_kernel(x_ref, sh_ref, sc_ref, gpre_ref, w_ref, rc_ref, rs1_ref, rs2_ref,
                   q_ref, kt_ref, vt_ref, vb_ref, qi_ref, kiwi_ref, kiwit_ref, gb_ref, u_ref, *, conv_ch):
    x = x_ref[...]
    hb = (_rms(x) * gpre_ref[...] * (1.0 + sc_ref[0]) + sh_ref[0]).astype(BF16)
    c, s1, s2 = rc_ref[...], rs1_ref[...], rs2_ref[...]

    def proj(off, width):
        return jnp.dot(hb, w_ref[:, off:off + width], preferred_element_type=F32)

    def rope_cols(z):
        return [_rope128(z[:, g * LANES:(g + 1) * LANES], c, s1, s2) for g in range(z.shape[1] // LANES)]

    lane = lax.broadcasted_iota(I32, (x.shape[0], LANES), 1)
    low = lane < HEAD_DIM

    zq = proj(_OFF_Q, ATTN_WIDTH)
    for g, r in enumerate(rope_cols(zq)):
        r = r * Q_SCALE
        swapped = pltpu.roll(r, HEAD_DIM, 1)
        for e in range(2):
            h = 2 * g + e
            kv_group = h // HEADS_PER_KV
            src = r if e == kv_group else swapped
            keep = low if kv_group == 0 else jnp.logical_not(low)
            q_ref[h] = jnp.where(keep, src, 0.0).astype(BF16)
    (kr,) = rope_cols(proj(_OFF_K, KV_WIDTH))
    kt_ref[...] = kr.T
    pv = proj(_OFF_V, KV_WIDTH)
    vt_ref[...] = pv.T
    vb_ref[...] = pv.astype(BF16)
    zqi = proj(_OFF_QI, IDX_HEADS * IDX_DIM)
    for g, r in enumerate(rope_cols(zqi)):
        qi_ref[:, (2 * g) * LANES:(2 * g + 1) * LANES] = jnp.where(low, r, 0.0).astype(BF16)
        qi_ref[:, (2 * g + 1) * LANES:(2 * g + 2) * LANES] = jnp.where(low, 0.0, r).astype(BF16)
    zkw = proj(_OFF_KIWI, LANES)
    kiwi = jnp.where(lane < IDX_DIM, _rope128(zkw, c, s1, s2), zkw * IDX_SCALE)
    kiwi_ref[...] = kiwi
    kiwit_ref[...] = kiwi.T
    gb_ref[...] = proj(_OFF_GB, conv_ch)
    u_ref[...] = proj(_OFF_GB + conv_ch, conv_ch) * proj(_OFF_GB + 2 * conv_ch, conv_ch)


def _mix_in(x, shift, scale, g_pre, w_in_p, rope_tabs, tm, tiles_per_group, rope_tiles, conv_ch):
    rows, d = x.shape
    const = lambda i: (0, 0)
    row = lambda i: (i, 0)
    rope_spec = pl.BlockSpec((tm, LANES), lambda i: (i % rope_tiles, 0))
    assert 2 * HEAD_DIM == LANES and KV_WIDTH == LANES and 2 * IDX_DIM == LANES
    groups = rows // (tm * tiles_per_group)
    flat = lambda w, dt: (pl.BlockSpec((tm, w), row), jax.ShapeDtypeStruct((rows, w), dt))
    transposed = lambda w: (pl.BlockSpec((None, w, tm), lambda i: (i // tiles_per_group, 0, i % tiles_per_group)),
                            jax.ShapeDtypeStruct((groups, w, tm * tiles_per_group), F32))
    q_out = (pl.BlockSpec((N_HEADS, tm, LANES), lambda i: (0, i, 0)),
             jax.ShapeDtypeStruct((N_HEADS, rows, LANES), BF16))
    outs = [q_out, transposed(KV_WIDTH), transposed(KV_WIDTH), flat(KV_WIDTH, BF16), flat(IDX_HEADS * LANES, BF16),
            flat(LANES, F32), transposed(LANES), flat(conv_ch, F32), flat(conv_ch, F32)]
    return pl.pallas_call(
        functools.partial(_mix_in_kernel, conv_ch=conv_ch),
        grid=(rows // tm,),
        in_specs=[pl.BlockSpec((tm, d), row),
                  _mod_spec(shift, tm, tiles_per_group),
                  _mod_spec(scale, tm, tiles_per_group),
                  pl.BlockSpec((1, d), const),
                  pl.BlockSpec(w_in_p.shape, const),
                  rope_spec, rope_spec, rope_spec],
        out_specs=[spec for spec, _ in outs],
        out_shape=[shape for _, shape in outs],
        compiler_params=_cparams(("arbitrary",)),
        name="mix_in",
    )(x, shift, scale, g_pre, w_in_p, *rope_tabs)


def _key_to_float(u):
    o = u ^ INT_MIN
    bits = jnp.where(o >= 0, o, o ^ INT_MAX)
    return lax.bitcast_convert_type(bits, F32)


def _count(sc_ref, nkb, r0, rg, tk, pred):
    rows = slice(r0, r0 + rg)

    def body(kb, acc):
        for j in range(tk // LANES):
            x = sc_ref[kb, rows, j * LANES:(j + 1) * LANES]
            kpos = kb * tk + j * LANES + lax.broadcasted_iota(I32, (rg, LANES), 1)
            acc = acc + jnp.where(pred(x, kpos, rows), 1.0, 0.0)
        return acc

    group = 4

    def group_body(i, acc):
        for g in range(group):
            acc = body(group * i + g, acc)
        return acc

    acc = lax.fori_loop(0, nkb // group, group_body, jnp.zeros((rg, LANES), F32))
    return lax.fori_loop(group * (nkb // group), nkb, body, acc)


def _select_rows(cnt, rg, topk, idx_bits, row_ok, tie_fn=None, two_bits=False):
    def count_ge(cand):
        t = _key_to_float(cand)
        return cnt(lambda x, kp, rs: x >= t[rs])

    def bit_body(i, carry):
        key, c_ge = carry
        cand = key | jnp.left_shift(jnp.int32(1), 31 - i)
        c = count_ge(cand)
        ok = c >= topk
        return jnp.where(ok, cand, key), jnp.where(ok, c, c_ge)

    def two_bit_body(i, carry):
        key, c_ge = carry
        sh = 30 - 2 * i
        c1, c2, c3 = (key | jnp.left_shift(jnp.int32(v), sh) for v in (2, 1, 3))
        n1, n2, n3 = count_ge(c1), count_ge(c2), count_ge(c3)
        ok1, ok2, ok3 = n1 >= topk, n2 >= topk, n3 >= topk
        key = jnp.where(ok3, c3, jnp.where(ok1, c1, jnp.where(ok2, c2, key)))
        c_ge = jnp.where(ok3, n3, jnp.where(ok1, n1, jnp.where(ok2, n2, c_ge)))
        return key, c_ge

    init = (jnp.zeros((rg, LANES), I32), jnp.zeros((rg, LANES), F32))
    key, c_ge = lax.fori_loop(0, 16, two_bit_body, init) if two_bits else lax.fori_loop(0, 32, bit_body, init)
    thr = _key_to_float(key)
    enough = thr >= -F32_MAX
    thr = jnp.where(enough, thr, -F32_MAX)
    excess = jnp.max(jnp.where((c_ge > topk) & enough & row_ok, 1.0, 0.0)) > 0.5

    def tie_search():
        if tie_fn is not None:
            return tie_fn(thr, c_ge)
        need = topk - cnt(lambda x, kp, rs: x > thr[rs])

        def jbody(i, p):
            cand = p | jnp.left_shift(jnp.int32(1), idx_bits - 1 - i)
            c = cnt(lambda x, kp, rs: (x == thr[rs]) & (kp < cand[rs]))
            return jnp.where(c < need, cand, p)
        return lax.fori_loop(0, idx_bits, jbody, jnp.zeros((rg, LANES), I32))

    jmax = lax.cond(excess, tie_search, lambda: jnp.full((rg, LANES), INT_MAX, I32))
    return thr, jmax


def _tie_bound_blocked(sc_ref, cnt_ref, nkb, r0, rg, tk, topk, thr, c_ge):
    reps = tk // LANES
    thr_t = _tile_lanes(thr, reps)

    group = 8

    def count_body(i, carry):
        for g in range(group):
            kb = jnp.minimum(group * i + g, nkb - 1)
            eq = jnp.where(sc_ref[kb, r0:r0 + rg, :] == thr_t, 1.0, 0.0)
            c = sum(eq[:, j * LANES:(j + 1) * LANES] for j in range(reps))
            cnt_ref[kb] = jnp.broadcast_to(jnp.sum(c, axis=1, keepdims=True), (rg, LANES))
        return carry

    lax.fori_loop(0, (nkb + group - 1) // group, count_body, 0)
    ties = lax.fori_loop(0, nkb, lambda kb, tot: tot + cnt_ref[kb], jnp.zeros((rg, LANES), F32))
    need = topk - (c_ge - ties)

    def locate_body(kb, carry):
        run, kstar, before = carry
        new_run = run + cnt_ref[kb]
        hit = (run < need) & (new_run >= need)
        return new_run, jnp.where(hit, kb, kstar), jnp.where(hit, run, before)

    zeros = jnp.zeros((rg, LANES), F32)
    _, kstar, before = lax.fori_loop(0, nkb, locate_body, (zeros, jnp.full((rg, LANES), -1, I32), zeros))
    kstar_t = _tile_lanes(kstar, reps)

    def gather_body(kb, own):
        eq = jnp.where(sc_ref[kb, r0:r0 + rg, :] == thr_t, 1.0, 0.0)
        return jnp.where(kstar_t == kb, eq, own)

    own = lax.fori_loop(0, nkb, gather_body, jnp.zeros((rg, tk), F32))
    need_in = need - before
    upper = jnp.where(lax.broadcasted_iota(I32, (tk, tk), 0) <= lax.broadcasted_iota(I32, (tk, tk), 1), 1.0, 0.0)
    running = jnp.dot(own.astype(BF16), upper.astype(BF16), preferred_element_type=F32)
    p = jnp.sum(jnp.where(running < _tile_lanes(need_in, reps), 1.0, 0.0), axis=1, keepdims=True).astype(I32)
    return jnp.where(kstar >= 0, kstar * tk + p, INT_MAX)


def _select(sc_ref, cnt_ref, t_ref, j_ref, nkb, rows, rg, tk, topk, idx_bits):
    groups = range(0, rows, rg)

    def cnt(pred):
        acc = jnp.concatenate([_count(sc_ref, nkb, r0, rg, tk, pred) for r0 in groups], axis=0)
        return jnp.broadcast_to(jnp.sum(acc, axis=1, keepdims=True), (rows, LANES))

    def tie_fn(thr, c_ge):
        return jnp.concatenate(
            [_tie_bound_blocked(sc_ref, cnt_ref, nkb, r0, rg, tk, topk, thr[r0:r0 + rg], c_ge[r0:r0 + rg])
             for r0 in groups], axis=0)

    thr, jmax = _select_rows(cnt, rows, topk, idx_bits, True, tie_fn)
    t_ref[...] = thr
    j_ref[...] = jmax


def _sel_mask(scb, kpos, thr, jmax):
    return (scb > thr) | ((scb == thr) & (kpos <= jmax))


def _prompt_attn_kernel(q_ref, qi_ref, kiwi_ref, kt_ref, v_ref, kit_ref, o_ref,
                        sc_ref, cnt_ref, wib_ref, m_ref, l_ref, acc_ref, t_ref, j_ref,
                        *, tq, topk, idx_bits):
    tk = tq
    reps = tk // LANES
    i = pl.program_id(1)
    nkb = i + 1

    kiwi = kiwi_ref[...]
    for h in range(IDX_HEADS):
        wib_ref[h] = jnp.broadcast_to(kiwi[:, IDX_DIM + h:IDX_DIM + h + 1], (tq, LANES))

    def score_body(kb, carry):
        kit = kit_ref[kb]
        acc = None
        for h in range(IDX_HEADS):
            s = jnp.dot(qi_ref[:, h * LANES:(h + 1) * LANES], kit, preferred_element_type=F32)
            t = jnp.maximum(s, 0.0) * _tile_lanes(wib_ref[h], reps)
            acc = t if acc is None else acc + t
        sc_ref[kb] = acc
        return carry

    def score_group(j, carry):
        for g in range(4):
            carry = score_body(4 * j + g, carry)
        return carry

    lax.fori_loop(0, nkb // 4, score_group, 0)
    lax.fori_loop(4 * (nkb // 4), nkb, score_body, 0)
    row = lax.broadcasted_iota(I32, (tq, tk), 0)
    col = lax.broadcasted_iota(I32, (tq, tk), 1)
    sc_ref[i] = jnp.where(col <= row, sc_ref[i], -jnp.inf)

    _select(sc_ref, cnt_ref, t_ref, j_ref, nkb, tq, min(128, tq), tk, topk, idx_bits)

    def bias_body(kb, carry):
        thr = _tile_lanes(t_ref[...], reps)
        jmax = _tile_lanes(j_ref[...], reps)
        sc_ref[kb] = jnp.where(_sel_mask(sc_ref[kb], kb * tk + col, thr, jmax), 0.0, NEG_BIG)
        return carry

    lax.fori_loop(0, nkb, bias_body, 0)

    m_ref[...] = jnp.full(m_ref.shape, NEG_BIG, F32)
    l_ref[...] = jnp.zeros(l_ref.shape, F32)
    acc_ref[...] = jnp.zeros(acc_ref.shape, F32)

    def attn_body(kb, carry):
        kt = kt_ref[kb]
        vb = v_ref[pl.ds(pl.multiple_of(kb * tk, tk), tk), :]
        bias = sc_ref[kb]
        for h in range(N_HEADS):
            s = jnp.dot(q_ref[h], kt, preferred_element_type=F32) + bias
            m_prev = m_ref[h]
            m_new = jnp.maximum(m_prev, jnp.max(s, axis=1, keepdims=True))
            alpha = jnp.exp2(m_prev - m_new)
            p = jnp.exp2(s - _tile_lanes(m_new, reps))
            l_ref[h] = alpha * l_ref[h] + sum(p[:, j * LANES:(j + 1) * LANES] for j in range(reps))
            acc_ref[h] = alpha * acc_ref[h] + jnp.dot(p.astype(BF16), vb, preferred_element_type=F32)
            m_ref[h] = m_new
        return carry

    def attn_group(j, carry):
        for g in range(4):
            carry = attn_body(4 * j + g, carry)
        return carry

    lax.fori_loop(0, nkb // 4, attn_group, 0)
    lax.fori_loop(4 * (nkb // 4), nkb, attn_body, 0)

    for h in range(N_HEADS):
        g = h // HEADS_PER_KV
        o = acc_ref[h] / jnp.sum(l_ref[h], axis=1, keepdims=True)
        o_ref[:, h * HEAD_DIM:(h + 1) * HEAD_DIM] = o[:, g * HEAD_DIM:(g + 1) * HEAD_DIM]


def _prompt_attn(q, qi, kiwi, kt, v, kit, tq, topk):
    b, s, _ = qi.shape
    nb = s // tq
    idx_bits = int(s).bit_length()
    blk = lambda w: pl.BlockSpec((None, tq, w), lambda bi, i: (bi, i, 0))
    full = lambda shp: pl.BlockSpec((None,) + shp, lambda bi, i: (bi,) + (0,) * len(shp),
                                    pipeline_mode=pl.Buffered(1))
    return pl.pallas_call(
        functools.partial(_prompt_attn_kernel, tq=tq, topk=topk, idx_bits=idx_bits),
        grid=(b, nb),
        in_specs=[pl.BlockSpec((N_HEADS, tq, LANES), lambda bi, i: (0, bi * nb + i, 0)),
                  blk(IDX_HEADS * LANES), blk(LANES),
                  full((nb, KV_WIDTH, tq)), full((s, KV_WIDTH)), full((nb, 2 * IDX_DIM, tq))],
        out_specs=blk(ATTN_WIDTH),
        out_shape=jax.ShapeDtypeStruct((b, s, ATTN_WIDTH), F32),
        scratch_shapes=[pltpu.VMEM((nb, tq, tq), F32),
                        pltpu.VMEM((nb, min(128, tq), LANES), F32),
                        pltpu.VMEM((IDX_HEADS, tq, LANES), F32),
                        pltpu.VMEM((N_HEADS, tq, LANES), F32),
                        pltpu.VMEM((N_HEADS, tq, LANES), F32),
                        pltpu.VMEM((N_HEADS, tq, KV_WIDTH), F32),
                        pltpu.VMEM((tq, LANES), F32),
                        pltpu.VMEM((tq, LANES), I32)],
        compiler_params=_cparams(("arbitrary", "arbitrary")),
        name="prompt_attn",
    )(q, qi, kiwi, kt, v, kit)


def _sample_attn_kernel(pt_ref, q_ref, qih_ref, wib_ref, kin_ref, kn_ref, vn_ref, *rest,
                        n_steps, pps, n_tok, tpad, page, topk, idx_bits):
    cik_refs, ck_refs, cv_refs = rest[:pps], rest[pps:2 * pps], rest[2 * pps:3 * pps]
    o_ref, sc_ref, sa_ref, vs_ref = rest[3 * pps:]
    step = pl.program_id(1)
    n_pages = n_steps * pps
    nkb = n_pages + 1
    qih = qih_ref[...]
    wib = wib_ref[...]
    q = q_ref[...]

    def idx_score(s):
        acc = None
        for h in range(IDX_HEADS):
            t = jnp.maximum(s[h * tpad:(h + 1) * tpad], 0.0) * wib[h * tpad:(h + 1) * tpad]
            acc = t if acc is None else acc + t
        return acc

    for j in range(pps):
        pg = step * pps + j
        sc_ref[pg] = idx_score(jnp.dot(qih, cik_refs[j][...].astype(BF16), preferred_element_type=F32))
        sa_ref[pg] = jnp.dot(q, ck_refs[j][...].astype(BF16), preferred_element_type=F32)
        vs_ref[pg] = cv_refs[j][...].astype(BF16)

    @pl.when(step == n_steps - 1)
    def _():
        s_new = idx_score(jnp.dot(qih, kin_ref[...], preferred_element_type=F32))
        tok = lax.broadcasted_iota(I32, (tpad, page), 0)
        col = lax.broadcasted_iota(I32, (tpad, page), 1)
        sc_ref[n_pages] = jnp.where((col <= tok) & (col < n_tok), s_new, -jnp.inf)
        sa_ref[n_pages] = jnp.dot(q, kn_ref[...], preferred_element_type=F32)
        vs_ref[n_pages] = vn_ref[...]

        kpos = (lax.broadcasted_iota(I32, (nkb, tpad, page), 0) * page
                + lax.broadcasted_iota(I32, (nkb, tpad, page), 2))

        def cnt(pred):
            w = jnp.where(pred(sc_ref[...], kpos, slice(None)), 1.0, 0.0)
            chunk = -(-nkb // SUBLANES)
            c = sum(jnp.sum(w[a:a + chunk], axis=0) for a in range(0, nkb, chunk))
            return jnp.broadcast_to(jnp.sum(c, axis=1, keepdims=True), c.shape)

        thr, jmax = _select_rows(cnt, tpad, topk, idx_bits, tok[:, :LANES] < n_tok, two_bits=True)
        sc_ref[...] = jnp.where(_sel_mask(sc_ref[...], kpos, thr[None], jmax[None]), 0.0, NEG_BIG)

        heads = N_HEADS
        inv_l = []
        for t in range(n_tok):
            s = sa_ref[:, t * heads:(t + 1) * heads, :] + sc_ref[:, t:t + 1, :]
            m = jnp.max(jnp.max(s, axis=0), axis=1, keepdims=True)
            pr = jnp.exp2(s - m[None])
            sa_ref[:, t * heads:(t + 1) * heads, :] = pr
            l = jnp.sum(jnp.sum(pr, axis=0), axis=1, keepdims=True)
            inv_l.append(jnp.broadcast_to(1.0 / l, (heads, KV_WIDTH)))

        p_all = jnp.concatenate([sa_ref[kb].astype(BF16) for kb in range(nkb)], axis=1)
        vt_all = jnp.concatenate([vs_ref[kb] for kb in range(nkb)], axis=1)
        acc = lax.dot_general(p_all, vt_all, NT_DIMS, preferred_element_type=F32)
        o_ref[...] = acc * jnp.concatenate(inv_l, axis=0)


def _sample_attn(page_table, q, qih, wib, kin_t, kn_t, vn_t, cik_t, ck_t, cv_t, n_tok, tpad, topk, pps):
    n, n_pages = page_table.shape
    page = ck_t.shape[2]
    assert page == LANES and KV_WIDTH == LANES and n_pages % pps == 0
    n_steps = n_pages // pps
    idx_bits = int((n_pages + 1) * page).bit_length()
    rows = n_tok * N_HEADS
    seq = lambda r, w: pl.BlockSpec((None, r, w), lambda b, p, pt: (b, 0, 0))

    def paged(r, j):
        return pl.BlockSpec((None, r, page), lambda b, p, pt: (pt[b, p * pps + j], 0, 0))

    grid_spec = pltpu.PrefetchScalarGridSpec(
        num_scalar_prefetch=1,
        grid=(n, n_steps),
        in_specs=([seq(rows, KV_WIDTH), seq(IDX_HEADS * tpad, IDX_DIM), seq(IDX_HEADS * tpad, LANES),
                   seq(IDX_DIM, page), seq(KV_WIDTH, page), seq(KV_WIDTH, page)]
                  + [paged(IDX_DIM, j) for j in range(pps)]
                  + [paged(KV_WIDTH, j) for j in range(pps)]
                  + [paged(KV_WIDTH, j) for j in range(pps)]),
        out_specs=seq(rows, KV_WIDTH),
        scratch_shapes=[pltpu.VMEM((n_pages + 1, tpad, page), F32),
                        pltpu.VMEM((n_pages + 1, rows, page), F32),
                        pltpu.VMEM((n_pages + 1, KV_WIDTH, page), BF16)],
    )
    return pl.pallas_call(
        functools.partial(_sample_attn_kernel, n_steps=n_steps, pps=pps, n_tok=n_tok, tpad=tpad, page=page,
                          topk=topk, idx_bits=idx_bits),
        grid_spec=grid_spec,
        out_shape=jax.ShapeDtypeStruct((n, rows, KV_WIDTH), F32),
        compiler_params=_cparams(("arbitrary", "arbitrary")),
        name="sample_attn",
    )(page_table, q, qih, wib, kin_t, kn_t, vn_t, *([cik_t] * pps), *([ck_t] * pps), *([cv_t] * pps))


def _mix_out_tail(x, o, gb, u0, um1, um2, wc_ref, wo_ref, gt, gpost, aw):
    y = wc_ref[0:1, :] * um2 + wc_ref[1:2, :] * um1 + wc_ref[2:3, :] * u0
    mixed = (jnp.dot(o.astype(BF16), wo_ref[0:aw, :], preferred_element_type=F32)
             + jnp.dot((gb * y).astype(BF16), wo_ref[aw:, :], preferred_element_type=F32))
    return x + gt * (_rms(mixed) * gpost)


def _mix_out_prompt_kernel(x_ref, o_ref_in, gb_ref, u_ref, halo_ref, gt_ref, gpost_ref, wc_ref, wo_ref,
                           out_ref, ext_ref, *, tm, tiles_per_seq):
    i = pl.program_id(0)
    halo = halo_ref[...]
    ext_ref[0:SUBLANES, :] = jnp.where(i % tiles_per_seq == 0, jnp.zeros_like(halo), halo)
    ext_ref[SUBLANES:, :] = u_ref[...]
    um2 = ext_ref[SUBLANES - 2:SUBLANES - 2 + tm, :]
    um1 = ext_ref[SUBLANES - 1:SUBLANES - 1 + tm, :]
    out_ref[...] = _mix_out_tail(x_ref[...], o_ref_in[...], gb_ref[...], u_ref[...], um1, um2,
                                 wc_ref, wo_ref, gt_ref[0], gpost_ref[...], ATTN_WIDTH)


def _mix_out_prompt(x, o, gb, u, gate, g_post, w_conv, w_o, tm, tiles_per_seq):
    rows, d = x.shape
    cw = u.shape[1]
    const = lambda i: (0, 0)
    row = lambda i: (i, 0)
    hb = tm // SUBLANES
    return pl.pallas_call(
        functools.partial(_mix_out_prompt_kernel, tm=tm, tiles_per_seq=tiles_per_seq),
        grid=(rows // tm,),
        in_specs=[pl.BlockSpec((tm, d), row),
                  pl.BlockSpec((tm, ATTN_WIDTH), row),
                  pl.BlockSpec((tm, cw), row),
                  pl.BlockSpec((tm, cw), row),
                  pl.BlockSpec((SUBLANES, cw), lambda i: (jnp.maximum(i * hb - 1, 0), 0)),
                  _mod_spec(gate, tm, tiles_per_seq),
                  pl.BlockSpec((1, d), const),
                  pl.BlockSpec(w_conv.shape, const),
                  pl.BlockSpec(w_o.shape, const)],
        out_specs=pl.BlockSpec((tm, d), row),
        out_shape=jax.ShapeDtypeStruct((rows, d), F32),
        scratch_shapes=[pltpu.VMEM((tm + SUBLANES, cw), F32)],
        compiler_params=_cparams(("arbitrary",)),
        name="mix_out_prompt",
    )(x, o, gb, u, u, gate, g_post, w_conv, w_o)


def _mix_out_sample_kernel(x_ref, o_ref_in, gb_ref, um2_ref, um1_ref, u0_ref, gt_ref, gpost_ref, wc_ref, wo_ref,
                           out_ref):
    out_ref[...] = _mix_out_tail(x_ref[...], o_ref_in[...], gb_ref[...], u0_ref[...], um1_ref[...], um2_ref[...],
                                 wc_ref, wo_ref, gt_ref[0], gpost_ref[...], ATTN_WIDTH)


def _mix_out_sample(x, o, gb, ext, gate, g_post, w_conv, w_o, tm):
    rows, d = x.shape
    cw = gb.shape[1]
    const = lambda i: (0, 0)
    row = lambda i: (i, 0)
    return pl.pallas_call(
        _mix_out_sample_kernel,
        grid=(rows // tm,),
        in_specs=[pl.BlockSpec((tm, d), row),
                  pl.BlockSpec((tm, ATTN_WIDTH), row),
                  pl.BlockSpec((tm, cw), row),
                  pl.BlockSpec((tm, cw), lambda i: (i, 0)),
                  pl.BlockSpec((tm, cw), lambda i: (i + 1, 0)),
                  pl.BlockSpec((tm, cw), lambda i: (i + 2, 0)),
                  pl.BlockSpec((1, tm, d), lambda i: (0, 0, 0)),
                  pl.BlockSpec((1, d), const),
                  pl.BlockSpec(w_conv.shape, const),
                  pl.BlockSpec(w_o.shape, const)],
        out_specs=pl.BlockSpec((tm, d), row),
        out_shape=jax.ShapeDtypeStruct((rows, d), F32),
        compiler_params=_cparams(("arbitrary",)),
        name="mix_out_sample",
    )(x, o, gb, ext, ext, ext, gate, g_post, w_conv, w_o)


def _prompt_layer(x, mod, lw, tm, tq):
    b, s, d = x.shape
    conv_ch = lw["w_conv"].shape[1]
    tps = s // tm
    md = lambda j, k: mod[:, j, k][:, None, :]
    xr = x.reshape(b * s, d)
    x1 = _ffn(xr, md(0, 0), md(0, 1), md(0, 2), lw["g_pre"][0:1], lw["g_post"][0:1], lw["ffn1_w13"], lw["ffn1_w2"],
              tm, tps)
    tabs = _rope_tables(jnp.arange(s))
    q, k_t, v_t, v_bf, qi, kiwi, kiwi_t, gb, u = _mix_in(x1, md(1, 0), md(1, 1), lw["g_pre"][1:2], lw["w_in_p"], tabs,
                                                         tm, tps, tps, conv_ch)
    nb = s // tq
    ki_t = kiwi_t[:, :IDX_DIM]
    blocks = lambda a: a.astype(BF16).reshape(b, a.shape[1], nb, tq).swapaxes(1, 2)
    kt = blocks(k_t)
    kit = blocks(jnp.concatenate([ki_t, ki_t], axis=1))
    topk = min(TOPK_MAX, s // 4)
    o = _prompt_attn(q, qi.reshape(b, s, -1), kiwi.reshape(b, s, -1), kt, v_bf.reshape(b, s, KV_WIDTH), kit, tq, topk)
    x2 = _mix_out_prompt(x1, o.reshape(b * s, -1), gb, u, md(1, 2), lw["g_post"][1:2], lw["w_conv"], lw["w_o"],
                         tm, tps)
    x3 = _ffn(x2, md(2, 0), md(2, 1), md(2, 2), lw["g_pre"][2:3], lw["g_post"][2:3], lw["ffn2_w13"], lw["ffn2_w2"],
              tm, tps)
    heads_last = lambda a: a.reshape(b, N_KV_HEADS, HEAD_DIM, s).transpose(0, 3, 1, 2)
    state = (heads_last(k_t), heads_last(v_t), ki_t.transpose(0, 2, 1),
             u.reshape(b, s, conv_ch)[:, s - (CONV_K - 1):])
    return x3.reshape(b, s, d), state


def _sample_layer(x_tm, mod, lw, cache_k, cache_v, cache_idx_k, state_conv, page_table):
    n = mod.shape[0]
    t_tok = x_tm.shape[0] // n
    d = x_tm.shape[1]
    conv_ch = lw["w_conv"].shape[1]
    n_pages = page_table.shape[1]
    page = cache_k.shape[1]
    past = n_pages * page
    md = lambda j, k: mod[:, j, k][None]
    x1 = _ffn(x_tm, md(0, 0), md(0, 1), md(0, 2), lw["g_pre"][0:1], lw["g_post"][0:1], lw["ffn1_w13"],
              lw["ffn1_w2"], n, t_tok)
    pos = jnp.repeat(past + jnp.arange(t_tok), n)
    tabs = _rope_tables(pos)
    q, k_t, v_t, _, qi, kiwi, kiwi_t, gb, u = _mix_in(x1, md(1, 0), md(1, 1), lw["g_pre"][1:2], lw["w_in_p"], tabs,
                                                      n, t_tok, t_tok, conv_ch)
    per_seq_t = lambda a: a[0].reshape(a.shape[1], t_tok, n).transpose(2, 0, 1)
    k_s, v_s, ki_s = per_seq_t(k_t), per_seq_t(v_t), per_seq_t(kiwi_t[:, :IDX_DIM])
    tpad = SUBLANES
    assert t_tok <= tpad

    def seq_major(a, heads, width):
        a = a.reshape(t_tok, n, heads, width).transpose(1, 2, 0, 3)
        a = jnp.pad(a, ((0, 0), (0, 0), (0, tpad - t_tok), (0, 0)))
        return a.reshape(n, heads * tpad, width)

    q_seq = q.reshape(N_HEADS, t_tok, n, LANES).transpose(2, 1, 0, 3).reshape(n, t_tok * N_HEADS, LANES)
    qi_nat = jnp.concatenate([qi[:, h * LANES + (h % 2) * IDX_DIM:h * LANES + (h % 2 + 1) * IDX_DIM]
                              for h in range(IDX_HEADS)], axis=1)
    qih = seq_major(qi_nat, IDX_HEADS, IDX_DIM)
    wib = jnp.broadcast_to(seq_major(kiwi[:, IDX_DIM:IDX_DIM + IDX_HEADS], IDX_HEADS, 1),
                           (n, IDX_HEADS * tpad, LANES))

    def new_keys_t(a):
        return jnp.pad(a.astype(BF16), ((0, 0), (0, 0), (0, page - t_tok)))

    ck_t = cache_k.transpose(0, 2, 3, 1).reshape(-1, KV_WIDTH, page)
    cv_t = cache_v.transpose(0, 2, 3, 1).reshape(-1, KV_WIDTH, page)
    cik_t = cache_idx_k.transpose(0, 2, 1)
    topk = min(TOPK_MAX, (past + t_tok) // 4)
    pps = max(p for p in (16, 8, 4, 2, 1) if n_pages % p == 0)
    o = _sample_attn(page_table, q_seq, qih, wib, new_keys_t(ki_s), new_keys_t(k_s), new_keys_t(v_s),
                     cik_t, ck_t, cv_t, t_tok, tpad, topk, pps)
    o = o.reshape(n, t_tok, N_KV_HEADS, HEADS_PER_KV, N_KV_HEADS, HEAD_DIM)
    o = jnp.stack([o[:, :, g, :, g] for g in range(N_KV_HEADS)], axis=2)
    o_tm = o.transpose(1, 0, 2, 3, 4).reshape(t_tok * n, ATTN_WIDTH)
    ext = jnp.concatenate([state_conv.transpose(1, 0, 2).reshape(-1, conv_ch), u], axis=0)
    x2 = _mix_out_sample(x1, o_tm, gb, ext, md(1, 2), lw["g_post"][1:2], lw["w_conv"], lw["w_o"], n)
    x3 = _ffn(x2, md(2, 0), md(2, 1), md(2, 2), lw["g_pre"][2:3], lw["g_post"][2:3], lw["ffn2_w13"],
              lw["ffn2_w2"], n, t_tok)
    tm_to_seq = lambda a: a.reshape((t_tok, n) + a.shape[1:]).swapaxes(0, 1)
    heads_last = lambda a: a.transpose(0, 2, 1).reshape(n, t_tok, N_KV_HEADS, HEAD_DIM)
    state = (heads_last(k_s), heads_last(v_s), ki_s.transpose(0, 2, 1), tm_to_seq(u)[:, t_tok - (CONV_K - 1):])
    return x3, state


def _row_tile(s, target):
    t = min(target, s)
    while s % t:
        t //= 2
    return t


def kernel(x_prompt, x_sample, cache_k, cache_v, cache_idx_k, state_conv, page_table, c_prompt, c_sample, w_ada, b_ada, g_pre, g_post, ffn1_w13, ffn1_w2, w_in, w_conv, w_o, ffn2_w13, ffn2_w2):
    depth = w_ada.shape[0]
    b, s, d = x_prompt.shape
    n, t_tok, _ = x_sample.shape
    conv_ch = w_conv.shape[2]
    tm = _row_tile(s, 512)
    tq = _row_tile(s, 256)

    yp = x_prompt
    ys = x_sample.swapaxes(0, 1).reshape(t_tok * n, d)
    c_all = jnp.concatenate([c_prompt, c_sample], axis=0)
    n_c = c_all.shape[0]
    c_all = jnp.pad(c_all, ((0, -n_c % SUBLANES), (0, 0)))
    outs = [[] for _ in range(8)]
    for l in range(depth):
        lw = dict(g_pre=g_pre[l], g_post=g_post[l], w_conv=w_conv[l],
                  ffn1_w13=ffn1_w13[l].astype(BF16), ffn1_w2=ffn1_w2[l].astype(BF16),
                  ffn2_w13=ffn2_w13[l].astype(BF16), ffn2_w2=ffn2_w2[l].astype(BF16),
                  w_in_p=_pack_w_in(w_in[l], conv_ch), w_o=w_o[l].astype(BF16))
        mod = _ada_mod(c_all, w_ada[l], b_ada[l])[:n_c].reshape(n_c, N_SUB, 3, d)
        yp, st_p = _prompt_layer(yp, mod[:b], lw, tm, tq)
        ys, st_s = _sample_layer(ys, mod[b:], lw, cache_k[l], cache_v[l], cache_idx_k[l], state_conv[l], page_table)
        for lst, a in zip(outs, st_p + st_s):
            lst.append(a)
    ys = ys.reshape(t_tok, n, d).swapaxes(0, 1)
    return (yp, ys) + tuple(jnp.stack(lst) for lst in outs)
```

```python
import functools

import numpy as np
import jax
import jax.numpy as jnp
from jax import lax
from jax.experimental import pallas as pl
from jax.experimental.pallas import tpu as pltpu

F32 = jnp.float32
BF16 = jnp.bfloat16
I32 = jnp.int32

N_HEADS = 8
HEAD_DIM = 64
N_KV_HEADS = 2
HEADS_PER_KV = N_HEADS // N_KV_HEADS
ATTN_WIDTH = N_HEADS * HEAD_DIM
KV_WIDTH = N_KV_HEADS * HEAD_DIM
ROT_DIV = 4
ROT = HEAD_DIM // ROT_DIV
ROT_HALF = ROT // 2
ROPE_THETA = 500000.0
IDX_HEADS = 4
IDX_DIM = 64
IDX_SCALE = (IDX_HEADS * IDX_DIM) ** -0.5
TOPK_MAX = 256
CONV_K = 3
N_SUB = 3
RMS_EPS = 1e-6
Q_SCALE = HEAD_DIM ** -0.5 * float(np.log2(np.e))

LANES = 128
SUBLANES = 8
VMEM_LIMIT = 56 * 1024 * 1024

NEG_BIG = -1e30
F32_MAX = float(np.finfo(np.float32).max)
INT_MIN = -(2 ** 31)
INT_MAX = 2 ** 31 - 1

NT_DIMS = (((1,), (1,)), ((), ()))


def _cparams(sem):
    return pltpu.CompilerParams(dimension_semantics=sem, vmem_limit_bytes=VMEM_LIMIT)


def _silu(x):
    return x * (1.0 / (1.0 + jnp.exp(-x)))


def _rms(x):
    return x * lax.rsqrt(jnp.mean(x * x, axis=-1, keepdims=True) + RMS_EPS)


def _tile_lanes(x, reps):
    return x if reps == 1 else jnp.concatenate([x] * reps, axis=1)


def _grouped_loop(n, body, init, group=4):
    def step(j, carry):
        for g in range(group):
            carry = body(group * j + g, carry)
        return carry

    carry = lax.fori_loop(0, n // group, step, init)
    return lax.fori_loop(group * (n // group), n, body, carry)


def _ada_kernel(c_ref, w_ref, b_ref, o_ref):
    a = _silu(c_ref[...]).astype(BF16)
    o_ref[...] = jnp.dot(a, w_ref[...].astype(BF16), preferred_element_type=F32) + b_ref[...]


def _ada_mod(c, w_ada, b_ada):
    n, d = c.shape
    nout = w_ada.shape[1]
    tn = 9 * LANES if nout % (9 * LANES) == 0 else LANES
    return pl.pallas_call(
        _ada_kernel,
        grid=(nout // tn,),
        in_specs=[pl.BlockSpec((n, d), lambda j: (0, 0)),
                  pl.BlockSpec((d, tn), lambda j: (0, j)),
                  pl.BlockSpec((1, tn), lambda j: (0, j))],
        out_specs=pl.BlockSpec((n, tn), lambda j: (0, j)),
        out_shape=jax.ShapeDtypeStruct((n, nout), F32),
        compiler_params=_cparams(("arbitrary",)),
        name="ada_mod",
    )(c, w_ada, b_ada.reshape(1, nout))


def _ffn_kernel(x_ref, sh_ref, sc_ref, gt_ref, gpre_ref, gpost_ref, w13_ref, w2_ref, o_ref, *, d_ff, chunk):
    x = x_ref[...]
    h = _rms(x) * gpre_ref[...] * (1.0 + sc_ref[0]) + sh_ref[0]
    hb = h.astype(BF16)
    y = jnp.zeros(x.shape, F32)
    for c in range(d_ff // chunk):
        a = jnp.dot(hb, w13_ref[:, c * chunk:(c + 1) * chunk], preferred_element_type=F32)
        b = jnp.dot(hb, w13_ref[:, d_ff + c * chunk:d_ff + (c + 1) * chunk], preferred_element_type=F32)
        g = (_silu(a) * b).astype(BF16)
        y = y + jnp.dot(g, w2_ref[c * chunk:(c + 1) * chunk, :], preferred_element_type=F32)
    o_ref[...] = x + 0.5 * gt_ref[0] * (_rms(y) * gpost_ref[...])


def _mod_spec(mod, tm, tiles_per_group):
    _, r, d = mod.shape
    return pl.BlockSpec((1, r, d), lambda i: (i // tiles_per_group, 0, 0))


def _ffn(x, shift, scale, gate, g_pre, g_post, w13, w2, tm, tiles_per_group):
    rows, d = x.shape
    d_ff = w2.shape[0]
    const = lambda i: (0, 0)
    return pl.pallas_call(
        functools.partial(_ffn_kernel, d_ff=d_ff, chunk=min(512, d_ff)),
        grid=(rows // tm,),
        in_specs=[pl.BlockSpec((tm, d), lambda i: (i, 0)),
                  _mod_spec(shift, tm, tiles_per_group),
                  _mod_spec(scale, tm, tiles_per_group),
                  _mod_spec(gate, tm, tiles_per_group),
                  pl.BlockSpec((1, d), const),
                  pl.BlockSpec((1, d), const),
                  pl.BlockSpec(w13.shape, const),
                  pl.BlockSpec(w2.shape, const)],
        out_specs=pl.BlockSpec((tm, d), lambda i: (i, 0)),
        out_shape=jax.ShapeDtypeStruct((rows, d), F32),
        compiler_params=_cparams(("arbitrary",)),
        name="ffn",
    )(x, shift, scale, gate, g_pre, g_post, w13, w2)


_OFF_Q = 0
_OFF_K = _OFF_Q + ATTN_WIDTH
_OFF_V = _OFF_K + KV_WIDTH
_OFF_QI = _OFF_V + KV_WIDTH
_OFF_KIWI = _OFF_QI + IDX_HEADS * IDX_DIM
_OFF_GB = _OFF_KIWI + LANES


def _pack_w_in(w_in, conv_ch):
    splits = [ATTN_WIDTH, KV_WIDTH, KV_WIDTH, IDX_HEADS * IDX_DIM, IDX_DIM, IDX_HEADS, conv_ch, conv_ch, conv_ch]
    offs = np.cumsum([0] + splits)
    d = w_in.shape[0]
    pad = jnp.zeros((d, LANES - IDX_DIM - IDX_HEADS), w_in.dtype)
    return jnp.concatenate([w_in[:, :offs[6]], pad, w_in[:, offs[6]:]], axis=1).astype(BF16)


def _rope_tables(pos):
    inv = jnp.power(ROPE_THETA, -jnp.arange(ROT_HALF, dtype=F32) * 2.0 / ROT)
    ang = pos.astype(F32)[:, None] * inv[None, :]
    cos, sin = jnp.cos(ang), jnp.sin(ang)
    n = pos.shape[0]
    one = jnp.ones((n, HEAD_DIM - ROT), F32)
    zero_r = jnp.zeros((n, HEAD_DIM - ROT), F32)
    zero_h = jnp.zeros((n, ROT_HALF), F32)
    c = jnp.concatenate([cos, cos, one], axis=1)
    s1 = jnp.concatenate([-sin, zero_h, zero_r], axis=1)
    s2 = jnp.concatenate([zero_h, sin, zero_r], axis=1)
    rep = LANES // HEAD_DIM
    return jnp.tile(c, (1, rep)), jnp.tile(s1, (1, rep)), jnp.tile(s2, (1, rep))


def _rope128(x, c, s1, s2):
    return x * c + pltpu.roll(x, LANES - ROT_HALF, 1) * s1 + pltpu.roll(x, ROT_HALF, 1) * s2


def _mix_in_kernel(x_ref, sh_ref, sc_ref, gpre_ref, w_ref, rc_ref, rs1_ref, rs2_ref,
                   q_ref, kt_ref, vt_ref, vb_ref, qi_ref, kiwi_ref, kiwit_ref, gb_ref, u_ref, *, conv_ch):
    x = x_ref[...]
    hb = (_rms(x) * gpre_ref[...] * (1.0 + sc_ref[0]) + sh_ref[0]).astype(BF16)
    c, s1, s2 = rc_ref[...], rs1_ref[...], rs2_ref[...]

    def proj(off, width):
        return jnp.dot(hb, w_ref[:, off:off + width], preferred_element_type=F32)

    def rope_cols(z):
        return [_rope128(z[:, g * LANES:(g + 1) * LANES], c, s1, s2) for g in range(z.shape[1] // LANES)]

    lane = lax.broadcasted_iota(I32, (x.shape[0], LANES), 1)
    low = lane < HEAD_DIM

    zq = proj(_OFF_Q, ATTN_WIDTH)
    for g, r in enumerate(rope_cols(zq)):
        r = r * Q_SCALE
        swapped = pltpu.roll(r, HEAD_DIM, 1)
        for e in range(2):
            h = 2 * g + e
            kv_group = h // HEADS_PER_KV
            src = r if e == kv_group else swapped
            keep = low if kv_group == 0 else jnp.logical_not(low)
            q_ref[h] = jnp.where(keep, src, 0.0).astype(BF16)
    (kr,) = rope_cols(proj(_OFF_K, KV_WIDTH))
    kt_ref[...] = kr.T
    pv = proj(_OFF_V, KV_WIDTH)
    vt_ref[...] = pv.T
    vb_ref[...] = pv.astype(BF16)
    zqi = proj(_OFF_QI, IDX_HEADS * IDX_DIM)
    for g, r in enumerate(rope_cols(zqi)):
        qi_ref[:, (2 * g) * LANES:(2 * g + 1) * LANES] = jnp.where(low, r, 0.0).astype(BF16)
        qi_ref[:, (2 * g + 1) * LANES:(2 * g + 2) * LANES] = jnp.where(low, 0.0, r).astype(BF16)
    zkw = proj(_OFF_KIWI, LANES)
    kiwi = jnp.where(lane < IDX_DIM, _rope128(zkw, c, s1, s2), zkw * IDX_SCALE)
    kiwi_ref[...] = kiwi
    kiwit_ref[...] = kiwi.T
    gb_ref[...] = proj(_OFF_GB, conv_ch)
    u_ref[...] = proj(_OFF_GB + conv_ch, conv_ch) * proj(_OFF_GB + 2 * conv_ch, conv_ch)


def _mix_in(x, shift, scale, g_pre, w_in_p, rope_tabs, tm, tiles_per_group, rope_tiles, conv_ch):
    rows, d = x.shape
    const = lambda i: (0, 0)
    row = lambda i: (i, 0)
    rope_spec = pl.BlockSpec((tm, LANES), lambda i: (i % rope_tiles, 0))
    assert 2 * HEAD_DIM == LANES and KV_WIDTH == LANES and 2 * IDX_DIM == LANES
    groups = rows // (tm * tiles_per_group)
    flat = lambda w, dt: (pl.BlockSpec((tm, w), row), jax.ShapeDtypeStruct((rows, w), dt))
    transposed = lambda w: (pl.BlockSpec((None, w, tm), lambda i: (i // tiles_per_group, 0, i % tiles_per_group)),
                            jax.ShapeDtypeStruct((groups, w, tm * tiles_per_group), F32))
    q_out = (pl.BlockSpec((N_HEADS, tm, LANES), lambda i: (0, i, 0)),
             jax.ShapeDtypeStruct((N_HEADS, rows, LANES), BF16))
    outs = [q_out, transposed(KV_WIDTH), transposed(KV_WIDTH), flat(KV_WIDTH, BF16), flat(IDX_HEADS * LANES, BF16),
            flat(LANES, F32), transposed(LANES), flat(conv_ch, F32), flat(conv_ch, F32)]
    return pl.pallas_call(
        functools.partial(_mix_in_kernel, conv_ch=conv_ch),
        grid=(rows // tm,),
        in_specs=[pl.BlockSpec((tm, d), row),
                  _mod_spec(shift, tm, tiles_per_group),
                  _mod_spec(scale, tm, tiles_per_group),
                  pl.BlockSpec((1, d), const),
                  pl.BlockSpec(w_in_p.shape, const),
                  rope_spec, rope_spec, rope_spec],
        out_specs=[spec for spec, _ in outs],
        out_shape=[shape for _, shape in outs],
        compiler_params=_cparams(("arbitrary",)),
        name="mix_in",
    )(x, shift, scale, g_pre, w_in_p, *rope_tabs)


def _key_to_float(u):
    o = u ^ INT_MIN
    bits = jnp.where(o >= 0, o, o ^ INT_MAX)
    return lax.bitcast_convert_type(bits, F32)


def _count(sc_ref, nkb, r0, rg, tk, pred):
    rows = slice(r0, r0 + rg)

    def body(kb, acc):
        for j in range(tk // LANES):
            x = sc_ref[kb, rows, j * LANES:(j + 1) * LANES]
            kpos = kb * tk + j * LANES + lax.broadcasted_iota(I32, (rg, LANES), 1)
            acc = acc + jnp.where(pred(x, kpos, rows), 1.0, 0.0)
        return acc

    group = 4

    def group_body(i, acc):
        for g in range(group):
            acc = body(group * i + g, acc)
        return acc

    acc = lax.fori_loop(0, nkb // group, group_body, jnp.zeros((rg, LANES), F32))
    return lax.fori_loop(group * (nkb // group), nkb, body, acc)


def _select_rows(cnt, rg, topk, idx_bits, row_ok, tie_fn=None, two_bits=False):
    def count_ge(cand):
        t = _key_to_float(cand)
        return cnt(lambda x, kp, rs: x >= t[rs])

    def bit_body(i, carry):
        key, c_ge = carry
        cand = key | jnp.left_shift(jnp.int32(1), 31 - i)
        c = count_ge(cand)
        ok = c >= topk
        return jnp.where(ok, cand, key), jnp.where(ok, c, c_ge)

    def two_bit_body(i, carry):
        key, c_ge = carry
        sh = 30 - 2 * i
        c1, c2, c3 = (key | jnp.left_shift(jnp.int32(v), sh) for v in (2, 1, 3))
        n1, n2, n3 = count_ge(c1), count_ge(c2), count_ge(c3)
        ok1, ok2, ok3 = n1 >= topk, n2 >= topk, n3 >= topk
        key = jnp.where(ok3, c3, jnp.where(ok1, c1, jnp.where(ok2, c2, key)))
        c_ge = jnp.where(ok3, n3, jnp.where(ok1, n1, jnp.where(ok2, n2, c_ge)))
        return key, c_ge

    init = (jnp.zeros((rg, LANES), I32), jnp.zeros((rg, LANES), F32))
    key, c_ge = lax.fori_loop(0, 16, two_bit_body, init) if two_bits else lax.fori_loop(0, 32, bit_body, init)
    thr = _key_to_float(key)
    enough = thr >= -F32_MAX
    thr = jnp.where(enough, thr, -F32_MAX)
    excess = jnp.max(jnp.where((c_ge > topk) & enough & row_ok, 1.0, 0.0)) > 0.5

    def tie_search():
        if tie_fn is not None:
            return tie_fn(thr, c_ge)
        need = topk - cnt(lambda x, kp, rs: x > thr[rs])

        def jbody(i, p):
            cand = p | jnp.left_shift(jnp.int32(1), idx_bits - 1 - i)
            c = cnt(lambda x, kp, rs: (x == thr[rs]) & (kp < cand[rs]))
            return jnp.where(c < need, cand, p)
        return lax.fori_loop(0, idx_bits, jbody, jnp.zeros((rg, LANES), I32))

    jmax = lax.cond(excess, tie_search, lambda: jnp.full((rg, LANES), INT_MAX, I32))
    return thr, jmax


def _tie_bound_blocked(sc_ref, cnt_ref, nkb, r0, rg, tk, topk, thr, c_ge):
    reps = tk // LANES
    thr_t = _tile_lanes(thr, reps)

    group = 8

    def count_body(i, carry):
        for g in range(group):
            kb = jnp.minimum(group * i + g, nkb - 1)
            eq = jnp.where(sc_ref[kb, r0:r0 + rg, :] == thr_t, 1.0, 0.0)
            c = sum(eq[:, j * LANES:(j + 1) * LANES] for j in range(reps))
            cnt_ref[kb] = jnp.broadcast_to(jnp.sum(c, axis=1, keepdims=True), (rg, LANES))
        return carry

    lax.fori_loop(0, (nkb + group - 1) // group, count_body, 0)
    ties = lax.fori_loop(0, nkb, lambda kb, tot: tot + cnt_ref[kb], jnp.zeros((rg, LANES), F32))
    need = topk - (c_ge - ties)

    def locate_body(kb, carry):
        run, kstar, before = carry
        new_run = run + cnt_ref[kb]
        hit = (run < need) & (new_run >= need)
        return new_run, jnp.where(hit, kb, kstar), jnp.where(hit, run, before)

    zeros = jnp.zeros((rg, LANES), F32)
    _, kstar, before = _grouped_loop(nkb, locate_body, (zeros, jnp.full((rg, LANES), -1, I32), zeros))
    kstar_t = _tile_lanes(kstar, reps)

    def gather_body(kb, own):
        eq = jnp.where(sc_ref[kb, r0:r0 + rg, :] == thr_t, 1.0, 0.0)
        return jnp.where(kstar_t == kb, eq, own)

    own = _grouped_loop(nkb, gather_body, jnp.zeros((rg, tk), F32))
    need_in = need - before
    upper = jnp.where(lax.broadcasted_iota(I32, (tk, tk), 0) <= lax.broadcasted_iota(I32, (tk, tk), 1), 1.0, 0.0)
    running = jnp.dot(own.astype(BF16), upper.astype(BF16), preferred_element_type=F32)
    p = jnp.sum(jnp.where(running < _tile_lanes(need_in, reps), 1.0, 0.0), axis=1, keepdims=True).astype(I32)
    return jnp.where(kstar >= 0, kstar * tk + p, INT_MAX)


def _select(sc_ref, cnt_ref, t_ref, j_ref, nkb, rows, rg, tk, topk, idx_bits):
    groups = range(0, rows, rg)

    def cnt(pred):
        acc = jnp.concatenate([_count(sc_ref, nkb, r0, rg, tk, pred) for r0 in groups], axis=0)
        return jnp.broadcast_to(jnp.sum(acc, axis=1, keepdims=True), (rows, LANES))

    def tie_fn(thr, c_ge):
        return jnp.concatenate(
            [_tie_bound_blocked(sc_ref, cnt_ref, nkb, r0, rg, tk, topk, thr[r0:r0 + rg], c_ge[r0:r0 + rg])
             for r0 in groups], axis=0)

    thr, jmax = _select_rows(cnt, rows, topk, idx_bits, True, tie_fn)
    t_ref[...] = thr
    j_ref[...] = jmax


def _sel_bias(scb, kpos, thr, jmax):
    tie = jnp.where(kpos <= jmax, 0.0, NEG_BIG)
    return jnp.where(scb > thr, 0.0, jnp.where(scb == thr, tie, NEG_BIG))


def _prompt_attn_kernel(q_ref, qi_ref, kiwi_ref, kt_ref, v_ref, kit_ref, o_ref,
                        sc_ref, cnt_ref, wib_ref, m_ref, l_ref, acc_ref, t_ref, j_ref,
                        *, tq, topk, idx_bits):
    tk = tq
    reps = tk // LANES
    i = pl.program_id(1)
    nkb = i + 1

    kiwi = kiwi_ref[...]
    for h in range(IDX_HEADS):
        wib_ref[h] = jnp.broadcast_to(kiwi[:, IDX_DIM + h:IDX_DIM + h + 1], (tq, LANES))

    def score_body(kb, carry):
        kit = kit_ref[kb]
        acc = None
        for h in range(IDX_HEADS):
            s = jnp.dot(qi_ref[:, h * LANES:(h + 1) * LANES], kit, preferred_element_type=F32)
            t = jnp.maximum(s, 0.0) * _tile_lanes(wib_ref[h], reps)
            acc = t if acc is None else acc + t
        sc_ref[kb] = acc
        return carry

    _grouped_loop(nkb, score_body, 0)
    row = lax.broadcasted_iota(I32, (tq, tk), 0)
    col = lax.broadcasted_iota(I32, (tq, tk), 1)
    sc_ref[i] = jnp.where(col <= row, sc_ref[i], -jnp.inf)

    _select(sc_ref, cnt_ref, t_ref, j_ref, nkb, tq, min(128, tq), tk, topk, idx_bits)

    def bias_body(kb, carry):
        thr = _tile_lanes(t_ref[...], reps)
        jmax = _tile_lanes(j_ref[...], reps)
        sc_ref[kb] = _sel_bias(sc_ref[kb], kb * tk + col, thr, jmax)
        return carry

    lax.fori_loop(0, nkb, bias_body, 0)

    m_ref[...] = jnp.full(m_ref.shape, NEG_BIG, F32)
    l_ref[...] = jnp.zeros(l_ref.shape, F32)
    acc_ref[...] = jnp.zeros(acc_ref.shape, F32)

    def attn_body(kb, carry):
        kt = kt_ref[kb]
        vb = v_ref[pl.ds(pl.multiple_of(kb * tk, tk), tk), :]
        bias = sc_ref[kb]
        for h in range(N_HEADS):
            s = jnp.dot(q_ref[h], kt, preferred_element_type=F32) + bias
            m_prev = m_ref[h]
            m_new = jnp.maximum(m_prev, jnp.max(s, axis=1, keepdims=True))
            alpha = jnp.exp2(m_prev - m_new)
            p = jnp.exp2(s - _tile_lanes(m_new, reps))
            l_ref[h] = alpha * l_ref[h] + sum(p[:, j * LANES:(j + 1) * LANES] for j in range(reps))
            acc_ref[h] = alpha * acc_ref[h] + jnp.dot(p.astype(BF16), vb, preferred_element_type=F32)
            m_ref[h] = m_new
        return carry

    _grouped_loop(nkb, attn_body, 0)

    for h in range(N_HEADS):
        g = h // HEADS_PER_KV
        o = acc_ref[h] / jnp.sum(l_ref[h], axis=1, keepdims=True)
        o_ref[:, h * HEAD_DIM:(h + 1) * HEAD_DIM] = o[:, g * HEAD_DIM:(g + 1) * HEAD_DIM]


def _prompt_attn(q, qi, kiwi, kt, v, kit, tq, topk):
    b, s, _ = qi.shape
    nb = s // tq
    idx_bits = int(s).bit_length()
    blk = lambda w: pl.BlockSpec((None, tq, w), lambda bi, i: (bi, i, 0))
    full = lambda shp: pl.BlockSpec((None,) + shp, lambda bi, i: (bi,) + (0,) * len(shp),
                                    pipeline_mode=pl.Buffered(1))
    return pl.pallas_call(
        functools.partial(_prompt_attn_kernel, tq=tq, topk=topk, idx_bits=idx_bits),
        grid=(b, nb),
        in_specs=[pl.BlockSpec((N_HEADS, tq, LANES), lambda bi, i: (0, bi * nb + i, 0)),
                  blk(IDX_HEADS * LANES), blk(LANES),
                  full((nb, KV_WIDTH, tq)), full((s, KV_WIDTH)), full((nb, 2 * IDX_DIM, tq))],
        out_specs=blk(ATTN_WIDTH),
        out_shape=jax.ShapeDtypeStruct((b, s, ATTN_WIDTH), F32),
        scratch_shapes=[pltpu.VMEM((nb, tq, tq), F32),
                        pltpu.VMEM((nb, min(128, tq), LANES), F32),
                        pltpu.VMEM((IDX_HEADS, tq, LANES), F32),
                        pltpu.VMEM((N_HEADS, tq, LANES), F32),
                        pltpu.VMEM((N_HEADS, tq, LANES), F32),
                        pltpu.VMEM((N_HEADS, tq, KV_WIDTH), F32),
                        pltpu.VMEM((tq, LANES), F32),
                        pltpu.VMEM((tq, LANES), I32)],
        compiler_params=_cparams(("arbitrary", "arbitrary")),
        name="prompt_attn",
    )(q, qi, kiwi, kt, v, kit)


def _sample_attn_kernel(pt_ref, q_ref, qih_ref, wib_ref, kin_ref, kn_ref, vn_ref, *rest,
                        n_steps, pps, n_tok, tpad, page, topk, idx_bits):
    cik_refs, ck_refs, cv_refs = rest[:pps], rest[pps:2 * pps], rest[2 * pps:3 * pps]
    o_ref, sc_ref, sa_ref, vs_ref = rest[3 * pps:]
    step = pl.program_id(1)
    n_pages = n_steps * pps
    nkb = n_pages + 1
    qih = qih_ref[...]
    wib = wib_ref[...]
    q = q_ref[...]

    def idx_score(s):
        acc = None
        for h in range(IDX_HEADS):
            t = jnp.maximum(s[h * tpad:(h + 1) * tpad], 0.0) * wib[h * tpad:(h + 1) * tpad]
            acc = t if acc is None else acc + t
        return acc

    for j in range(pps):
        pg = step * pps + j
        sc_ref[pg] = idx_score(jnp.dot(qih, cik_refs[j][...].astype(BF16), preferred_element_type=F32))
        sa_ref[pg] = jnp.dot(q, ck_refs[j][...].astype(BF16), preferred_element_type=F32)
        vs_ref[pg] = cv_refs[j][...].astype(BF16)

    @pl.when(step == n_steps - 1)
    def _():
        s_new = idx_score(jnp.dot(qih, kin_ref[...], preferred_element_type=F32))
        tok = lax.broadcasted_iota(I32, (tpad, page), 0)
        col = lax.broadcasted_iota(I32, (tpad, page), 1)
        sc_ref[n_pages] = jnp.where((col <= tok) & (col < n_tok), s_new, -jnp.inf)
        sa_ref[n_pages] = jnp.dot(q, kn_ref[...], preferred_element_type=F32)
        vs_ref[n_pages] = vn_ref[...]

        kpos = (lax.broadcasted_iota(I32, (nkb, tpad, page), 0) * page
                + lax.broadcasted_iota(I32, (nkb, tpad, page), 2))

        def cnt(pred):
            w = jnp.where(pred(sc_ref[...], kpos, slice(None)), 1.0, 0.0)
            chunk = -(-nkb // SUBLANES)
            c = sum(jnp.sum(w[a:a + chunk], axis=0) for a in range(0, nkb, chunk))
            return jnp.broadcast_to(jnp.sum(c, axis=1, keepdims=True), c.shape)

        thr, jmax = _select_rows(cnt, tpad, topk, idx_bits, tok[:, :LANES] < n_tok, two_bits=True)
        sc_ref[...] = _sel_bias(sc_ref[...], kpos, thr[None], jmax[None])

        heads = N_HEADS
        inv_l = []
        for t in range(n_tok):
            s = sa_ref[:, t * heads:(t + 1) * heads, :] + sc_ref[:, t:t + 1, :]
            m = jnp.max(jnp.max(s, axis=0), axis=1, keepdims=True)
            pr = jnp.exp2(s - m[None])
            sa_ref[:, t * heads:(t + 1) * heads, :] = pr
            l = jnp.sum(jnp.sum(pr, axis=0), axis=1, keepdims=True)
            inv_l.append(jnp.broadcast_to(1.0 / l, (heads, KV_WIDTH)))

        p_all = jnp.concatenate([sa_ref[kb].astype(BF16) for kb in range(nkb)], axis=1)
        vt_all = jnp.concatenate([vs_ref[kb] for kb in range(nkb)], axis=1)
        acc = lax.dot_general(p_all, vt_all, NT_DIMS, preferred_element_type=F32)
        o_ref[...] = acc * jnp.concatenate(inv_l, axis=0)


def _sample_attn(page_table, q, qih, wib, kin_t, kn_t, vn_t, cik_t, ck_t, cv_t, n_tok, tpad, topk, pps):
    n, n_pages = page_table.shape
    page = ck_t.shape[2]
    assert page == LANES and KV_WIDTH == LANES and n_pages % pps == 0
    n_steps = n_pages // pps
    idx_bits = int((n_pages + 1) * page).bit_length()
    rows = n_tok * N_HEADS
    seq = lambda r, w: pl.BlockSpec((None, r, w), lambda b, p, pt: (b, 0, 0))

    def paged(r, j):
        return pl.BlockSpec((None, r, page), lambda b, p, pt: (pt[b, p * pps + j], 0, 0))

    grid_spec = pltpu.PrefetchScalarGridSpec(
        num_scalar_prefetch=1,
        grid=(n, n_steps),
        in_specs=([seq(rows, KV_WIDTH), seq(IDX_HEADS * tpad, IDX_DIM), seq(IDX_HEADS * tpad, LANES),
                   seq(IDX_DIM, page), seq(KV_WIDTH, page), seq(KV_WIDTH, page)]
                  + [paged(IDX_DIM, j) for j in range(pps)]
                  + [paged(KV_WIDTH, j) for j in range(pps)]
                  + [paged(KV_WIDTH, j) for j in range(pps)]),
        out_specs=seq(rows, KV_WIDTH),
        scratch_shapes=[pltpu.VMEM((n_pages + 1, tpad, page), F32),
                        pltpu.VMEM((n_pages + 1, rows, page), F32),
                        pltpu.VMEM((n_pages + 1, KV_WIDTH, page), BF16)],
    )
    return pl.pallas_call(
        functools.partial(_sample_attn_kernel, n_steps=n_steps, pps=pps, n_tok=n_tok, tpad=tpad, page=page,
                          topk=topk, idx_bits=idx_bits),
        grid_spec=grid_spec,
        out_shape=jax.ShapeDtypeStruct((n, rows, KV_WIDTH), F32),
        compiler_params=_cparams(("arbitrary", "arbitrary")),
        name="sample_attn",
    )(page_table, q, qih, wib, kin_t, kn_t, vn_t, *([cik_t] * pps), *([ck_t] * pps), *([cv_t] * pps))


def _mix_out_tail(x, o, gb, u0, um1, um2, wc_ref, wo_ref, gt, gpost, aw):
    y = wc_ref[0:1, :] * um2 + wc_ref[1:2, :] * um1 + wc_ref[2:3, :] * u0
    mixed = (jnp.dot(o.astype(BF16), wo_ref[0:aw, :], preferred_element_type=F32)
             + jnp.dot((gb * y).astype(BF16), wo_ref[aw:, :], preferred_element_type=F32))
    return x + gt * (_rms(mixed) * gpost)


def _mix_out_prompt_kernel(x_ref, o_ref_in, gb_ref, u_ref, halo_ref, gt_ref, gpost_ref, wc_ref, wo_ref,
                           out_ref, ext_ref, *, tm, tiles_per_seq):
    i = pl.program_id(0)
    halo = halo_ref[...]
    ext_ref[0:SUBLANES, :] = jnp.where(i % tiles_per_seq == 0, jnp.zeros_like(halo), halo)
    ext_ref[SUBLANES:, :] = u_ref[...]
    um2 = ext_ref[SUBLANES - 2:SUBLANES - 2 + tm, :]
    um1 = ext_ref[SUBLANES - 1:SUBLANES - 1 + tm, :]
    out_ref[...] = _mix_out_tail(x_ref[...], o_ref_in[...], gb_ref[...], u_ref[...], um1, um2,
                                 wc_ref, wo_ref, gt_ref[0], gpost_ref[...], ATTN_WIDTH)


def _mix_out_prompt(x, o, gb, u, gate, g_post, w_conv, w_o, tm, tiles_per_seq):
    rows, d = x.shape
    cw = u.shape[1]
    const = lambda i: (0, 0)
    row = lambda i: (i, 0)
    hb = tm // SUBLANES
    return pl.pallas_call(
        functools.partial(_mix_out_prompt_kernel, tm=tm, tiles_per_seq=tiles_per_seq),
        grid=(rows // tm,),
        in_specs=[pl.BlockSpec((tm, d), row),
                  pl.BlockSpec((tm, ATTN_WIDTH), row),
                  pl.BlockSpec((tm, cw), row),
                  pl.BlockSpec((tm, cw), row),
                  pl.BlockSpec((SUBLANES, cw), lambda i: (jnp.maximum(i * hb - 1, 0), 0)),
                  _mod_spec(gate, tm, tiles_per_seq),
                  pl.BlockSpec((1, d), const),
                  pl.BlockSpec(w_conv.shape, const),
                  pl.BlockSpec(w_o.shape, const)],
        out_specs=pl.BlockSpec((tm, d), row),
        out_shape=jax.ShapeDtypeStruct((rows, d), F32),
        scratch_shapes=[pltpu.VMEM((tm + SUBLANES, cw), F32)],
        compiler_params=_cparams(("arbitrary",)),
        name="mix_out_prompt",
    )(x, o, gb, u, u, gate, g_post, w_conv, w_o)


def _mix_out_sample_kernel(x_ref, o_ref_in, gb_ref, um2_ref, um1_ref, u0_ref, gt_ref, gpost_ref, wc_ref, wo_ref,
                           out_ref):
    out_ref[...] = _mix_out_tail(x_ref[...], o_ref_in[...], gb_ref[...], u0_ref[...], um1_ref[...], um2_ref[...],
                                 wc_ref, wo_ref, gt_ref[0], gpost_ref[...], ATTN_WIDTH)


def _mix_out_sample(x, o, gb, ext, gate, g_post, w_conv, w_o, tm):
    rows, d = x.shape
    cw = gb.shape[1]
    const = lambda i: (0, 0)
    row = lambda i: (i, 0)
    return pl.pallas_call(
        _mix_out_sample_kernel,
        grid=(rows // tm,),
        in_specs=[pl.BlockSpec((tm, d), row),
                  pl.BlockSpec((tm, ATTN_WIDTH), row),
                  pl.BlockSpec((tm, cw), row),
                  pl.BlockSpec((tm, cw), lambda i: (i, 0)),
                  pl.BlockSpec((tm, cw), lambda i: (i + 1, 0)),
                  pl.BlockSpec((tm, cw), lambda i: (i + 2, 0)),
                  pl.BlockSpec((1, tm, d), lambda i: (0, 0, 0)),
                  pl.BlockSpec((1, d), const),
                  pl.BlockSpec(w_conv.shape, const),
                  pl.BlockSpec(w_o.shape, const)],
        out_specs=pl.BlockSpec((tm, d), row),
        out_shape=jax.ShapeDtypeStruct((rows, d), F32),
        compiler_params=_cparams(("arbitrary",)),
        name="mix_out_sample",
    )(x, o, gb, ext, ext, ext, gate, g_post, w_conv, w_o)


def _prompt_layer(x, mod, lw, tm, tq):
    b, s, d = x.shape
    conv_ch = lw["w_conv"].shape[1]
    tps = s // tm
    md = lambda j, k: mod[:, j, k][:, None, :]
    xr = x.reshape(b * s, d)
    x1 = _ffn(xr, md(0, 0), md(0, 1), md(0, 2), lw["g_pre"][0:1], lw["g_post"][0:1], lw["ffn1_w13"], lw["ffn1_w2"],
              tm, tps)
    tabs = _rope_tables(jnp.arange(s))
    q, k_t, v_t, v_bf, qi, kiwi, kiwi_t, gb, u = _mix_in(x1, md(1, 0), md(1, 1), lw["g_pre"][1:2], lw["w_in_p"], tabs,
                                                         tm, tps, tps, conv_ch)
    nb = s // tq
    ki_t = kiwi_t[:, :IDX_DIM]
    blocks = lambda a: a.astype(BF16).reshape(b, a.shape[1], nb, tq).swapaxes(1, 2)
    kt = blocks(k_t)
    kit = blocks(jnp.concatenate([ki_t, ki_t], axis=1))
    topk = min(TOPK_MAX, s // 4)
    o = _prompt_attn(q, qi.reshape(b, s, -1), kiwi.reshape(b, s, -1), kt, v_bf.reshape(b, s, KV_WIDTH), kit, tq, topk)
    x2 = _mix_out_prompt(x1, o.reshape(b * s, -1), gb, u, md(1, 2), lw["g_post"][1:2], lw["w_conv"], lw["w_o"],
                         tm, tps)
    x3 = _ffn(x2, md(2, 0), md(2, 1), md(2, 2), lw["g_pre"][2:3], lw["g_post"][2:3], lw["ffn2_w13"], lw["ffn2_w2"],
              tm, tps)
    heads_last = lambda a: a.reshape(b, N_KV_HEADS, HEAD_DIM, s).transpose(0, 3, 1, 2)
    state = (heads_last(k_t), heads_last(v_t), ki_t.transpose(0, 2, 1),
             u.reshape(b, s, conv_ch)[:, s - (CONV_K - 1):])
    return x3.reshape(b, s, d), state


def _sample_layer(x_tm, mod, lw, cache_k, cache_v, cache_idx_k, state_conv, page_table):
    n = mod.shape[0]
    t_tok = x_tm.shape[0] // n
    d = x_tm.shape[1]
    conv_ch = lw["w_conv"].shape[1]
    n_pages = page_table.shape[1]
    page = cache_k.shape[1]
    past = n_pages * page
    md = lambda j, k: mod[:, j, k][None]
    x1 = _ffn(x_tm, md(0, 0), md(0, 1), md(0, 2), lw["g_pre"][0:1], lw["g_post"][0:1], lw["ffn1_w13"],
              lw["ffn1_w2"], n, t_tok)
    pos = jnp.repeat(past + jnp.arange(t_tok), n)
    tabs = _rope_tables(pos)
    q, k_t, v_t, _, qi, kiwi, kiwi_t, gb, u = _mix_in(x1, md(1, 0), md(1, 1), lw["g_pre"][1:2], lw["w_in_p"], tabs,
                                                      n, t_tok, t_tok, conv_ch)
    per_seq_t = lambda a: a[0].reshape(a.shape[1], t_tok, n).transpose(2, 0, 1)
    k_s, v_s, ki_s = per_seq_t(k_t), per_seq_t(v_t), per_seq_t(kiwi_t[:, :IDX_DIM])
    tpad = SUBLANES
    assert t_tok <= tpad

    def seq_major(a, heads, width):
        a = a.reshape(t_tok, n, heads, width).transpose(1, 2, 0, 3)
        a = jnp.pad(a, ((0, 0), (0, 0), (0, tpad - t_tok), (0, 0)))
        return a.reshape(n, heads * tpad, width)

    q_seq = q.reshape(N_HEADS, t_tok, n, LANES).transpose(2, 1, 0, 3).reshape(n, t_tok * N_HEADS, LANES)
    qi_nat = jnp.concatenate([qi[:, h * LANES + (h % 2) * IDX_DIM:h * LANES + (h % 2 + 1) * IDX_DIM]
                              for h in range(IDX_HEADS)], axis=1)
    qih = seq_major(qi_nat, IDX_HEADS, IDX_DIM)
    wib = jnp.broadcast_to(seq_major(kiwi[:, IDX_DIM:IDX_DIM + IDX_HEADS], IDX_HEADS, 1),
                           (n, IDX_HEADS * tpad, LANES))

    def new_keys_t(a):
        return jnp.pad(a.astype(BF16), ((0, 0), (0, 0), (0, page - t_tok)))

    ck_t = cache_k.transpose(0, 2, 3, 1).reshape(-1, KV_WIDTH, page)
    cv_t = cache_v.transpose(0, 2, 3, 1).reshape(-1, KV_WIDTH, page)
    cik_t = cache_idx_k.transpose(0, 2, 1)
    topk = min(TOPK_MAX, (past + t_tok) // 4)
    pps = max(p for p in (16, 8, 4, 2, 1) if n_pages % p == 0)
    o = _sample_attn(page_table, q_seq, qih, wib, new_keys_t(ki_s), new_keys_t(k_s), new_keys_t(v_s),
                     cik_t, ck_t, cv_t, t_tok, tpad, topk, pps)
    o = o.reshape(n, t_tok, N_KV_HEADS, HEADS_PER_KV, N_KV_HEADS, HEAD_DIM)
    o = jnp.stack([o[:, :, g, :, g] for g in range(N_KV_HEADS)], axis=2)
    o_tm = o.transpose(1, 0, 2, 3, 4).reshape(t_tok * n, ATTN_WIDTH)
    ext = jnp.concatenate([state_conv.transpose(1, 0, 2).reshape(-1, conv_ch), u], axis=0)
    x2 = _mix_out_sample(x1, o_tm, gb, ext, md(1, 2), lw["g_post"][1:2], lw["w_conv"], lw["w_o"], n)
    x3 = _ffn(x2, md(2, 0), md(2, 1), md(2, 2), lw["g_pre"][2:3], lw["g_post"][2:3], lw["ffn2_w13"],
              lw["ffn2_w2"], n, t_tok)
    tm_to_seq = lambda a: a.reshape((t_tok, n) + a.shape[1:]).swapaxes(0, 1)
    heads_last = lambda a: a.transpose(0, 2, 1).reshape(n, t_tok, N_KV_HEADS, HEAD_DIM)
    state = (heads_last(k_s), heads_last(v_s), ki_s.transpose(0, 2, 1), tm_to_seq(u)[:, t_tok - (CONV_K - 1):])
    return x3, state


def _row_tile(s, target):
    t = min(target, s)
    while s % t:
        t //= 2
    return t


def kernel(x_prompt, x_sample, cache_k, cache_v, cache_idx_k, state_conv, page_table, c_prompt, c_sample, w_ada, b_ada, g_pre, g_post, ffn1_w13, ffn1_w2, w_in, w_conv, w_o, ffn2_w13, ffn2_w2):
    depth = w_ada.shape[0]
    b, s, d = x_prompt.shape
    n, t_tok, _ = x_sample.shape
    conv_ch = w_conv.shape[2]
    tm = _row_tile(s, 512)
    tq = _row_tile(s, 256)

    yp = x_prompt
    ys = x_sample.swapaxes(0, 1).reshape(t_tok * n, d)
    c_all = jnp.concatenate([c_prompt, c_sample], axis=0)
    n_c = c_all.shape[0]
    c_all = jnp.pad(c_all, ((0, -n_c % SUBLANES), (0, 0)))
    outs = [[] for _ in range(8)]
    for l in range(depth):
        lw = dict(g_pre=g_pre[l], g_post=g_post[l], w_conv=w_conv[l],
                  ffn1_w13=ffn1_w13[l].astype(BF16), ffn1_w2=ffn1_w2[l].astype(BF16),
                  ffn2_w13=ffn2_w13[l].astype(BF16), ffn2_w2=ffn2_w2[l].astype(BF16),
                  w_in_p=_pack_w_in(w_in[l], conv_ch), w_o=w_o[l].astype(BF16))
        mod = _ada_mod(c_all, w_ada[l], b_ada[l])[:n_c].reshape(n_c, N_SUB, 3, d)
        yp, st_p = _prompt_layer(yp, mod[:b], lw, tm, tq)
        ys, st_s = _sample_layer(ys, mod[b:], lw, cache_k[l], cache_v[l], cache_idx_k[l], state_conv[l], page_table)
        for lst, a in zip(outs, st_p + st_s):
            lst.append(a)
    ys = ys.reshape(t_tok, n, d).swapaxes(0, 1)
    return (yp, ys) + tuple(jnp.stack(lst) for lst in outs)
```
